```python
import math
import jax, jax.numpy as jnp
from jax import lax
import numpy as np

D_MODEL = 1024
BATCH = 2
SEQ = 16384
DEPTH = 1

N_Q_HEADS = 8
N_KV_HEADS = 2
HEAD_DIM = 64
ATTN_WIDTH = N_Q_HEADS * HEAD_DIM
WINDOW = 128
BLOCK = 128
CONV_WIDTH = D_MODEL - ATTN_WIDTH
CONV_K = 3
QKV_COLS = (N_Q_HEADS + 2 * N_KV_HEADS) * HEAD_DIM
IN_COLS = QKV_COLS + 3 * CONV_WIDTH
N_EXPERTS = 32
TOP_K = 4
D_FF = D_MODEL
SWIGLU_ALPHA = 1.702
SWIGLU_LIMIT = 7.0
MOE_BLOCK = 128
EPS = 1e-5

kernel_name = "hymba_conv_swa_sink_alibi_moe"


def rmsnorm(x, g):
    x32 = x.astype(jnp.float32)
    y = x32 * lax.rsqrt(jnp.mean(x32 * x32, axis=-1, keepdims=True) + EPS)
    return (y * g.astype(jnp.float32)).astype(x.dtype)


def alibi_slopes(n_heads):
    return jnp.exp2(-8.0 * jnp.arange(1, n_heads + 1, dtype=jnp.float32) / n_heads)


def sliding_window_attention(q, k, v, sinks):
    bsz, s_len = q.shape[0], q.shape[1]
    nb = s_len // BLOCK
    grp = N_Q_HEADS // N_KV_HEADS
    qb = q.reshape(bsz, nb, BLOCK, N_KV_HEADS, grp, HEAD_DIM)
    kb = k.reshape(bsz, nb, BLOCK, N_KV_HEADS, HEAD_DIM)
    vb = v.reshape(bsz, nb, BLOCK, N_KV_HEADS, HEAD_DIM)

    def band(t):
        prev = jnp.concatenate([jnp.zeros_like(t[:, :1]), t[:, :-1]], axis=1)
        return jnp.concatenate([prev, t], axis=2)

    kband, vband = band(kb), band(vb)
    scale = 1.0 / math.sqrt(HEAD_DIM)
    s = jnp.einsum('bnqhgd,bnshd->bnhgqs', qb, kband,
                   preferred_element_type=jnp.float32) * scale

    qi = jnp.arange(BLOCK)[:, None]
    kj = jnp.arange(2 * BLOCK)[None, :]
    dist = qi + BLOCK - kj
    kpos = jnp.arange(nb)[:, None, None] * BLOCK + kj[None] - BLOCK
    valid = (dist >= 0) & (dist < WINDOW) & (kpos >= 0)

    slopes = alibi_slopes(N_Q_HEADS).reshape(N_KV_HEADS, grp)
    s = s - slopes[:, :, None, None] * dist.astype(jnp.float32)
    s = jnp.where(valid[None, :, None, None], s, -jnp.inf)

    sink = sinks.astype(jnp.float32).reshape(N_KV_HEADS, grp)[None, None, :, :, None, None]
    sink = jnp.broadcast_to(sink, s.shape[:-1] + (1,))
    p = jax.nn.softmax(jnp.concatenate([s, sink], axis=-1), axis=-1)[..., :-1]
    o = jnp.einsum('bnhgqs,bnshd->bnqhgd', p.astype(v.dtype), vband)
    return o.reshape(bsz, s_len, ATTN_WIDTH)


def short_conv(b_gate, c_gate, x_in, conv_w):
    u = c_gate * x_in
    s_len = u.shape[1]
    u_pad = jnp.pad(u, ((0, 0), (CONV_K - 1, 0), (0, 0)))
    y = (conv_w[0] * u_pad[:, 0:s_len]
         + conv_w[1] * u_pad[:, 1:s_len + 1]
         + conv_w[2] * u_pad[:, 2:s_len + 2])
    return b_gate * y


def clamped_swiglu(hu):
    x_glu, x_lin = hu[..., :D_FF], hu[..., D_FF:]
    x_glu = jnp.minimum(x_glu, SWIGLU_LIMIT)
    x_lin = jnp.clip(x_lin, -SWIGLU_LIMIT, SWIGLU_LIMIT)
    return x_glu * jax.nn.sigmoid(SWIGLU_ALPHA * x_glu) * (x_lin + 1.0)


def moe(h, w_router, b_router, w1, b1, w2, b2):
    bsz, s_len, d = h.shape
    n_tok = bsz * s_len
    xt = h.reshape(n_tok, d)
    logits = (xt @ w_router + b_router).astype(jnp.float32)
    top_val, top_idx = lax.top_k(logits, TOP_K)
    gates = jax.nn.softmax(top_val, axis=-1).astype(h.dtype)

    nk = n_tok * TOP_K
    e_flat = top_idx.reshape(nk)
    tok_flat = jnp.arange(nk, dtype=jnp.int32) // TOP_K
    g_flat = gates.reshape(nk)
    order = jnp.argsort(e_flat)
    e_sorted, tok_sorted, g_sorted = e_flat[order], tok_flat[order], g_flat[order]

    counts = jnp.bincount(e_flat, length=N_EXPERTS)
    padded = (counts + MOE_BLOCK - 1) // MOE_BLOCK * MOE_BLOCK
    starts = jnp.cumsum(counts) - counts
    pstarts = jnp.cumsum(padded) - padded
    pends = pstarts + padded
    dest = pstarts[e_sorted] + (jnp.arange(nk, dtype=jnp.int32) - starts[e_sorted])

    n_rows = (nk + MOE_BLOCK - 1) // MOE_BLOCK * MOE_BLOCK + N_EXPERTS * MOE_BLOCK
    n_blocks = n_rows // MOE_BLOCK
    row_tok = jnp.full((n_rows,), n_tok, jnp.int32).at[dest].set(tok_sorted)
    row_gate = jnp.zeros((n_rows,), h.dtype).at[dest].set(g_sorted)
    blk_start = jnp.arange(n_blocks, dtype=jnp.int32) * MOE_BLOCK
    blk_expert = jnp.minimum(jnp.searchsorted(pends, blk_start, side='right'), N_EXPERTS - 1)

    x_pad = jnp.concatenate([xt, jnp.zeros((1, d), xt.dtype)], axis=0)
    x_rows = x_pad[row_tok].reshape(n_blocks, MOE_BLOCK, d)

    def expert_block(args):
        xb, e = args
        act = clamped_swiglu(xb @ w1[e] + b1[e])
        return act @ w2[e] + b2[e]

    y_rows = lax.map(expert_block, (x_rows, blk_expert)).reshape(n_rows, d)
    out = jax.ops.segment_sum(y_rows * row_gate[:, None], row_tok, num_segments=n_tok + 1)[:n_tok]
    return out.reshape(bsz, s_len, d)


def setup_inputs(seed: int = 0) -> dict:
    key = jax.random.key(seed)
    ks = jax.random.split(key, 16)
    f32 = jnp.float32
    nrm = lambda k, shape, scale: jax.random.normal(k, shape, f32) * scale
    return {
        'x': nrm(ks[0], (BATCH, SEQ, D_MODEL), 1.0),
        'g_mix': 1.0 + nrm(ks[1], (DEPTH, D_MODEL), 0.02),
        'w_in': nrm(ks[2], (DEPTH, D_MODEL, IN_COLS), D_MODEL ** -0.5),
        'conv_w': nrm(ks[3], (DEPTH, CONV_K, CONV_WIDTH), CONV_K ** -0.5),
        'sinks': nrm(ks[4], (DEPTH, N_Q_HEADS), 0.5),
        'g_attn_out': 1.0 + nrm(ks[5], (DEPTH, ATTN_WIDTH), 0.02),
        'g_conv_out': 1.0 + nrm(ks[6], (DEPTH, CONV_WIDTH), 0.02),
        'w_out': nrm(ks[7], (DEPTH, D_MODEL, D_MODEL), D_MODEL ** -0.5),
        'g_ffn': 1.0 + nrm(ks[8], (DEPTH, D_MODEL), 0.02),
        'w_router': nrm(ks[9], (DEPTH, D_MODEL, N_EXPERTS), D_MODEL ** -0.5),
        'b_router': nrm(ks[10], (DEPTH, N_EXPERTS), 0.01),
        'w1': nrm(ks[11], (DEPTH, N_EXPERTS, D_MODEL, 2 * D_FF), D_MODEL ** -0.5),
        'b1': nrm(ks[12], (DEPTH, N_EXPERTS, 2 * D_FF), 0.02),
        'w2': nrm(ks[13], (DEPTH, N_EXPERTS, D_FF, D_MODEL), D_FF ** -0.5),
        'b2': nrm(ks[14], (DEPTH, N_EXPERTS, D_MODEL), 0.02),
        'g_final': 1.0 + nrm(ks[15], (D_MODEL,), 0.02),
    }


def reference(x, g_mix, w_in, conv_w, sinks, g_attn_out, g_conv_out, w_out,
              g_ffn, w_router, b_router, w1, b1, w2, b2, g_final):
    bsz, s_len, _ = x.shape
    split_at = [ATTN_WIDTH, ATTN_WIDTH + N_KV_HEADS * HEAD_DIM, QKV_COLS,
                QKV_COLS + CONV_WIDTH, QKV_COLS + 2 * CONV_WIDTH]
    for l in range(DEPTH):
        h = rmsnorm(x, g_mix[l])
        p = h @ w_in[l]
        q, k, v, b_gate, c_gate, x_in = jnp.split(p, split_at, axis=-1)
        q = q.reshape(bsz, s_len, N_Q_HEADS, HEAD_DIM)
        k = k.reshape(bsz, s_len, N_KV_HEADS, HEAD_DIM)
        v = v.reshape(bsz, s_len, N_KV_HEADS, HEAD_DIM)
        attn = sliding_window_attention(q, k, v, sinks[l])
        conv = short_conv(b_gate, c_gate, x_in, conv_w[l])
        mix = jnp.concatenate([rmsnorm(attn, g_attn_out[l]),
                               rmsnorm(conv, g_conv_out[l])], axis=-1)
        x = x + mix @ w_out[l]
        h = rmsnorm(x, g_ffn[l])
        x = x + moe(h, w_router[l], b_router[l], w1[l], b1[l], w2[l], b2[l])
    return rmsnorm(x, g_final)
```

```python
import functools
import math

import jax
import jax.numpy as jnp
from jax import lax
from jax.experimental import pallas as pl
from jax.experimental.pallas import tpu as pltpu

D_MODEL = 1024
N_Q_HEADS = 8
N_KV_HEADS = 2
HEAD_DIM = 64
GROUP = N_Q_HEADS // N_KV_HEADS
ATTN_WIDTH = N_Q_HEADS * HEAD_DIM
KV_WIDTH = N_KV_HEADS * HEAD_DIM
CONV_WIDTH = D_MODEL - ATTN_WIDTH
QKV_COLS = ATTN_WIDTH + 2 * KV_WIDTH
IN_COLS = QKV_COLS + 3 * CONV_WIDTH
WINDOW = 128
BLOCK = 128
BAND = 2 * BLOCK
N_EXPERTS = 32
TOP_K = 4
D_FF = D_MODEL
SWIGLU_ALPHA = 1.702
SWIGLU_LIMIT = 7.0
EPS = 1e-5

LANES = 128
SUBLANES = 8
VMEM_LIMIT_BYTES = 56 * 1024 * 1024

TOKEN_TILE = 512
ROW_BLOCK = 256
SCATTER_TILE = 1024
COMBINE_TILE = 256
SEM_RING = 32


def _rms(x, g):
    return x * lax.rsqrt(jnp.mean(x * x, axis=-1, keepdims=True) + EPS) * g


def _block_diag4(lo_tile, hi_tile):
    z = jnp.zeros_like(lo_tile)
    return jnp.concatenate([
        jnp.concatenate([lo_tile, z], axis=1),
        jnp.concatenate([hi_tile, z], axis=1),
        jnp.concatenate([z, lo_tile], axis=1),
        jnp.concatenate([z, hi_tile], axis=1)], axis=0)


def _mixer_kernel(tiles_per_seq,
                  sink_ref,
                  x_ref, gmix_ref, win_ref, convw_ref, bias_ref, gattn_ref, gconv_ref, wout_ref,
                  gffn_ref, wrt_ref, br_ref,
                  x1_ref, h2_ref, eid_ref, gate_ref, rank_ref, cnt_ref,
                  p_scr, kv_scr, u_scr, attn_scr, tri_scr, cnt_scr):
    tm = x_ref.shape[0]
    i = pl.program_id(0)
    first = (i % tiles_per_seq) == 0
    f32, bf16 = jnp.float32, jnp.bfloat16

    @pl.when(i == 0)
    def _():
        cnt_scr[...] = jnp.zeros_like(cnt_scr)
        r = lax.broadcasted_iota(jnp.int32, (tm, tm), 0)
        c = lax.broadcasted_iota(jnp.int32, (tm, tm), 1)
        tri_scr[...] = jnp.where(r < c, 1.0, 0.0).astype(bf16)

    @pl.when(first)
    def _():
        kv_scr[0:BLOCK, :] = jnp.zeros((BLOCK, kv_scr.shape[1]), bf16)
        u_scr[0:SUBLANES, :] = jnp.zeros((SUBLANES, CONV_WIDTH), f32)

    x = x_ref[...]
    h = _rms(x, gmix_ref[...])
    p_scr[...] = jnp.dot(h.astype(bf16), win_ref[...], preferred_element_type=f32)

    kk = p_scr[:, ATTN_WIDTH:ATTN_WIDTH + KV_WIDTH]
    vv = p_scr[:, ATTN_WIDTH + KV_WIDTH:QKV_COLS]
    kr = pltpu.roll(kk, HEAD_DIM, axis=1)
    vr = pltpu.roll(vv, HEAD_DIM, axis=1)
    kv_scr[BLOCK:BLOCK + tm, :] = jnp.concatenate([kk, kr, vv, vr], axis=1).astype(bf16)

    lane = lax.broadcasted_iota(jnp.int32, (BAND, LANES), 1)
    lo = lane < HEAD_DIM
    col = lax.broadcasted_iota(jnp.int32, (BLOCK, GROUP * BAND), 1)
    prev_half = (col % BAND) < BLOCK
    scale = 1.0 / math.sqrt(HEAD_DIM)
    zero_t = jnp.zeros((BAND, LANES), bf16)

    for j in range(tm // BLOCK):
        band = kv_scr[j * BLOCK:j * BLOCK + BAND, :]
        bk, bkr = band[:, 0:LANES], band[:, LANES:2 * LANES]
        bv, bvr = band[:, 2 * LANES:3 * LANES], band[:, 3 * LANES:4 * LANES]
        for hh in range(N_KV_HEADS):
            if hh == 0:
                k_lo, k_hi = jnp.where(lo, bk, zero_t), jnp.where(lo, zero_t, bkr)
                v_lo, v_hi = jnp.where(lo, bv, zero_t), jnp.where(lo, zero_t, bvr)
            else:
                k_lo, k_hi = jnp.where(lo, bkr, zero_t), jnp.where(lo, zero_t, bk)
                v_lo, v_hi = jnp.where(lo, bvr, zero_t), jnp.where(lo, zero_t, bv)
            k4 = _block_diag4(k_lo, k_hi)
            v4 = _block_diag4(v_lo, v_hi)
            qh = (p_scr[j * BLOCK:(j + 1) * BLOCK, hh * 2 * LANES:(hh + 1) * 2 * LANES] * scale).astype(bf16)
            s = lax.dot_general(qh, k4, (((1,), (1,)), ((), ())), preferred_element_type=f32)
            s = s + bias_ref[hh]
            if j == 0:
                s = jnp.where(jnp.logical_and(first, prev_half), -jnp.inf, s)
            probs = []
            for g in range(GROUP):
                sg = s[:, g * BAND:(g + 1) * BAND]
                sink = sink_ref[hh * GROUP + g]
                m = jnp.maximum(jnp.max(sg, axis=-1, keepdims=True), sink)
                e = jnp.exp(sg - m)
                den = jnp.sum(e, axis=-1, keepdims=True) + jnp.exp(sink - m)
                probs.append((e * (1.0 / den)).astype(bf16))
            pmat = jnp.concatenate(probs, axis=1)
            o = jnp.dot(pmat, v4, preferred_element_type=f32)
            attn_scr[j * BLOCK:(j + 1) * BLOCK, hh * 2 * LANES:(hh + 1) * 2 * LANES] = o

    bg = p_scr[:, QKV_COLS:QKV_COLS + CONV_WIDTH]
    cg = p_scr[:, QKV_COLS + CONV_WIDTH:QKV_COLS + 2 * CONV_WIDTH]
    xin = p_scr[:, QKV_COLS + 2 * CONV_WIDTH:IN_COLS]
    u = cg * xin
    u_scr[SUBLANES:SUBLANES + tm, :] = u
    u1 = u_scr[SUBLANES - 1:SUBLANES - 1 + tm, :]
    u2 = u_scr[SUBLANES - 2:SUBLANES - 2 + tm, :]
    y = convw_ref[0:1, :] * u2 + convw_ref[1:2, :] * u1 + convw_ref[2:3, :] * u
    conv = bg * y
    kv_scr[0:BLOCK, :] = kv_scr[tm:tm + BLOCK, :]
    u_scr[0:SUBLANES, :] = u_scr[tm:tm + SUBLANES, :]

    mix = jnp.concatenate([_rms(attn_scr[...], gattn_ref[...]), _rms(conv, gconv_ref[...])], axis=1)
    x1 = x + jnp.dot(mix.astype(bf16), wout_ref[...], preferred_element_type=f32)
    x1_ref[...] = x1
    h2 = _rms(x1, gffn_ref[...])
    h2_ref[...] = h2

    logits = lax.dot_general(wrt_ref[...], h2, (((1,), (1,)), ((), ())),
                             precision=lax.Precision.HIGHEST, preferred_element_type=f32) + br_ref[...]
    row = lax.broadcasted_iota(jnp.int32, (N_EXPERTS, tm), 0)
    vals, idxs = [], []
    l = logits
    for _ in range(TOP_K):
        m = jnp.max(l, axis=0, keepdims=True)
        idx = jnp.min(jnp.where(l == m, row, N_EXPERTS), axis=0, keepdims=True)
        vals.append(m)
        idxs.append(idx)
        l = jnp.where(row == idx, -jnp.inf, l)
    es = [jnp.exp(v - vals[0]) for v in vals]
    inv = 1.0 / (es[0] + es[1] + es[2] + es[3])
    gate_ref[...] = jnp.concatenate([e * inv for e in es], axis=0)
    eid_ref[...] = jnp.concatenate(idxs, axis=0)

    onehots = [row == idx for idx in idxs]
    sel = jnp.logical_or(jnp.logical_or(onehots[0], onehots[1]), jnp.logical_or(onehots[2], onehots[3]))
    oh = jnp.where(sel, 1.0, 0.0)
    base = cnt_scr[...][:, 0:1]
    cum = jnp.dot(oh.astype(bf16), tri_scr[...], preferred_element_type=f32) + base
    ranks = [jnp.sum(jnp.where(o, cum, 0.0), axis=0, keepdims=True) for o in onehots]
    rank_ref[...] = jnp.concatenate(ranks, axis=0).astype(jnp.int32)
    new_cnt = base + jnp.sum(oh, axis=1, keepdims=True)
    cnt_scr[...] = jnp.broadcast_to(new_cnt, cnt_scr.shape)
    cnt_ref[...] = jnp.broadcast_to(new_cnt, cnt_ref.shape).astype(jnp.int32)


def _mixer_call(x2d, sinks, g_mix, w_in, conv_w, bias, g_attn, g_conv, w_out, g_ffn, wrt, br, seq_len):
    n_tok = x2d.shape[0]
    tm = TOKEN_TILE
    n_tiles = n_tok // tm
    const = lambda shape: pl.BlockSpec(shape, lambda i, *_: (0,) * len(shape))
    grid_spec = pltpu.PrefetchScalarGridSpec(
        num_scalar_prefetch=0,
        grid=(n_tiles,),
        in_specs=[
            pl.BlockSpec(memory_space=pltpu.SMEM),
            pl.BlockSpec((tm, D_MODEL), lambda i: (i, 0)),
            const((1, D_MODEL)), const((D_MODEL, IN_COLS)), const((3, CONV_WIDTH)),
            const((N_KV_HEADS, BLOCK, GROUP * BAND)),
            const((1, ATTN_WIDTH)), const((1, CONV_WIDTH)), const((D_MODEL, D_MODEL)),
            const((1, D_MODEL)), const((N_EXPERTS, D_MODEL)), const((N_EXPERTS, 1)),
        ],
        out_specs=[
            pl.BlockSpec((tm, D_MODEL), lambda i: (i, 0)),
            pl.BlockSpec((tm, D_MODEL), lambda i: (i, 0)),
            pl.BlockSpec((TOP_K, tm), lambda i: (0, i)),
            pl.BlockSpec((TOP_K, tm), lambda i: (0, i)),
            pl.BlockSpec((TOP_K, tm), lambda i: (0, i)),
            pl.BlockSpec((N_EXPERTS, LANES), lambda i: (0, 0)),
        ],
        scratch_shapes=[
            pltpu.VMEM((tm, IN_COLS), jnp.float32),
            pltpu.VMEM((BLOCK + tm, 4 * LANES), jnp.bfloat16),
            pltpu.VMEM((SUBLANES + tm, CONV_WIDTH), jnp.float32),
            pltpu.VMEM((tm, ATTN_WIDTH), jnp.float32),
            pltpu.VMEM((tm, tm), jnp.bfloat16),
            pltpu.VMEM((N_EXPERTS, LANES), jnp.float32),
        ])
    out_shape = [
        jax.ShapeDtypeStruct((n_tok, D_MODEL), jnp.float32),
        jax.ShapeDtypeStruct((n_tok, D_MODEL), jnp.float32),
        jax.ShapeDtypeStruct((TOP_K, n_tok), jnp.int32),
        jax.ShapeDtypeStruct((TOP_K, n_tok), jnp.float32),
        jax.ShapeDtypeStruct((TOP_K, n_tok), jnp.int32),
        jax.ShapeDtypeStruct((N_EXPERTS, LANES), jnp.int32),
    ]
    return pl.pallas_call(
        functools.partial(_mixer_kernel, seq_len // tm),
        grid_spec=grid_spec, out_shape=out_shape, name="mixer_router",
        compiler_params=pltpu.CompilerParams(dimension_semantics=("arbitrary",),
                                             vmem_limit_bytes=VMEM_LIMIT_BYTES),
    )(sinks, x2d, g_mix, w_in, conv_w, bias, g_attn, g_conv, w_out, g_ffn, wrt, br)


def _expert_starts(cnt_ref, pstart_ref):
    def body(e, pos):
        pstart_ref[e] = pos
        return pos + (cnt_ref[e] + ROW_BLOCK - 1) // ROW_BLOCK * ROW_BLOCK
    return lax.fori_loop(0, N_EXPERTS, body, jnp.int32(0))


def _row_copy(src, dst, sem):
    return pltpu.make_async_copy(src, dst, sem)


def _scatter_kernel(n_blocks,
                    cnt_ref,
                    h2_ref, eid_ref, rank_ref, zeros_ref,
                    rows_ref, tbl_ref,
                    pstart_ref, sems, zsem):
    i = pl.program_id(0)
    tg = h2_ref.shape[0]

    @pl.when(i == 0)
    def _():
        total = _expert_starts(cnt_ref, pstart_ref)
        def fill(e, carry):
            pos, last = carry
            nb = (cnt_ref[e] + ROW_BLOCK - 1) // ROW_BLOCK
            def put(b, _):
                tbl_ref[pos + b] = e
                return 0
            lax.fori_loop(0, nb, put, 0)
            return pos + nb, jnp.where(nb > 0, e, last)
        used, last = lax.fori_loop(0, N_EXPERTS, fill, (jnp.int32(0), jnp.int32(0)))
        def tail(b, _):
            tbl_ref[b] = last
            return 0
        lax.fori_loop(used, n_blocks, tail, 0)
        tbl_ref[n_blocks] = used
        del total
        def zfill(e, _):
            c = cnt_ref[e]
            @pl.when(c > 0)
            def _():
                end = pstart_ref[e] + (c + ROW_BLOCK - 1) // ROW_BLOCK * ROW_BLOCK
                start = pl.multiple_of(end - ROW_BLOCK, ROW_BLOCK)
                cp = pltpu.make_async_copy(zeros_ref, rows_ref.at[pl.ds(start, ROW_BLOCK)], zsem)
                cp.start()
                cp.wait()
            return 0
        lax.fori_loop(0, N_EXPERTS, zfill, 0)
        def ztail(b, _):
            cp = pltpu.make_async_copy(
                zeros_ref, rows_ref.at[pl.ds(pl.multiple_of(b * ROW_BLOCK, ROW_BLOCK), ROW_BLOCK)], zsem)
            cp.start()
            cp.wait()
            return 0
        lax.fori_loop(used, n_blocks, ztail, 0)

    n_dma = tg * TOP_K

    def copy_for(q):
        t = q // TOP_K
        k = q % TOP_K
        dst = pstart_ref[eid_ref[k, t]] + rank_ref[k, t]
        return _row_copy(h2_ref.at[pl.ds(t, 1)], rows_ref.at[pl.ds(dst, 1)], sems.at[q % SEM_RING])

    def issue(q, _):
        @pl.when(q >= SEM_RING)
        def _():
            copy_for(q - SEM_RING).wait()
        copy_for(q).start()
        return 0
    lax.fori_loop(0, n_dma, issue, 0)

    def drain(q, _):
        copy_for(q).wait()
        return 0
    lax.fori_loop(n_dma - SEM_RING, n_dma, drain, 0)


def _scatter_call(cnt, h2, eid, rank, n_rows):
    n_tok = h2.shape[0]
    tg = SCATTER_TILE
    n_blocks = n_rows // ROW_BLOCK
    zeros = jnp.zeros((ROW_BLOCK, D_MODEL), jnp.float32)
    grid_spec = pltpu.PrefetchScalarGridSpec(
        num_scalar_prefetch=1,
        grid=(n_tok // tg,),
        in_specs=[
            pl.BlockSpec((tg, D_MODEL), lambda i, c: (i, 0)),
            pl.BlockSpec((TOP_K, tg), lambda i, c: (0, i), memory_space=pltpu.SMEM),
            pl.BlockSpec((TOP_K, tg), lambda i, c: (0, i), memory_space=pltpu.SMEM),
            pl.BlockSpec(memory_space=pl.ANY),
        ],
        out_specs=[
            pl.BlockSpec(memory_space=pl.ANY),
            pl.BlockSpec(memory_space=pltpu.SMEM),
        ],
        scratch_shapes=[
            pltpu.SMEM((N_EXPERTS,), jnp.int32),
            pltpu.SemaphoreType.DMA((SEM_RING,)),
            pltpu.SemaphoreType.DMA(()),
        ])
    return pl.pallas_call(
        functools.partial(_scatter_kernel, n_blocks),
        grid_spec=grid_spec,
        out_shape=[jax.ShapeDtypeStruct((n_rows, D_MODEL), jnp.float32),
                   jax.ShapeDtypeStruct((n_blocks + 1,), jnp.int32)],
        name="row_scatter",
        compiler_params=pltpu.CompilerParams(dimension_semantics=("arbitrary",),
                                             vmem_limit_bytes=VMEM_LIMIT_BYTES),
    )(cnt, h2, eid, rank, zeros)


def _expert_kernel(n_blocks, tbl_ref, x_ref, w1_ref, b1_ref, w2_ref, b2_ref, y_ref, w1_bf, w2_bf, prev_e):
    i = pl.program_id(0)
    bf16 = jnp.bfloat16
    e = tbl_ref[i]

    @pl.when(i == 0)
    def _():
        prev_e[0] = jnp.int32(-1)

    @pl.when(e != prev_e[0])
    def _():
        w1_bf[...] = w1_ref[0].astype(bf16)
        w2_bf[...] = w2_ref[0].astype(bf16)
        prev_e[0] = e

    @pl.when(i >= tbl_ref[n_blocks])
    def _():
        y_ref[...] = jnp.zeros_like(y_ref)

    @pl.when(i < tbl_ref[n_blocks])
    def _():
        hu = jnp.dot(x_ref[...].astype(bf16), w1_bf[...], preferred_element_type=jnp.float32) + b1_ref[0]
        glu = jnp.minimum(hu[:, :D_FF], SWIGLU_LIMIT)
        lin = jnp.clip(hu[:, D_FF:], -SWIGLU_LIMIT, SWIGLU_LIMIT)
        act = glu * (1.0 / (1.0 + jnp.exp(-SWIGLU_ALPHA * glu))) * (lin + 1.0)
        y_ref[...] = jnp.dot(act.astype(bf16), w2_bf[...], preferred_element_type=jnp.float32) + b2_ref[0]


def _expert_call(tbl, rows, w1, b1, w2, b2):
    n_rows = rows.shape[0]
    n_blocks = n_rows // ROW_BLOCK
    last = lambda i, t: jnp.minimum(i, t[n_blocks] - 1)
    grid_spec = pltpu.PrefetchScalarGridSpec(
        num_scalar_prefetch=1,
        grid=(n_blocks,),
        in_specs=[
            pl.BlockSpec((ROW_BLOCK, D_MODEL), lambda i, t: (last(i, t), 0)),
            pl.BlockSpec((1, D_MODEL, 2 * D_FF), lambda i, t: (t[i], 0, 0)),
            pl.BlockSpec((1, 1, 2 * D_FF), lambda i, t: (t[i], 0, 0)),
            pl.BlockSpec((1, D_FF, D_MODEL), lambda i, t: (t[i], 0, 0)),
            pl.BlockSpec((1, 1, D_MODEL), lambda i, t: (t[i], 0, 0)),
        ],
        out_specs=pl.BlockSpec((ROW_BLOCK, D_MODEL), lambda i, t: (i, 0)),
        scratch_shapes=[
            pltpu.VMEM((D_MODEL, 2 * D_FF), jnp.bfloat16),
            pltpu.VMEM((D_FF, D_MODEL), jnp.bfloat16),
            pltpu.SMEM((1,), jnp.int32),
        ])
    return pl.pallas_call(
        functools.partial(_expert_kernel, n_blocks),
        grid_spec=grid_spec,
        out_shape=jax.ShapeDtypeStruct((n_rows, D_MODEL), jnp.float32),
        name="expert_ffn",
        compiler_params=pltpu.CompilerParams(dimension_semantics=("arbitrary",),
                                             vmem_limit_bytes=VMEM_LIMIT_BYTES),
    )(tbl, rows, w1, b1.reshape(N_EXPERTS, 1, 2 * D_FF), w2, b2.reshape(N_EXPERTS, 1, D_MODEL))


def _combine_kernel(cnt_ref, x1_ref, gate_ref, eid_ref, rank_ref, y_ref, gfin_ref, out_ref,
                    ybuf, pstart_ref, sems):
    i = pl.program_id(0)
    tc = x1_ref.shape[0]

    @pl.when(i == 0)
    def _():
        _expert_starts(cnt_ref, pstart_ref)

    n_dma = tc * TOP_K

    def copy_for(q):
        t = q // TOP_K
        k = q % TOP_K
        src = pstart_ref[eid_ref[k, t]] + rank_ref[k, t]
        return _row_copy(y_ref.at[pl.ds(src, 1)], ybuf.at[k, pl.ds(t, 1)], sems.at[q % SEM_RING])

    def issue(q, _):
        @pl.when(q >= SEM_RING)
        def _():
            copy_for(q - SEM_RING).wait()
        copy_for(q).start()
        return 0
    lax.fori_loop(0, n_dma, issue, 0)

    def drain(q, _):
        copy_for(q).wait()
        return 0
    lax.fori_loop(n_dma - SEM_RING, n_dma, drain, 0)

    g = gate_ref[...]
    acc = x1_ref[...]
    for k in range(TOP_K):
        acc = acc + g[:, k:k + 1] * ybuf[k]
    out_ref[...] = _rms(acc, gfin_ref[...])


def _combine_call(cnt, x1, gate_tm, eid, rank, y_rows, g_final):
    n_tok = x1.shape[0]
    tc = COMBINE_TILE
    grid_spec = pltpu.PrefetchScalarGridSpec(
        num_scalar_prefetch=1,
        grid=(n_tok // tc,),
        in_specs=[
            pl.BlockSpec((tc, D_MODEL), lambda i, c: (i, 0)),
            pl.BlockSpec((tc, 2 * TOP_K), lambda i, c: (i, 0)),
            pl.BlockSpec((TOP_K, tc), lambda i, c: (0, i), memory_space=pltpu.SMEM),
            pl.BlockSpec((TOP_K, tc), lambda i, c: (0, i), memory_space=pltpu.SMEM),
            pl.BlockSpec(memory_space=pl.ANY),
            pl.BlockSpec((1, D_MODEL), lambda i, c: (0, 0)),
        ],
        out_specs=pl.BlockSpec((tc, D_MODEL), lambda i, c: (i, 0)),
        scratch_shapes=[
            pltpu.VMEM((TOP_K, tc, D_MODEL), jnp.float32),
            pltpu.SMEM((N_EXPERTS,), jnp.int32),
            pltpu.SemaphoreType.DMA((SEM_RING,)),
        ])
    return pl.pallas_call(
        _combine_kernel,
        grid_spec=grid_spec,
        out_shape=jax.ShapeDtypeStruct((n_tok, D_MODEL), jnp.float32),
        name="combine_norm",
        compiler_params=pltpu.CompilerParams(dimension_semantics=("arbitrary",),
                                             vmem_limit_bytes=VMEM_LIMIT_BYTES),
    )(cnt, x1, gate_tm, eid, rank, y_rows, g_final)


def _attention_bias():
    slopes = jnp.exp2(-8.0 * jnp.arange(1, N_Q_HEADS + 1, dtype=jnp.float32) / N_Q_HEADS)
    qi = jnp.arange(BLOCK)[:, None]
    kj = jnp.arange(BAND)[None, :]
    dist = qi + BLOCK - kj
    valid = (dist >= 0) & (dist < WINDOW)
    b = -slopes[:, None, None] * dist.astype(jnp.float32)[None]
    b = jnp.where(valid[None], b, -jnp.inf)
    b = b.reshape(N_KV_HEADS, GROUP, BLOCK, BAND).transpose(0, 2, 1, 3)
    return b.reshape(N_KV_HEADS, BLOCK, GROUP * BAND)


def kernel(x, g_mix, w_in, conv_w, sinks, g_attn_out, g_conv_out, w_out, g_ffn, w_router, b_router,
           w1, b1, w2, b2, g_final):
    bsz, s_len, d = x.shape
    n_tok = bsz * s_len
    n_rows = n_tok * TOP_K + N_EXPERTS * ROW_BLOCK
    x2d = x.reshape(n_tok, d)
    bias = _attention_bias()
    l = 0
    x1, h2, eid, gate, rank, cnt = _mixer_call(
        x2d, sinks[l], g_mix[l].reshape(1, d), w_in[l].astype(jnp.bfloat16), conv_w[l], bias,
        g_attn_out[l].reshape(1, ATTN_WIDTH), g_conv_out[l].reshape(1, CONV_WIDTH),
        w_out[l].astype(jnp.bfloat16), g_ffn[l].reshape(1, d), w_router[l].T,
        b_router[l].reshape(N_EXPERTS, 1), s_len)
    cnt1 = cnt[:, 0]
    rows, tbl = _scatter_call(cnt1, h2, eid, rank, n_rows)
    y_rows = _expert_call(tbl, rows, w1[l], b1[l], w2[l], b2[l])
    gate_tm = jnp.concatenate([gate, jnp.zeros_like(gate)], axis=0).T
    out = _combine_call(cnt1, x1, gate_tm, eid, rank, y_rows, g_final.reshape(1, d))
    return out.reshape(bsz, s_len, d)
```

```python
import functools
import math

import jax
import jax.numpy as jnp
from jax import lax
from jax.experimental import pallas as pl
from jax.experimental.pallas import tpu as pltpu

D_MODEL = 1024
N_Q_HEADS = 8
N_KV_HEADS = 2
HEAD_DIM = 64
GROUP = N_Q_HEADS // N_KV_HEADS
ATTN_WIDTH = N_Q_HEADS * HEAD_DIM
KV_WIDTH = N_KV_HEADS * HEAD_DIM
CONV_WIDTH = D_MODEL - ATTN_WIDTH
QKV_COLS = ATTN_WIDTH + 2 * KV_WIDTH
IN_COLS = QKV_COLS + 3 * CONV_WIDTH
WINDOW = 128
BLOCK = 128
BAND = 2 * BLOCK
N_EXPERTS = 32
TOP_K = 4
D_FF = D_MODEL
SWIGLU_ALPHA = 1.702
SWIGLU_LIMIT = 7.0
EPS = 1e-5

LANES = 128
SUBLANES = 8
VMEM_LIMIT_BYTES = 56 * 1024 * 1024

TOKEN_TILE = 512
ROW_BLOCK = 256
SCATTER_TILE = 1024
COMBINE_TILE = 256
ISSUE_UNROLL = 8


def _rms(x, g):
    return x * lax.rsqrt(jnp.mean(x * x, axis=-1, keepdims=True) + EPS) * g


def _block_diag4(lo_tile, hi_tile):
    z = jnp.zeros_like(lo_tile)
    return jnp.concatenate([
        jnp.concatenate([lo_tile, z], axis=1),
        jnp.concatenate([hi_tile, z], axis=1),
        jnp.concatenate([z, lo_tile], axis=1),
        jnp.concatenate([z, hi_tile], axis=1)], axis=0)


def _mixer_kernel(tiles_per_seq,
                  sink_ref,
                  x_ref, gmix_ref, win_ref, convw_ref, bias_ref, gattn_ref, gconv_ref, wout_ref,
                  gffn_ref, wrt_ref, br_ref,
                  x1_ref, h2_ref, eid_ref, gate_ref, rank_ref, cnt_ref,
                  p_scr, kv_scr, u_scr, attn_scr, tri_scr, cnt_scr):
    tm = x_ref.shape[0]
    i = pl.program_id(0)
    first = (i % tiles_per_seq) == 0
    f32, bf16 = jnp.float32, jnp.bfloat16

    @pl.when(i == 0)
    def _():
        cnt_scr[...] = jnp.zeros_like(cnt_scr)
        r = lax.broadcasted_iota(jnp.int32, (tm, tm), 0)
        c = lax.broadcasted_iota(jnp.int32, (tm, tm), 1)
        tri_scr[...] = jnp.where(r < c, 1.0, 0.0).astype(bf16)

    @pl.when(first)
    def _():
        kv_scr[0:BLOCK, :] = jnp.zeros((BLOCK, kv_scr.shape[1]), bf16)
        u_scr[0:SUBLANES, :] = jnp.zeros((SUBLANES, CONV_WIDTH), f32)

    x = x_ref[...]
    h = _rms(x, gmix_ref[...])
    p_scr[...] = jnp.dot(h.astype(bf16), win_ref[...], preferred_element_type=f32)

    kk = p_scr[:, ATTN_WIDTH:ATTN_WIDTH + KV_WIDTH]
    vv = p_scr[:, ATTN_WIDTH + KV_WIDTH:QKV_COLS]
    kr = pltpu.roll(kk, HEAD_DIM, axis=1)
    vr = pltpu.roll(vv, HEAD_DIM, axis=1)
    kv_scr[BLOCK:BLOCK + tm, :] = jnp.concatenate([kk, kr, vv, vr], axis=1).astype(bf16)

    lane = lax.broadcasted_iota(jnp.int32, (BAND, LANES), 1)
    lo = lane < HEAD_DIM
    col = lax.broadcasted_iota(jnp.int32, (BLOCK, GROUP * BAND), 1)
    prev_half = (col % BAND) < BLOCK
    scale = 1.0 / math.sqrt(HEAD_DIM)
    zero_t = jnp.zeros((BAND, LANES), bf16)

    for j in range(tm // BLOCK):
        band = kv_scr[j * BLOCK:j * BLOCK + BAND, :]
        bk, bkr = band[:, 0:LANES], band[:, LANES:2 * LANES]
        bv, bvr = band[:, 2 * LANES:3 * LANES], band[:, 3 * LANES:4 * LANES]
        for hh in range(N_KV_HEADS):
            if hh == 0:
                k_lo, k_hi = jnp.where(lo, bk, zero_t), jnp.where(lo, zero_t, bkr)
                v_lo, v_hi = jnp.where(lo, bv, zero_t), jnp.where(lo, zero_t, bvr)
            else:
                k_lo, k_hi = jnp.where(lo, bkr, zero_t), jnp.where(lo, zero_t, bk)
                v_lo, v_hi = jnp.where(lo, bvr, zero_t), jnp.where(lo, zero_t, bv)
            k4 = _block_diag4(k_lo, k_hi)
            v4 = _block_diag4(v_lo, v_hi)
            qh = (p_scr[j * BLOCK:(j + 1) * BLOCK, hh * 2 * LANES:(hh + 1) * 2 * LANES] * scale).astype(bf16)
            s = lax.dot_general(qh, k4, (((1,), (1,)), ((), ())), preferred_element_type=f32)
            s = s + bias_ref[hh]
            if j == 0:
                s = jnp.where(jnp.logical_and(first, prev_half), -jnp.inf, s)
            probs = []
            for g in range(GROUP):
                sg = s[:, g * BAND:(g + 1) * BAND]
                sink = sink_ref[hh * GROUP + g]
                m = jnp.maximum(jnp.max(sg, axis=-1, keepdims=True), sink)
                e = jnp.exp(sg - m)
                den = jnp.sum(e, axis=-1, keepdims=True) + jnp.exp(sink - m)
                probs.append((e * (1.0 / den)).astype(bf16))
            pmat = jnp.concatenate(probs, axis=1)
            o = jnp.dot(pmat, v4, preferred_element_type=f32)
            attn_scr[j * BLOCK:(j + 1) * BLOCK, hh * 2 * LANES:(hh + 1) * 2 * LANES] = o

    bg = p_scr[:, QKV_COLS:QKV_COLS + CONV_WIDTH]
    cg = p_scr[:, QKV_COLS + CONV_WIDTH:QKV_COLS + 2 * CONV_WIDTH]
    xin = p_scr[:, QKV_COLS + 2 * CONV_WIDTH:IN_COLS]
    u = cg * xin
    u_scr[SUBLANES:SUBLANES + tm, :] = u
    u1 = u_scr[SUBLANES - 1:SUBLANES - 1 + tm, :]
    u2 = u_scr[SUBLANES - 2:SUBLANES - 2 + tm, :]
    y = convw_ref[0:1, :] * u2 + convw_ref[1:2, :] * u1 + convw_ref[2:3, :] * u
    conv = bg * y
    kv_scr[0:BLOCK, :] = kv_scr[tm:tm + BLOCK, :]
    u_scr[0:SUBLANES, :] = u_scr[tm:tm + SUBLANES, :]

    mix = jnp.concatenate([_rms(attn_scr[...], gattn_ref[...]), _rms(conv, gconv_ref[...])], axis=1)
    x1 = x + jnp.dot(mix.astype(bf16), wout_ref[...], preferred_element_type=f32)
    x1_ref[...] = x1
    h2 = _rms(x1, gffn_ref[...])
    h2_ref[...] = h2

    logits = lax.dot_general(wrt_ref[...], h2, (((1,), (1,)), ((), ())),
                             precision=lax.Precision.HIGHEST, preferred_element_type=f32) + br_ref[...]
    row = lax.broadcasted_iota(jnp.int32, (N_EXPERTS, tm), 0)
    vals, idxs = [], []
    l = logits
    for _ in range(TOP_K):
        m = jnp.max(l, axis=0, keepdims=True)
        idx = jnp.min(jnp.where(l == m, row, N_EXPERTS), axis=0, keepdims=True)
        vals.append(m)
        idxs.append(idx)
        l = jnp.where(row == idx, -jnp.inf, l)
    es = [jnp.exp(v - vals[0]) for v in vals]
    inv = 1.0 / (es[0] + es[1] + es[2] + es[3])
    gate_ref[...] = jnp.concatenate([e * inv for e in es], axis=0)
    eid_ref[...] = jnp.concatenate(idxs, axis=0)

    onehots = [row == idx for idx in idxs]
    sel = jnp.logical_or(jnp.logical_or(onehots[0], onehots[1]), jnp.logical_or(onehots[2], onehots[3]))
    oh = jnp.where(sel, 1.0, 0.0)
    base = cnt_scr[...][:, 0:1]
    cum = jnp.dot(oh.astype(bf16), tri_scr[...], preferred_element_type=f32) + base
    ranks = [jnp.sum(jnp.where(o, cum, 0.0), axis=0, keepdims=True) for o in onehots]
    rank_ref[...] = jnp.concatenate(ranks, axis=0).astype(jnp.int32)
    new_cnt = base + jnp.sum(oh, axis=1, keepdims=True)
    cnt_scr[...] = jnp.broadcast_to(new_cnt, cnt_scr.shape)
    cnt_ref[...] = jnp.broadcast_to(new_cnt, cnt_ref.shape).astype(jnp.int32)


def _mixer_call(x2d, sinks, g_mix, w_in, conv_w, bias, g_attn, g_conv, w_out, g_ffn, wrt, br, seq_len):
    n_tok = x2d.shape[0]
    tm = TOKEN_TILE
    n_tiles = n_tok // tm
    const = lambda shape: pl.BlockSpec(shape, lambda i, *_: (0,) * len(shape))
    grid_spec = pltpu.PrefetchScalarGridSpec(
        num_scalar_prefetch=0,
        grid=(n_tiles,),
        in_specs=[
            pl.BlockSpec(memory_space=pltpu.SMEM),
            pl.BlockSpec((tm, D_MODEL), lambda i: (i, 0)),
            const((1, D_MODEL)), const((D_MODEL, IN_COLS)), const((3, CONV_WIDTH)),
            const((N_KV_HEADS, BLOCK, GROUP * BAND)),
            const((1, ATTN_WIDTH)), const((1, CONV_WIDTH)), const((D_MODEL, D_MODEL)),
            const((1, D_MODEL)), const((N_EXPERTS, D_MODEL)), const((N_EXPERTS, 1)),
        ],
        out_specs=[
            pl.BlockSpec((tm, D_MODEL), lambda i: (i, 0)),
            pl.BlockSpec((tm, D_MODEL), lambda i: (i, 0)),
            pl.BlockSpec((TOP_K, tm), lambda i: (0, i)),
            pl.BlockSpec((TOP_K, tm), lambda i: (0, i)),
            pl.BlockSpec((TOP_K, tm), lambda i: (0, i)),
            pl.BlockSpec((N_EXPERTS, LANES), lambda i: (0, 0)),
        ],
        scratch_shapes=[
            pltpu.VMEM((tm, IN_COLS), jnp.float32),
            pltpu.VMEM((BLOCK + tm, 4 * LANES), jnp.bfloat16),
            pltpu.VMEM((SUBLANES + tm, CONV_WIDTH), jnp.float32),
            pltpu.VMEM((tm, ATTN_WIDTH), jnp.float32),
            pltpu.VMEM((tm, tm), jnp.bfloat16),
            pltpu.VMEM((N_EXPERTS, LANES), jnp.float32),
        ])
    out_shape = [
        jax.ShapeDtypeStruct((n_tok, D_MODEL), jnp.float32),
        jax.ShapeDtypeStruct((n_tok, D_MODEL), jnp.float32),
        jax.ShapeDtypeStruct((TOP_K, n_tok), jnp.int32),
        jax.ShapeDtypeStruct((TOP_K, n_tok), jnp.float32),
        jax.ShapeDtypeStruct((TOP_K, n_tok), jnp.int32),
        jax.ShapeDtypeStruct((N_EXPERTS, LANES), jnp.int32),
    ]
    return pl.pallas_call(
        functools.partial(_mixer_kernel, seq_len // tm),
        grid_spec=grid_spec, out_shape=out_shape, name="mixer_router",
        compiler_params=pltpu.CompilerParams(dimension_semantics=("arbitrary",),
                                             vmem_limit_bytes=VMEM_LIMIT_BYTES),
    )(sinks, x2d, g_mix, w_in, conv_w, bias, g_attn, g_conv, w_out, g_ffn, wrt, br)


def _expert_starts(cnt_ref, pstart_ref):
    def body(e, pos):
        pstart_ref[e] = pos
        return pos + (cnt_ref[e] + ROW_BLOCK - 1) // ROW_BLOCK * ROW_BLOCK
    return lax.fori_loop(0, N_EXPERTS, body, jnp.int32(0))


def _scatter_kernel(n_blocks,
                    cnt_ref,
                    h2_ref, eid_ref, rank_ref, zeros_ref,
                    rows_ref, tbl_ref,
                    pstart_ref, sems, zsem):
    i = pl.program_id(0)
    tg = h2_ref.shape[0]

    @pl.when(i == 0)
    def _():
        total = _expert_starts(cnt_ref, pstart_ref)
        def fill(e, carry):
            pos, last = carry
            nb = (cnt_ref[e] + ROW_BLOCK - 1) // ROW_BLOCK
            def put(b, _):
                tbl_ref[pos + b] = e
                return 0
            lax.fori_loop(0, nb, put, 0)
            return pos + nb, jnp.where(nb > 0, e, last)
        used, last = lax.fori_loop(0, N_EXPERTS, fill, (jnp.int32(0), jnp.int32(0)))
        def tail(b, _):
            tbl_ref[b] = last
            return 0
        lax.fori_loop(used, n_blocks, tail, 0)
        tbl_ref[n_blocks] = used
        del total
        def zfill(e, _):
            c = cnt_ref[e]
            @pl.when(c > 0)
            def _():
                end = pstart_ref[e] + (c + ROW_BLOCK - 1) // ROW_BLOCK * ROW_BLOCK
                start = pl.multiple_of(end - ROW_BLOCK, ROW_BLOCK)
                cp = pltpu.make_async_copy(zeros_ref, rows_ref.at[pl.ds(start, ROW_BLOCK)], zsem)
                cp.start()
                cp.wait()
            return 0
        lax.fori_loop(0, N_EXPERTS, zfill, 0)
        def ztail(b, _):
            cp = pltpu.make_async_copy(
                zeros_ref, rows_ref.at[pl.ds(pl.multiple_of(b * ROW_BLOCK, ROW_BLOCK), ROW_BLOCK)], zsem)
            cp.start()
            cp.wait()
            return 0
        lax.fori_loop(used, n_blocks, ztail, 0)

    n_dma = tg * TOP_K

    def issue(grp, _):
        for dt in range(ISSUE_UNROLL):
            t = grp * ISSUE_UNROLL + dt
            for k in range(TOP_K):
                dst = pstart_ref[eid_ref[k, t]] + rank_ref[k, t]
                pltpu.make_async_copy(h2_ref.at[pl.ds(t, 1)], rows_ref.at[pl.ds(dst, 1)], sems).start()
        return 0
    lax.fori_loop(0, tg // ISSUE_UNROLL, issue, 0)
    pltpu.make_async_copy(rows_ref.at[pl.ds(0, n_dma)], rows_ref.at[pl.ds(n_dma, n_dma)], sems).wait()


def _scatter_call(cnt, h2, eid, rank, n_rows):
    n_tok = h2.shape[0]
    tg = SCATTER_TILE
    n_blocks = n_rows // ROW_BLOCK
    zeros = jnp.zeros((ROW_BLOCK, D_MODEL), jnp.float32)
    grid_spec = pltpu.PrefetchScalarGridSpec(
        num_scalar_prefetch=1,
        grid=(n_tok // tg,),
        in_specs=[
            pl.BlockSpec((tg, D_MODEL), lambda i, c: (i, 0)),
            pl.BlockSpec((TOP_K, tg), lambda i, c: (0, i), memory_space=pltpu.SMEM),
            pl.BlockSpec((TOP_K, tg), lambda i, c: (0, i), memory_space=pltpu.SMEM),
            pl.BlockSpec(memory_space=pl.ANY),
        ],
        out_specs=[
            pl.BlockSpec(memory_space=pl.ANY),
            pl.BlockSpec(memory_space=pltpu.SMEM),
        ],
        scratch_shapes=[
            pltpu.SMEM((N_EXPERTS,), jnp.int32),
            pltpu.SemaphoreType.DMA(()),
            pltpu.SemaphoreType.DMA(()),
        ])
    return pl.pallas_call(
        functools.partial(_scatter_kernel, n_blocks),
        grid_spec=grid_spec,
        out_shape=[jax.ShapeDtypeStruct((n_rows, D_MODEL), jnp.float32),
                   jax.ShapeDtypeStruct((n_blocks + 1,), jnp.int32)],
        name="row_scatter",
        compiler_params=pltpu.CompilerParams(dimension_semantics=("arbitrary",),
                                             vmem_limit_bytes=VMEM_LIMIT_BYTES),
    )(cnt, h2, eid, rank, zeros)


def _expert_kernel(n_blocks, tbl_ref, x_ref, w1_ref, b1_ref, w2_ref, b2_ref, y_ref, w1_bf, w2_bf, prev_e):
    i = pl.program_id(0)
    bf16 = jnp.bfloat16
    e = tbl_ref[i]

    @pl.when(i == 0)
    def _():
        prev_e[0] = jnp.int32(-1)

    @pl.when(e != prev_e[0])
    def _():
        w1_bf[...] = w1_ref[0].astype(bf16)
        w2_bf[...] = w2_ref[0].astype(bf16)
        prev_e[0] = e

    @pl.when(i >= tbl_ref[n_blocks])
    def _():
        y_ref[...] = jnp.zeros_like(y_ref)

    @pl.when(i < tbl_ref[n_blocks])
    def _():
        hu = jnp.dot(x_ref[...].astype(bf16), w1_bf[...], preferred_element_type=jnp.float32) + b1_ref[0]
        glu = jnp.minimum(hu[:, :D_FF], SWIGLU_LIMIT)
        lin = jnp.clip(hu[:, D_FF:], -SWIGLU_LIMIT, SWIGLU_LIMIT)
        act = glu * (1.0 / (1.0 + jnp.exp(-SWIGLU_ALPHA * glu))) * (lin + 1.0)
        y_ref[...] = jnp.dot(act.astype(bf16), w2_bf[...], preferred_element_type=jnp.float32) + b2_ref[0]


def _expert_call(tbl, rows, w1, b1, w2, b2):
    n_rows = rows.shape[0]
    n_blocks = n_rows // ROW_BLOCK
    last = lambda i, t: jnp.minimum(i, t[n_blocks] - 1)
    grid_spec = pltpu.PrefetchScalarGridSpec(
        num_scalar_prefetch=1,
        grid=(n_blocks,),
        in_specs=[
            pl.BlockSpec((ROW_BLOCK, D_MODEL), lambda i, t: (last(i, t), 0)),
            pl.BlockSpec((1, D_MODEL, 2 * D_FF), lambda i, t: (t[i], 0, 0)),
            pl.BlockSpec((1, 1, 2 * D_FF), lambda i, t: (t[i], 0, 0)),
            pl.BlockSpec((1, D_FF, D_MODEL), lambda i, t: (t[i], 0, 0)),
            pl.BlockSpec((1, 1, D_MODEL), lambda i, t: (t[i], 0, 0)),
        ],
        out_specs=pl.BlockSpec((ROW_BLOCK, D_MODEL), lambda i, t: (i, 0)),
        scratch_shapes=[
            pltpu.VMEM((D_MODEL, 2 * D_FF), jnp.bfloat16),
            pltpu.VMEM((D_FF, D_MODEL), jnp.bfloat16),
            pltpu.SMEM((1,), jnp.int32),
        ])
    return pl.pallas_call(
        functools.partial(_expert_kernel, n_blocks),
        grid_spec=grid_spec,
        out_shape=jax.ShapeDtypeStruct((n_rows, D_MODEL), jnp.float32),
        name="expert_ffn",
        compiler_params=pltpu.CompilerParams(dimension_semantics=("arbitrary",),
                                             vmem_limit_bytes=VMEM_LIMIT_BYTES),
    )(tbl, rows, w1, b1.reshape(N_EXPERTS, 1, 2 * D_FF), w2, b2.reshape(N_EXPERTS, 1, D_MODEL))


def _combine_kernel(cnt_ref, x1_ref, gate_ref, eid_ref, rank_ref, eidn_ref, rankn_ref, y_ref, gfin_ref,
                    out_ref, ybuf, pstart_ref, sems):
    i = pl.program_id(0)
    n_steps = pl.num_programs(0)
    tc = x1_ref.shape[0]

    def issue_tile(e_ref, r_ref, slot):
        def issue(grp, _):
            for dt in range(ISSUE_UNROLL):
                t = grp * ISSUE_UNROLL + dt
                for k in range(TOP_K):
                    src = pstart_ref[e_ref[k, t]] + r_ref[k, t]
                    pltpu.make_async_copy(y_ref.at[pl.ds(src, 1)], ybuf.at[slot, k, pl.ds(t, 1)],
                                          sems.at[slot]).start()
            return 0
        lax.fori_loop(0, tc // ISSUE_UNROLL, issue, 0)

    @pl.when(i == 0)
    def _():
        _expert_starts(cnt_ref, pstart_ref)
        issue_tile(eid_ref, rank_ref, 0)

    slot = i % 2

    @pl.when(i + 1 < n_steps)
    def _():
        issue_tile(eidn_ref, rankn_ref, 1 - slot)

    for k in range(TOP_K):
        pltpu.make_async_copy(y_ref.at[pl.ds(0, tc)], ybuf.at[slot, k], sems.at[slot]).wait()

    g = gate_ref[...]
    acc = x1_ref[...]
    for k in range(TOP_K):
        acc = acc + g[:, k:k + 1] * ybuf[slot, k]
    out_ref[...] = _rms(acc, gfin_ref[...])


def _combine_call(cnt, x1, gate_tm, eid, rank, y_rows, g_final):
    n_tok = x1.shape[0]
    tc = COMBINE_TILE
    n_steps = n_tok // tc
    nxt = lambda i, c: (0, jnp.minimum(i + 1, n_steps - 1))
    grid_spec = pltpu.PrefetchScalarGridSpec(
        num_scalar_prefetch=1,
        grid=(n_steps,),
        in_specs=[
            pl.BlockSpec((tc, D_MODEL), lambda i, c: (i, 0)),
            pl.BlockSpec((tc, 2 * TOP_K), lambda i, c: (i, 0)),
            pl.BlockSpec((TOP_K, tc), lambda i, c: (0, i), memory_space=pltpu.SMEM),
            pl.BlockSpec((TOP_K, tc), lambda i, c: (0, i), memory_space=pltpu.SMEM),
            pl.BlockSpec((TOP_K, tc), nxt, memory_space=pltpu.SMEM),
            pl.BlockSpec((TOP_K, tc), nxt, memory_space=pltpu.SMEM),
            pl.BlockSpec(memory_space=pl.ANY),
            pl.BlockSpec((1, D_MODEL), lambda i, c: (0, 0)),
        ],
        out_specs=pl.BlockSpec((tc, D_MODEL), lambda i, c: (i, 0)),
        scratch_shapes=[
            pltpu.VMEM((2, TOP_K, tc, D_MODEL), jnp.float32),
            pltpu.SMEM((N_EXPERTS,), jnp.int32),
            pltpu.SemaphoreType.DMA((2,)),
        ])
    return pl.pallas_call(
        _combine_kernel,
        grid_spec=grid_spec,
        out_shape=jax.ShapeDtypeStruct((n_tok, D_MODEL), jnp.float32),
        name="combine_norm",
        compiler_params=pltpu.CompilerParams(dimension_semantics=("arbitrary",),
                                             vmem_limit_bytes=VMEM_LIMIT_BYTES),
    )(cnt, x1, gate_tm, eid, rank, eid, rank, y_rows, g_final)


def _attention_bias():
    slopes = jnp.exp2(-8.0 * jnp.arange(1, N_Q_HEADS + 1, dtype=jnp.float32) / N_Q_HEADS)
    qi = jnp.arange(BLOCK)[:, None]
    kj = jnp.arange(BAND)[None, :]
    dist = qi + BLOCK - kj
    valid = (dist >= 0) & (dist < WINDOW)
    b = -slopes[:, None, None] * dist.astype(jnp.float32)[None]
    b = jnp.where(valid[None], b, -jnp.inf)
    b = b.reshape(N_KV_HEADS, GROUP, BLOCK, BAND).transpose(0, 2, 1, 3)
    return b.reshape(N_KV_HEADS, BLOCK, GROUP * BAND)


def kernel(x, g_mix, w_in, conv_w, sinks, g_attn_out, g_conv_out, w_out, g_ffn, w_router, b_router,
           w1, b1, w2, b2, g_final):
    bsz, s_len, d = x.shape
    n_tok = bsz * s_len
    n_rows = n_tok * TOP_K + N_EXPERTS * ROW_BLOCK
    x2d = x.reshape(n_tok, d)
    bias = _attention_bias()
    l = 0
    x1, h2, eid, gate, rank, cnt = _mixer_call(
        x2d, sinks[l], g_mix[l].reshape(1, d), w_in[l].astype(jnp.bfloat16), conv_w[l], bias,
        g_attn_out[l].reshape(1, ATTN_WIDTH), g_conv_out[l].reshape(1, CONV_WIDTH),
        w_out[l].astype(jnp.bfloat16), g_ffn[l].reshape(1, d), w_router[l].T,
        b_router[l].reshape(N_EXPERTS, 1), s_len)
    cnt1 = cnt[:, 0]
    rows, tbl = _scatter_call(cnt1, h2, eid, rank, n_rows)
    y_rows = _expert_call(tbl, rows, w1[l], b1[l], w2[l], b2[l])
    gate_tm = jnp.concatenate([gate, jnp.zeros_like(gate)], axis=0).T
    out = _combine_call(cnt1, x1, gate_tm, eid, rank, y_rows, g_final.reshape(1, d))
    return out.reshape(bsz, s_len, d)
```

```python
import functools
import math

import jax
import jax.numpy as jnp
from jax import lax
from jax.experimental import pallas as pl
from jax.experimental.pallas import tpu as pltpu

D_MODEL = 1024
N_Q_HEADS = 8
N_KV_HEADS = 2
HEAD_DIM = 64
GROUP = N_Q_HEADS // N_KV_HEADS
ATTN_WIDTH = N_Q_HEADS * HEAD_DIM
KV_WIDTH = N_KV_HEADS * HEAD_DIM
CONV_WIDTH = D_MODEL - ATTN_WIDTH
QKV_COLS = ATTN_WIDTH + 2 * KV_WIDTH
IN_COLS = QKV_COLS + 3 * CONV_WIDTH
WINDOW = 128
BLOCK = 128
BAND = 2 * BLOCK
N_EXPERTS = 32
TOP_K = 4
D_FF = D_MODEL
SWIGLU_ALPHA = 1.702
SWIGLU_LIMIT = 7.0
EPS = 1e-5

LANES = 128
SUBLANES = 8
VMEM_LIMIT_BYTES = 56 * 1024 * 1024

TOKEN_TILE = 512
ROUTE_TILE = 256
ROW_BLOCK = 256
STAGE_ROWS = -(-(TOP_K * ROUTE_TILE + N_EXPERTS * (SUBLANES - 1)) // LANES) * LANES


def _rms(x, g):
    return x * lax.rsqrt(jnp.mean(x * x, axis=-1, keepdims=True) + EPS) * g


def _block_diag4(lo_tile, hi_tile):
    z = jnp.zeros_like(lo_tile)
    return jnp.concatenate([
        jnp.concatenate([lo_tile, z], axis=1),
        jnp.concatenate([hi_tile, z], axis=1),
        jnp.concatenate([z, lo_tile], axis=1),
        jnp.concatenate([z, hi_tile], axis=1)], axis=0)


def _mixer_kernel(tiles_per_seq,
                  sink_ref,
                  x_ref, gmix_ref, win_ref, convw_ref, bias_ref, gattn_ref, gconv_ref, wout_ref,
                  gffn_ref, wrt_ref, br_ref,
                  x1_ref, h2_ref, pos_ref, gate_ref, cnt_ref,
                  p_scr, kv_scr, u_scr, attn_scr, tri_scr):
    tm = x_ref.shape[0]
    ts = ROUTE_TILE
    i = pl.program_id(0)
    first = (i % tiles_per_seq) == 0
    f32, bf16 = jnp.float32, jnp.bfloat16

    @pl.when(i == 0)
    def _():
        cnt_ref[...] = jnp.zeros_like(cnt_ref)
        r = lax.broadcasted_iota(jnp.int32, (ts, ts), 0)
        c = lax.broadcasted_iota(jnp.int32, (ts, ts), 1)
        tri_scr[...] = jnp.where(r < c, 1.0, 0.0).astype(bf16)

    @pl.when(first)
    def _():
        kv_scr[0:BLOCK, :] = jnp.zeros((BLOCK, kv_scr.shape[1]), bf16)
        u_scr[0:SUBLANES, :] = jnp.zeros((SUBLANES, CONV_WIDTH), f32)

    x = x_ref[...]
    h = _rms(x, gmix_ref[...])
    p_scr[...] = jnp.dot(h.astype(bf16), win_ref[...], preferred_element_type=f32)

    kk = p_scr[:, ATTN_WIDTH:ATTN_WIDTH + KV_WIDTH]
    vv = p_scr[:, ATTN_WIDTH + KV_WIDTH:QKV_COLS]
    kr = pltpu.roll(kk, HEAD_DIM, axis=1)
    vr = pltpu.roll(vv, HEAD_DIM, axis=1)
    kv_scr[BLOCK:BLOCK + tm, :] = jnp.concatenate([kk, kr, vv, vr], axis=1).astype(bf16)

    lane = lax.broadcasted_iota(jnp.int32, (BAND, LANES), 1)
    lo = lane < HEAD_DIM
    col = lax.broadcasted_iota(jnp.int32, (BLOCK, GROUP * BAND), 1)
    prev_half = (col % BAND) < BLOCK
    scale = 1.0 / math.sqrt(HEAD_DIM)
    zero_t = jnp.zeros((BAND, LANES), bf16)

    for j in range(tm // BLOCK):
        band = kv_scr[j * BLOCK:j * BLOCK + BAND, :]
        bk, bkr = band[:, 0:LANES], band[:, LANES:2 * LANES]
        bv, bvr = band[:, 2 * LANES:3 * LANES], band[:, 3 * LANES:4 * LANES]
        for hh in range(N_KV_HEADS):
            if hh == 0:
                k_lo, k_hi = jnp.where(lo, bk, zero_t), jnp.where(lo, zero_t, bkr)
                v_lo, v_hi = jnp.where(lo, bv, zero_t), jnp.where(lo, zero_t, bvr)
            else:
                k_lo, k_hi = jnp.where(lo, bkr, zero_t), jnp.where(lo, zero_t, bk)
                v_lo, v_hi = jnp.where(lo, bvr, zero_t), jnp.where(lo, zero_t, bv)
            k4 = _block_diag4(k_lo, k_hi)
            v4 = _block_diag4(v_lo, v_hi)
            qh = (p_scr[j * BLOCK:(j + 1) * BLOCK, hh * 2 * LANES:(hh + 1) * 2 * LANES] * scale).astype(bf16)
            s = lax.dot_general(qh, k4, (((1,), (1,)), ((), ())), preferred_element_type=f32)
            s = s + bias_ref[hh]
            if j == 0:
                s = jnp.where(jnp.logical_and(first, prev_half), -jnp.inf, s)
            probs = []
            for g in range(GROUP):
                sg = s[:, g * BAND:(g + 1) * BAND]
                sink = sink_ref[hh * GROUP + g]
                m = jnp.maximum(jnp.max(sg, axis=-1, keepdims=True), sink)
                e = jnp.exp(sg - m)
                den = jnp.sum(e, axis=-1, keepdims=True) + jnp.exp(sink - m)
                probs.append((e * (1.0 / den)).astype(bf16))
            pmat = jnp.concatenate(probs, axis=1)
            o = jnp.dot(pmat, v4, preferred_element_type=f32)
            attn_scr[j * BLOCK:(j + 1) * BLOCK, hh * 2 * LANES:(hh + 1) * 2 * LANES] = o

    bg = p_scr[:, QKV_COLS:QKV_COLS + CONV_WIDTH]
    cg = p_scr[:, QKV_COLS + CONV_WIDTH:QKV_COLS + 2 * CONV_WIDTH]
    xin = p_scr[:, QKV_COLS + 2 * CONV_WIDTH:IN_COLS]
    u = cg * xin
    u_scr[SUBLANES:SUBLANES + tm, :] = u
    u1 = u_scr[SUBLANES - 1:SUBLANES - 1 + tm, :]
    u2 = u_scr[SUBLANES - 2:SUBLANES - 2 + tm, :]
    y = convw_ref[0:1, :] * u2 + convw_ref[1:2, :] * u1 + convw_ref[2:3, :] * u
    conv = bg * y
    kv_scr[0:BLOCK, :] = kv_scr[tm:tm + BLOCK, :]
    u_scr[0:SUBLANES, :] = u_scr[tm:tm + SUBLANES, :]

    mix = jnp.concatenate([_rms(attn_scr[...], gattn_ref[...]), _rms(conv, gconv_ref[...])], axis=1)
    x1 = x + jnp.dot(mix.astype(bf16), wout_ref[...], preferred_element_type=f32)
    x1_ref[...] = x1
    h2 = _rms(x1, gffn_ref[...])
    h2_ref[...] = h2.astype(bf16)

    logits = lax.dot_general(wrt_ref[...], h2, (((1,), (1,)), ((), ())),
                             precision=lax.Precision.HIGHEST, preferred_element_type=f32) + br_ref[...]
    row = lax.broadcasted_iota(jnp.int32, (N_EXPERTS, tm), 0)
    vals, idxs = [], []
    l = logits
    for _ in range(TOP_K):
        m = jnp.max(l, axis=0, keepdims=True)
        idx = jnp.min(jnp.where(l == m, row, N_EXPERTS), axis=0, keepdims=True)
        vals.append(m)
        idxs.append(idx)
        l = jnp.where(row == idx, -jnp.inf, l)
    es = [jnp.exp(v - vals[0]) for v in vals]
    inv = 1.0 / (es[0] + es[1] + es[2] + es[3])
    gate_ref[...] = jnp.concatenate([e * inv for e in es], axis=0)

    onehots = [row == idx for idx in idxs]
    sel = jnp.logical_or(jnp.logical_or(onehots[0], onehots[1]), jnp.logical_or(onehots[2], onehots[3]))
    oh = jnp.where(sel, 1.0, 0.0)

    er = lax.broadcasted_iota(jnp.int32, (N_EXPERTS, N_EXPERTS), 0)
    ec = lax.broadcasted_iota(jnp.int32, (N_EXPERTS, N_EXPERTS), 1)
    lower = jnp.where(ec < er, 1.0, 0.0).astype(bf16)
    cnt_lane = lax.broadcasted_iota(jnp.int32, cnt_ref.shape, 1)
    poss = []
    for sub in range(tm // ts):
        sl = slice(sub * ts, (sub + 1) * ts)
        oh_s = oh[:, sl]
        cum = jnp.dot(oh_s.astype(bf16), tri_scr[...], preferred_element_type=f32)
        cnt = jnp.sum(oh_s, axis=1, keepdims=True)
        units = jnp.floor((cnt + (SUBLANES - 1.0)) * (1.0 / SUBLANES))
        goff = SUBLANES * jnp.dot(lower, jnp.broadcast_to(units, (N_EXPERTS, LANES)).astype(bf16),
                                  preferred_element_type=f32)[:, 0:1]
        pos = goff + cum
        poss.append(jnp.concatenate(
            [jnp.sum(jnp.where(o[:, sl], pos, 0.0), axis=0, keepdims=True) for o in onehots], axis=0))
        tile_id = i * (tm // ts) + sub
        cnt_ref[...] = jnp.where(cnt_lane == tile_id, cnt.astype(jnp.int32), cnt_ref[...])
    pos_ref[...] = jnp.concatenate(poss, axis=1).astype(jnp.int32)


def _mixer_call(x2d, sinks, g_mix, w_in, conv_w, bias, g_attn, g_conv, w_out, g_ffn, wrt, br, seq_len):
    n_tok = x2d.shape[0]
    tm = TOKEN_TILE
    n_tiles = n_tok // tm
    assert n_tok // ROUTE_TILE == LANES, "one lane of the count table per routing tile"
    const = lambda shape: pl.BlockSpec(shape, lambda i, *_: (0,) * len(shape))
    grid_spec = pltpu.PrefetchScalarGridSpec(
        num_scalar_prefetch=0,
        grid=(n_tiles,),
        in_specs=[
            pl.BlockSpec(memory_space=pltpu.SMEM),
            pl.BlockSpec((tm, D_MODEL), lambda i: (i, 0)),
            const((1, D_MODEL)), const((D_MODEL, IN_COLS)), const((3, CONV_WIDTH)),
            const((N_KV_HEADS, BLOCK, GROUP * BAND)),
            const((1, ATTN_WIDTH)), const((1, CONV_WIDTH)), const((D_MODEL, D_MODEL)),
            const((1, D_MODEL)), const((N_EXPERTS, D_MODEL)), const((N_EXPERTS, 1)),
        ],
        out_specs=[
            pl.BlockSpec((tm, D_MODEL), lambda i: (i, 0)),
            pl.BlockSpec((tm, D_MODEL), lambda i: (i, 0)),
            pl.BlockSpec((TOP_K, tm), lambda i: (0, i)),
            pl.BlockSpec((TOP_K, tm), lambda i: (0, i)),
            pl.BlockSpec((N_EXPERTS, LANES), lambda i: (0, 0)),
        ],
        scratch_shapes=[
            pltpu.VMEM((tm, IN_COLS), jnp.float32),
            pltpu.VMEM((BLOCK + tm, 4 * LANES), jnp.bfloat16),
            pltpu.VMEM((SUBLANES + tm, CONV_WIDTH), jnp.float32),
            pltpu.VMEM((tm, ATTN_WIDTH), jnp.float32),
            pltpu.VMEM((ROUTE_TILE, ROUTE_TILE), jnp.bfloat16),
        ])
    out_shape = [
        jax.ShapeDtypeStruct((n_tok, D_MODEL), jnp.float32),
        jax.ShapeDtypeStruct((n_tok, D_MODEL), jnp.bfloat16),
        jax.ShapeDtypeStruct((TOP_K, n_tok), jnp.int32),
        jax.ShapeDtypeStruct((TOP_K, n_tok), jnp.float32),
        jax.ShapeDtypeStruct((N_EXPERTS, LANES), jnp.int32),
    ]
    return pl.pallas_call(
        functools.partial(_mixer_kernel, seq_len // tm),
        grid_spec=grid_spec, out_shape=out_shape, name="mixer_router",
        compiler_params=pltpu.CompilerParams(dimension_semantics=("arbitrary",),
                                             vmem_limit_bytes=VMEM_LIMIT_BYTES),
    )(sinks, x2d, g_mix, w_in, conv_w, bias, g_attn, g_conv, w_out, g_ffn, wrt, br)


def _excl_cumsum(a, axis):
    idx = lax.broadcasted_iota(jnp.int32, a.shape, axis)
    inc = a
    step = 1
    while step < a.shape[axis]:
        inc = inc + jnp.where(idx >= step, pltpu.roll(inc, step, axis), 0)
        step *= 2
    return inc - a


def _tables_kernel(cnt_ref, size_ref, goff_ref, gdst_ref, blk_ref, used_ref):
    size = ((cnt_ref[...] + (SUBLANES - 1)) // SUBLANES) * SUBLANES
    size_ref[...] = size
    goff_ref[...] = _excl_cumsum(size, 0)
    total = jnp.broadcast_to(jnp.sum(size, axis=1, keepdims=True), size.shape)
    region = ((total + (ROW_BLOCK - 1)) // ROW_BLOCK) * ROW_BLOCK
    pstart = _excl_cumsum(region, 0)
    gdst_ref[...] = pstart + _excl_cumsum(size, 1)
    pend = (pstart + region)[:, 0:1]
    start = lax.broadcasted_iota(jnp.int32, (N_EXPERTS, blk_ref.shape[1]), 1) * ROW_BLOCK
    owner = jnp.sum(jnp.where(pend <= start, 1, 0), axis=0, keepdims=True)
    blk_ref[...] = jnp.minimum(owner, N_EXPERTS - 1)
    used_ref[...] = jnp.broadcast_to(jnp.max(pend, axis=0, keepdims=True) // ROW_BLOCK, used_ref.shape)


def _tables_call(cnt, n_blocks):
    nb_pad = -(-n_blocks // LANES) * LANES
    tbl = jax.ShapeDtypeStruct((N_EXPERTS, LANES), jnp.int32)
    return pl.pallas_call(
        _tables_kernel,
        out_shape=[tbl, tbl, tbl, jax.ShapeDtypeStruct((1, nb_pad), jnp.int32),
                   jax.ShapeDtypeStruct((1, LANES), jnp.int32)],
        name="group_tables",
    )(cnt)


def _start_group_copies(tile, size_ref, goff_ref, gdst_ref, sem, make_copy):
    def body(e, rows):
        n = pl.multiple_of(size_ref[e * LANES + tile], SUBLANES)
        @pl.when(n > 0)
        def _():
            off = pl.multiple_of(goff_ref[e * LANES + tile], SUBLANES)
            dst = pl.multiple_of(gdst_ref[e * LANES + tile], SUBLANES)
            make_copy(off, dst, n, sem).start()
        return rows + n
    return lax.fori_loop(0, N_EXPERTS, body, jnp.int32(0))


def _wait_rows(rows_hbm, n, sem):
    @pl.when(n > 0)
    def _():
        m = pl.multiple_of(n, SUBLANES)
        pltpu.make_async_copy(rows_hbm.at[pl.ds(0, m)], rows_hbm.at[pl.ds(STAGE_ROWS, m)], sem).wait()


def _dispatch_kernel(n_blocks,
                     size_ref, goff_ref, gdst_ref, used_ref,
                     h2_ref, pos_ref, zeros_ref, rows_ref,
                     stage, inflight, sems, zsem):
    t = pl.program_id(0)
    n_tiles = pl.num_programs(0)
    slot = t % 2

    @pl.when(t == 0)
    def _():
        def zero_block(start):
            cp = pltpu.make_async_copy(zeros_ref, rows_ref.at[pl.ds(pl.multiple_of(start, ROW_BLOCK), ROW_BLOCK)], zsem)
            cp.start()
            cp.wait()
        def zfill(e, _):
            first = gdst_ref[e * LANES]
            last = gdst_ref[e * LANES + LANES - 1] + size_ref[e * LANES + LANES - 1]
            @pl.when(last > first)
            def _():
                zero_block((last + ROW_BLOCK - 1) // ROW_BLOCK * ROW_BLOCK - ROW_BLOCK)
            return 0
        lax.fori_loop(0, N_EXPERTS, zfill, 0)
        def ztail(b, _):
            zero_block(b * ROW_BLOCK)
            return 0
        lax.fori_loop(used_ref[0], n_blocks, ztail, 0)

    @pl.when(t >= 2)
    def _():
        _wait_rows(rows_ref, inflight[slot], sems.at[slot])

    pos = pos_ref[...]
    r = lax.broadcasted_iota(jnp.int32, (STAGE_ROWS, pos.shape[1]), 0)
    hit = jnp.logical_or(jnp.logical_or(r == pos[0:1], r == pos[1:2]),
                         jnp.logical_or(r == pos[2:3], r == pos[3:4]))
    perm = jnp.where(hit, 1.0, 0.0).astype(jnp.bfloat16)
    stage[slot] = jnp.dot(perm, h2_ref[...], preferred_element_type=jnp.float32)

    def make_copy(off, dst, n, sem):
        return pltpu.make_async_copy(stage.at[slot, pl.ds(off, n)], rows_ref.at[pl.ds(dst, n)], sem)
    inflight[slot] = _start_group_copies(t, size_ref, goff_ref, gdst_ref, sems.at[slot], make_copy)

    @pl.when(t == n_tiles - 1)
    def _():
        _wait_rows(rows_ref, inflight[1 - slot], sems.at[1 - slot])
        _wait_rows(rows_ref, inflight[slot], sems.at[slot])


def _dispatch_call(size, goff, gdst, used, h2, pos, n_rows):
    n_tok = h2.shape[0]
    ts = ROUTE_TILE
    n_blocks = n_rows // ROW_BLOCK
    zeros = jnp.zeros((ROW_BLOCK, D_MODEL), jnp.float32)
    grid_spec = pltpu.PrefetchScalarGridSpec(
        num_scalar_prefetch=4,
        grid=(n_tok // ts,),
        in_specs=[
            pl.BlockSpec((ts, D_MODEL), lambda t, *_: (t, 0)),
            pl.BlockSpec((TOP_K, ts), lambda t, *_: (0, t)),
            pl.BlockSpec(memory_space=pl.ANY),
        ],
        out_specs=pl.BlockSpec(memory_space=pl.ANY),
        scratch_shapes=[
            pltpu.VMEM((2, STAGE_ROWS, D_MODEL), jnp.float32),
            pltpu.SMEM((2,), jnp.int32),
            pltpu.SemaphoreType.DMA((2,)),
            pltpu.SemaphoreType.DMA(()),
        ])
    return pl.pallas_call(
        functools.partial(_dispatch_kernel, n_blocks),
        grid_spec=grid_spec,
        out_shape=jax.ShapeDtypeStruct((n_rows, D_MODEL), jnp.float32),
        name="dispatch_rows",
        compiler_params=pltpu.CompilerParams(dimension_semantics=("arbitrary",),
                                             vmem_limit_bytes=VMEM_LIMIT_BYTES),
    )(size, goff, gdst, used, h2, pos, zeros)


def _expert_kernel(blk_ref, used_ref, x_ref, w1_ref, b1_ref, w2_ref, b2_ref, y_ref, w1_bf, w2_bf, prev_e):
    i = pl.program_id(0)
    bf16 = jnp.bfloat16
    e = blk_ref[i]

    @pl.when(i == 0)
    def _():
        prev_e[0] = jnp.int32(-1)

    @pl.when(e != prev_e[0])
    def _():
        w1_bf[...] = w1_ref[0].astype(bf16)
        w2_bf[...] = w2_ref[0].astype(bf16)
        prev_e[0] = e

    @pl.when(i >= used_ref[0])
    def _():
        y_ref[...] = jnp.zeros_like(y_ref)

    @pl.when(i < used_ref[0])
    def _():
        hu = jnp.dot(x_ref[...].astype(bf16), w1_bf[...], preferred_element_type=jnp.float32) + b1_ref[0]
        glu = jnp.minimum(hu[:, :D_FF], SWIGLU_LIMIT)
        lin = jnp.clip(hu[:, D_FF:], -SWIGLU_LIMIT, SWIGLU_LIMIT)
        act = glu * (1.0 / (1.0 + jnp.exp(-SWIGLU_ALPHA * glu))) * (lin + 1.0)
        y_ref[...] = jnp.dot(act.astype(bf16), w2_bf[...], preferred_element_type=jnp.float32) + b2_ref[0]


def _expert_call(blk, used, rows, w1, b1, w2, b2):
    n_rows = rows.shape[0]
    n_blocks = n_rows // ROW_BLOCK
    last = lambda i, b, u: jnp.minimum(i, u[0] - 1)
    grid_spec = pltpu.PrefetchScalarGridSpec(
        num_scalar_prefetch=2,
        grid=(n_blocks,),
        in_specs=[
            pl.BlockSpec((ROW_BLOCK, D_MODEL), lambda i, b, u: (last(i, b, u), 0)),
            pl.BlockSpec((1, D_MODEL, 2 * D_FF), lambda i, b, u: (b[i], 0, 0)),
            pl.BlockSpec((1, 1, 2 * D_FF), lambda i, b, u: (b[i], 0, 0)),
            pl.BlockSpec((1, D_FF, D_MODEL), lambda i, b, u: (b[i], 0, 0)),
            pl.BlockSpec((1, 1, D_MODEL), lambda i, b, u: (b[i], 0, 0)),
        ],
        out_specs=pl.BlockSpec((ROW_BLOCK, D_MODEL), lambda i, b, u: (i, 0)),
        scratch_shapes=[
            pltpu.VMEM((D_MODEL, 2 * D_FF), jnp.bfloat16),
            pltpu.VMEM((D_FF, D_MODEL), jnp.bfloat16),
            pltpu.SMEM((1,), jnp.int32),
        ])
    return pl.pallas_call(
        _expert_kernel,
        grid_spec=grid_spec,
        out_shape=jax.ShapeDtypeStruct((n_rows, D_MODEL), jnp.float32),
        name="expert_ffn",
        compiler_params=pltpu.CompilerParams(dimension_semantics=("arbitrary",),
                                             vmem_limit_bytes=VMEM_LIMIT_BYTES),
    )(blk, used, rows, w1, b1.reshape(N_EXPERTS, 1, 2 * D_FF), w2, b2.reshape(N_EXPERTS, 1, D_MODEL))


def _combine_kernel(size_ref, goff_ref, gdst_ref,
                    x1_ref, pos_ref, gate_ref, y_ref, gfin_ref, out_ref,
                    stage, inflight, sems):
    t = pl.program_id(0)
    n_tiles = pl.num_programs(0)
    slot = t % 2

    def fetch(tile, slot_):
        def make_copy(off, src, n, sem):
            return pltpu.make_async_copy(y_ref.at[pl.ds(src, n)], stage.at[slot_, pl.ds(off, n)], sem)
        inflight[slot_] = _start_group_copies(tile, size_ref, goff_ref, gdst_ref, sems.at[slot_], make_copy)

    @pl.when(t == 0)
    def _():
        stage[...] = jnp.zeros_like(stage)
        fetch(0, 0)

    @pl.when(t + 1 < n_tiles)
    def _():
        fetch(t + 1, 1 - slot)

    _wait_rows(y_ref, inflight[slot], sems.at[slot])

    pos = pos_ref[...]
    gate = gate_ref[...]
    r = lax.broadcasted_iota(jnp.int32, (STAGE_ROWS, pos.shape[1]), 0)
    w = jnp.where(r == pos[0:1], gate[0:1], 0.0)
    for k in range(1, TOP_K):
        w = w + jnp.where(r == pos[k:k + 1], gate[k:k + 1], 0.0)
    moe = lax.dot_general(w.astype(jnp.bfloat16), stage[slot].astype(jnp.bfloat16),
                          (((0,), (0,)), ((), ())), preferred_element_type=jnp.float32)
    out_ref[...] = _rms(x1_ref[...] + moe, gfin_ref[...])


def _combine_call(size, goff, gdst, x1, pos, gate, y_rows, g_final):
    n_tok = x1.shape[0]
    ts = ROUTE_TILE
    grid_spec = pltpu.PrefetchScalarGridSpec(
        num_scalar_prefetch=3,
        grid=(n_tok // ts,),
        in_specs=[
            pl.BlockSpec((ts, D_MODEL), lambda t, *_: (t, 0)),
            pl.BlockSpec((TOP_K, ts), lambda t, *_: (0, t)),
            pl.BlockSpec((TOP_K, ts), lambda t, *_: (0, t)),
            pl.BlockSpec(memory_space=pl.ANY),
            pl.BlockSpec((1, D_MODEL), lambda t, *_: (0, 0)),
        ],
        out_specs=pl.BlockSpec((ts, D_MODEL), lambda t, *_: (t, 0)),
        scratch_shapes=[
            pltpu.VMEM((2, STAGE_ROWS, D_MODEL), jnp.float32),
            pltpu.SMEM((2,), jnp.int32),
            pltpu.SemaphoreType.DMA((2,)),
        ])
    return pl.pallas_call(
        _combine_kernel,
        grid_spec=grid_spec,
        out_shape=jax.ShapeDtypeStruct((n_tok, D_MODEL), jnp.float32),
        name="combine_norm",
        compiler_params=pltpu.CompilerParams(dimension_semantics=("arbitrary",),
                                             vmem_limit_bytes=VMEM_LIMIT_BYTES),
    )(size, goff, gdst, x1, pos, gate, y_rows, g_final)


def _attention_bias():
    slopes = jnp.exp2(-8.0 * jnp.arange(1, N_Q_HEADS + 1, dtype=jnp.float32) / N_Q_HEADS)
    qi = jnp.arange(BLOCK)[:, None]
    kj = jnp.arange(BAND)[None, :]
    dist = qi + BLOCK - kj
    valid = (dist >= 0) & (dist < WINDOW)
    b = -slopes[:, None, None] * dist.astype(jnp.float32)[None]
    b = jnp.where(valid[None], b, -jnp.inf)
    b = b.reshape(N_KV_HEADS, GROUP, BLOCK, BAND).transpose(0, 2, 1, 3)
    return b.reshape(N_KV_HEADS, BLOCK, GROUP * BAND)


def kernel(x, g_mix, w_in, conv_w, sinks, g_attn_out, g_conv_out, w_out, g_ffn, w_router, b_router,
           w1, b1, w2, b2, g_final):
    bsz, s_len, d = x.shape
    n_tok = bsz * s_len
    n_tiles = n_tok // ROUTE_TILE
    n_rows = n_tok * TOP_K + n_tiles * N_EXPERTS * SUBLANES + N_EXPERTS * ROW_BLOCK
    n_blocks = n_rows // ROW_BLOCK
    x2d = x.reshape(n_tok, d)
    bias = _attention_bias()
    l = 0
    x1, h2, pos, gate, cnt = _mixer_call(
        x2d, sinks[l], g_mix[l].reshape(1, d), w_in[l].astype(jnp.bfloat16), conv_w[l], bias,
        g_attn_out[l].reshape(1, ATTN_WIDTH), g_conv_out[l].reshape(1, CONV_WIDTH),
        w_out[l].astype(jnp.bfloat16), g_ffn[l].reshape(1, d), w_router[l].T,
        b_router[l].reshape(N_EXPERTS, 1), s_len)
    size, goff, gdst, blk, used = _tables_call(cnt, n_blocks)
    size, goff, gdst = size.reshape(-1), goff.reshape(-1), gdst.reshape(-1)
    blk, used = blk.reshape(-1), used.reshape(-1)[:1]
    rows = _dispatch_call(size, goff, gdst, used, h2, pos, n_rows)
    y_rows = _expert_call(blk, used, rows, w1[l], b1[l], w2[l], b2[l])
    out = _combine_call(size, goff, gdst, x1, pos, gate, y_rows, g_final.reshape(1, d))
    return out.reshape(bsz, s_len, d)
```

```python
import functools
import math

import jax
import jax.numpy as jnp
from jax import lax
from jax.experimental import pallas as pl
from jax.experimental.pallas import tpu as pltpu

D_MODEL = 1024
N_Q_HEADS = 8
N_KV_HEADS = 2
HEAD_DIM = 64
GROUP = N_Q_HEADS // N_KV_HEADS
ATTN_WIDTH = N_Q_HEADS * HEAD_DIM
KV_WIDTH = N_KV_HEADS * HEAD_DIM
CONV_WIDTH = D_MODEL - ATTN_WIDTH
QKV_COLS = ATTN_WIDTH + 2 * KV_WIDTH
IN_COLS = QKV_COLS + 3 * CONV_WIDTH
WINDOW = 128
BLOCK = 128
BAND = 2 * BLOCK
N_EXPERTS = 32
TOP_K = 4
D_FF = D_MODEL
SWIGLU_ALPHA = 1.702
SWIGLU_LIMIT = 7.0
EPS = 1e-5

LANES = 128
SUBLANES = 8
VMEM_LIMIT_BYTES = 56 * 1024 * 1024

TOKEN_TILE = 512
ROUTE_TILE = 256
ROW_BLOCK = 256
STAGE_ROWS = -(-(TOP_K * ROUTE_TILE + N_EXPERTS * (SUBLANES - 1)) // LANES) * LANES


def _rms(x, g):
    return x * lax.rsqrt(jnp.mean(x * x, axis=-1, keepdims=True) + EPS) * g


def _block_diag4(lo_tile, hi_tile):
    z = jnp.zeros_like(lo_tile)
    return jnp.concatenate([
        jnp.concatenate([lo_tile, z], axis=1),
        jnp.concatenate([hi_tile, z], axis=1),
        jnp.concatenate([z, lo_tile], axis=1),
        jnp.concatenate([z, hi_tile], axis=1)], axis=0)


def _mixer_kernel(tiles_per_seq,
                  sink_ref,
                  x_ref, gmix_ref, win_ref, convw_ref, bias_ref, gattn_ref, gconv_ref, wout_ref,
                  gffn_ref, wrt_ref, br_ref,
                  x1_ref, h2_ref, pos_ref, gate_ref, cnt_ref,
                  p_scr, kv_scr, u_scr, attn_scr, tri_scr):
    tm = x_ref.shape[0]
    ts = ROUTE_TILE
    i = pl.program_id(0)
    first = (i % tiles_per_seq) == 0
    f32, bf16 = jnp.float32, jnp.bfloat16

    @pl.when(i == 0)
    def _():
        cnt_ref[...] = jnp.zeros_like(cnt_ref)
        r = lax.broadcasted_iota(jnp.int32, (ts, ts), 0)
        c = lax.broadcasted_iota(jnp.int32, (ts, ts), 1)
        tri_scr[...] = jnp.where(r < c, 1.0, 0.0).astype(bf16)

    @pl.when(first)
    def _():
        kv_scr[0:BLOCK, :] = jnp.zeros((BLOCK, kv_scr.shape[1]), bf16)
        u_scr[0:SUBLANES, :] = jnp.zeros((SUBLANES, CONV_WIDTH), f32)

    x = x_ref[...]
    h = _rms(x, gmix_ref[...])
    p_scr[...] = jnp.dot(h.astype(bf16), win_ref[...], preferred_element_type=f32)

    kk = p_scr[:, ATTN_WIDTH:ATTN_WIDTH + KV_WIDTH]
    vv = p_scr[:, ATTN_WIDTH + KV_WIDTH:QKV_COLS]
    kr = pltpu.roll(kk, HEAD_DIM, axis=1)
    vr = pltpu.roll(vv, HEAD_DIM, axis=1)
    kv_scr[BLOCK:BLOCK + tm, :] = jnp.concatenate([kk, kr, vv, vr], axis=1).astype(bf16)

    lane = lax.broadcasted_iota(jnp.int32, (BAND, LANES), 1)
    lo = lane < HEAD_DIM
    col = lax.broadcasted_iota(jnp.int32, (BLOCK, GROUP * BAND), 1)
    prev_half = (col % BAND) < BLOCK
    scale = 1.0 / math.sqrt(HEAD_DIM)
    zero_t = jnp.zeros((BAND, LANES), bf16)

    for j in range(tm // BLOCK):
        band = kv_scr[j * BLOCK:j * BLOCK + BAND, :]
        bk, bkr = band[:, 0:LANES], band[:, LANES:2 * LANES]
        bv, bvr = band[:, 2 * LANES:3 * LANES], band[:, 3 * LANES:4 * LANES]
        for hh in range(N_KV_HEADS):
            if hh == 0:
                k_lo, k_hi = jnp.where(lo, bk, zero_t), jnp.where(lo, zero_t, bkr)
                v_lo, v_hi = jnp.where(lo, bv, zero_t), jnp.where(lo, zero_t, bvr)
            else:
                k_lo, k_hi = jnp.where(lo, bkr, zero_t), jnp.where(lo, zero_t, bk)
                v_lo, v_hi = jnp.where(lo, bvr, zero_t), jnp.where(lo, zero_t, bv)
            k4 = _block_diag4(k_lo, k_hi)
            v4 = _block_diag4(v_lo, v_hi)
            qh = (p_scr[j * BLOCK:(j + 1) * BLOCK, hh * 2 * LANES:(hh + 1) * 2 * LANES] * scale).astype(bf16)
            s = lax.dot_general(qh, k4, (((1,), (1,)), ((), ())), preferred_element_type=f32)
            s = s + bias_ref[hh]
            if j == 0:
                s = jnp.where(jnp.logical_and(first, prev_half), -jnp.inf, s)
            probs = []
            for g in range(GROUP):
                sg = s[:, g * BAND:(g + 1) * BAND]
                sink = sink_ref[hh * GROUP + g]
                m = jnp.maximum(jnp.max(sg, axis=-1, keepdims=True), sink)
                e = jnp.exp(sg - m)
                den = jnp.sum(e, axis=-1, keepdims=True) + jnp.exp(sink - m)
                probs.append((e * (1.0 / den)).astype(bf16))
            pmat = jnp.concatenate(probs, axis=1)
            o = jnp.dot(pmat, v4, preferred_element_type=f32)
            attn_scr[j * BLOCK:(j + 1) * BLOCK, hh * 2 * LANES:(hh + 1) * 2 * LANES] = o

    bg = p_scr[:, QKV_COLS:QKV_COLS + CONV_WIDTH]
    cg = p_scr[:, QKV_COLS + CONV_WIDTH:QKV_COLS + 2 * CONV_WIDTH]
    xin = p_scr[:, QKV_COLS + 2 * CONV_WIDTH:IN_COLS]
    u = cg * xin
    u_scr[SUBLANES:SUBLANES + tm, :] = u
    u1 = u_scr[SUBLANES - 1:SUBLANES - 1 + tm, :]
    u2 = u_scr[SUBLANES - 2:SUBLANES - 2 + tm, :]
    y = convw_ref[0:1, :] * u2 + convw_ref[1:2, :] * u1 + convw_ref[2:3, :] * u
    conv = bg * y
    kv_scr[0:BLOCK, :] = kv_scr[tm:tm + BLOCK, :]
    u_scr[0:SUBLANES, :] = u_scr[tm:tm + SUBLANES, :]

    mix = jnp.concatenate([_rms(attn_scr[...], gattn_ref[...]), _rms(conv, gconv_ref[...])], axis=1)
    x1 = x + jnp.dot(mix.astype(bf16), wout_ref[...], preferred_element_type=f32)
    x1_ref[...] = x1
    h2 = _rms(x1, gffn_ref[...])
    h2_ref[...] = h2.astype(bf16)

    logits = lax.dot_general(wrt_ref[...], h2, (((1,), (1,)), ((), ())),
                             precision=lax.Precision.HIGHEST, preferred_element_type=f32) + br_ref[...]
    row = lax.broadcasted_iota(jnp.int32, (N_EXPERTS, tm), 0)
    vals, idxs = [], []
    l = logits
    for _ in range(TOP_K):
        m = jnp.max(l, axis=0, keepdims=True)
        idx = jnp.min(jnp.where(l == m, row, N_EXPERTS), axis=0, keepdims=True)
        vals.append(m)
        idxs.append(idx)
        l = jnp.where(row == idx, -jnp.inf, l)
    es = [jnp.exp(v - vals[0]) for v in vals]
    inv = 1.0 / (es[0] + es[1] + es[2] + es[3])
    gate_ref[...] = jnp.concatenate([e * inv for e in es], axis=0)

    onehots = [row == idx for idx in idxs]
    sel = jnp.logical_or(jnp.logical_or(onehots[0], onehots[1]), jnp.logical_or(onehots[2], onehots[3]))
    oh = jnp.where(sel, 1.0, 0.0)

    er = lax.broadcasted_iota(jnp.int32, (N_EXPERTS, N_EXPERTS), 0)
    ec = lax.broadcasted_iota(jnp.int32, (N_EXPERTS, N_EXPERTS), 1)
    lower = jnp.where(ec < er, 1.0, 0.0).astype(bf16)
    cnt_lane = lax.broadcasted_iota(jnp.int32, cnt_ref.shape, 1)
    poss = []
    for sub in range(tm // ts):
        sl = slice(sub * ts, (sub + 1) * ts)
        oh_s = oh[:, sl]
        cum = jnp.dot(oh_s.astype(bf16), tri_scr[...], preferred_element_type=f32)
        cnt = jnp.sum(oh_s, axis=1, keepdims=True)
        units = jnp.floor((cnt + (SUBLANES - 1.0)) * (1.0 / SUBLANES))
        goff = SUBLANES * jnp.dot(lower, jnp.broadcast_to(units, (N_EXPERTS, LANES)).astype(bf16),
                                  preferred_element_type=f32)[:, 0:1]
        pos = goff + cum
        poss.append(jnp.concatenate(
            [jnp.sum(jnp.where(o[:, sl], pos, 0.0), axis=0, keepdims=True) for o in onehots], axis=0))
        tile_id = i * (tm // ts) + sub
        cnt_ref[...] = jnp.where(cnt_lane == tile_id, cnt.astype(jnp.int32), cnt_ref[...])
    pos_ref[...] = jnp.concatenate(poss, axis=1).astype(jnp.int32)


def _mixer_call(x2d, sinks, g_mix, w_in, conv_w, bias, g_attn, g_conv, w_out, g_ffn, wrt, br, seq_len):
    n_tok = x2d.shape[0]
    tm = TOKEN_TILE
    n_tiles = n_tok // tm
    assert n_tok // ROUTE_TILE == LANES, "one lane of the count table per routing tile"
    const = lambda shape: pl.BlockSpec(shape, lambda i, *_: (0,) * len(shape))
    grid_spec = pltpu.PrefetchScalarGridSpec(
        num_scalar_prefetch=0,
        grid=(n_tiles,),
        in_specs=[
            pl.BlockSpec(memory_space=pltpu.SMEM),
            pl.BlockSpec((tm, D_MODEL), lambda i: (i, 0)),
            const((1, D_MODEL)), const((D_MODEL, IN_COLS)), const((3, CONV_WIDTH)),
            const((N_KV_HEADS, BLOCK, GROUP * BAND)),
            const((1, ATTN_WIDTH)), const((1, CONV_WIDTH)), const((D_MODEL, D_MODEL)),
            const((1, D_MODEL)), const((N_EXPERTS, D_MODEL)), const((N_EXPERTS, 1)),
        ],
        out_specs=[
            pl.BlockSpec((tm, D_MODEL), lambda i: (i, 0)),
            pl.BlockSpec((tm, D_MODEL), lambda i: (i, 0)),
            pl.BlockSpec((TOP_K, tm), lambda i: (0, i)),
            pl.BlockSpec((TOP_K, tm), lambda i: (0, i)),
            pl.BlockSpec((N_EXPERTS, LANES), lambda i: (0, 0)),
        ],
        scratch_shapes=[
            pltpu.VMEM((tm, IN_COLS), jnp.float32),
            pltpu.VMEM((BLOCK + tm, 4 * LANES), jnp.bfloat16),
            pltpu.VMEM((SUBLANES + tm, CONV_WIDTH), jnp.float32),
            pltpu.VMEM((tm, ATTN_WIDTH), jnp.float32),
            pltpu.VMEM((ROUTE_TILE, ROUTE_TILE), jnp.bfloat16),
        ])
    out_shape = [
        jax.ShapeDtypeStruct((n_tok, D_MODEL), jnp.float32),
        jax.ShapeDtypeStruct((n_tok, D_MODEL), jnp.bfloat16),
        jax.ShapeDtypeStruct((TOP_K, n_tok), jnp.int32),
        jax.ShapeDtypeStruct((TOP_K, n_tok), jnp.float32),
        jax.ShapeDtypeStruct((N_EXPERTS, LANES), jnp.int32),
    ]
    return pl.pallas_call(
        functools.partial(_mixer_kernel, seq_len // tm),
        grid_spec=grid_spec, out_shape=out_shape, name="mixer_router",
        compiler_params=pltpu.CompilerParams(dimension_semantics=("arbitrary",),
                                             vmem_limit_bytes=VMEM_LIMIT_BYTES),
    )(sinks, x2d, g_mix, w_in, conv_w, bias, g_attn, g_conv, w_out, g_ffn, wrt, br)


def _excl_cumsum(a, axis):
    idx = lax.broadcasted_iota(jnp.int32, a.shape, axis)
    inc = a
    step = 1
    while step < a.shape[axis]:
        inc = inc + jnp.where(idx >= step, pltpu.roll(inc, step, axis), 0)
        step *= 2
    return inc - a


def _tables_kernel(cnt_ref, size_ref, goff_ref, gdst_ref, blk_ref, used_ref):
    size = ((cnt_ref[...] + (SUBLANES - 1)) // SUBLANES) * SUBLANES
    size_ref[...] = size
    goff_ref[...] = _excl_cumsum(size, 0)
    total = jnp.broadcast_to(jnp.sum(size, axis=1, keepdims=True), size.shape)
    region = ((total + (ROW_BLOCK - 1)) // ROW_BLOCK) * ROW_BLOCK
    pstart = _excl_cumsum(region, 0)
    gdst_ref[...] = pstart + _excl_cumsum(size, 1)
    pend = (pstart + region)[:, 0:1]
    start = lax.broadcasted_iota(jnp.int32, (N_EXPERTS, blk_ref.shape[1]), 1) * ROW_BLOCK
    owner = jnp.sum(jnp.where(pend <= start, 1, 0), axis=0, keepdims=True)
    blk_ref[...] = jnp.minimum(owner, N_EXPERTS - 1)
    used_ref[...] = jnp.broadcast_to(jnp.max(pend, axis=0, keepdims=True) // ROW_BLOCK, used_ref.shape)


def _tables_call(cnt, n_blocks):
    nb_pad = -(-n_blocks // LANES) * LANES
    tbl = jax.ShapeDtypeStruct((N_EXPERTS, LANES), jnp.int32)
    return pl.pallas_call(
        _tables_kernel,
        out_shape=[tbl, tbl, tbl, jax.ShapeDtypeStruct((1, nb_pad), jnp.int32),
                   jax.ShapeDtypeStruct((1, LANES), jnp.int32)],
        name="group_tables",
    )(cnt)


def _start_group_copies(tile, size_ref, goff_ref, gdst_ref, sem, make_copy):
    def body(e, rows):
        n = pl.multiple_of(size_ref[e * LANES + tile], SUBLANES)
        @pl.when(n > 0)
        def _():
            off = pl.multiple_of(goff_ref[e * LANES + tile], SUBLANES)
            dst = pl.multiple_of(gdst_ref[e * LANES + tile], SUBLANES)
            make_copy(off, dst, n, sem).start()
        return rows + n
    return lax.fori_loop(0, N_EXPERTS, body, jnp.int32(0))


def _wait_rows(rows_hbm, n, sem):
    @pl.when(n > 0)
    def _():
        m = pl.multiple_of(n, SUBLANES)
        half = rows_hbm.shape[0] // (2 * SUBLANES) * SUBLANES
        pltpu.make_async_copy(rows_hbm.at[pl.ds(0, m)], rows_hbm.at[pl.ds(half, m)], sem).wait()


def _dispatch_kernel(n_blocks,
                     size_ref, goff_ref, gdst_ref, used_ref,
                     h2_ref, pos_ref, rows_ref,
                     stage, zbuf, inflight, sems, zsem):
    t = pl.program_id(0)
    n_tiles = pl.num_programs(0)
    slot = t % 2

    @pl.when(t == 0)
    def _():
        zbuf[...] = jnp.zeros_like(zbuf)
        def zero_block(start):
            pltpu.make_async_copy(
                zbuf, rows_ref.at[pl.ds(pl.multiple_of(start, ROW_BLOCK), ROW_BLOCK)], zsem).start()
        def zfill(e, n):
            first = gdst_ref[e * LANES]
            last = gdst_ref[e * LANES + LANES - 1] + size_ref[e * LANES + LANES - 1]
            @pl.when(last > first)
            def _():
                zero_block((last + ROW_BLOCK - 1) // ROW_BLOCK * ROW_BLOCK - ROW_BLOCK)
            return n + jnp.where(last > first, ROW_BLOCK, 0)
        n_zero = lax.fori_loop(0, N_EXPERTS, zfill, jnp.int32(0))
        def ztail(b, _):
            zero_block(b * ROW_BLOCK)
            return 0
        lax.fori_loop(used_ref[0], n_blocks, ztail, 0)
        _wait_rows(rows_ref, n_zero + (n_blocks - used_ref[0]) * ROW_BLOCK, zsem)

    @pl.when(t >= 2)
    def _():
        _wait_rows(rows_ref, inflight[slot], sems.at[slot])

    pos = pos_ref[...]
    r = lax.broadcasted_iota(jnp.int32, (STAGE_ROWS, pos.shape[1]), 0)
    hit = jnp.logical_or(jnp.logical_or(r == pos[0:1], r == pos[1:2]),
                         jnp.logical_or(r == pos[2:3], r == pos[3:4]))
    perm = jnp.where(hit, 1.0, 0.0).astype(jnp.bfloat16)
    stage[slot] = jnp.dot(perm, h2_ref[...], preferred_element_type=jnp.float32)

    def make_copy(off, dst, n, sem):
        return pltpu.make_async_copy(stage.at[slot, pl.ds(off, n)], rows_ref.at[pl.ds(dst, n)], sem)
    inflight[slot] = _start_group_copies(t, size_ref, goff_ref, gdst_ref, sems.at[slot], make_copy)

    @pl.when(t == n_tiles - 1)
    def _():
        _wait_rows(rows_ref, inflight[1 - slot], sems.at[1 - slot])
        _wait_rows(rows_ref, inflight[slot], sems.at[slot])


def _dispatch_call(size, goff, gdst, used, h2, pos, n_rows):
    n_tok = h2.shape[0]
    ts = ROUTE_TILE
    n_blocks = n_rows // ROW_BLOCK
    grid_spec = pltpu.PrefetchScalarGridSpec(
        num_scalar_prefetch=4,
        grid=(n_tok // ts,),
        in_specs=[
            pl.BlockSpec((ts, D_MODEL), lambda t, *_: (t, 0)),
            pl.BlockSpec((TOP_K, ts), lambda t, *_: (0, t)),
        ],
        out_specs=pl.BlockSpec(memory_space=pl.ANY),
        scratch_shapes=[
            pltpu.VMEM((2, STAGE_ROWS, D_MODEL), jnp.float32),
            pltpu.VMEM((ROW_BLOCK, D_MODEL), jnp.float32),
            pltpu.SMEM((2,), jnp.int32),
            pltpu.SemaphoreType.DMA((2,)),
            pltpu.SemaphoreType.DMA(()),
        ])
    return pl.pallas_call(
        functools.partial(_dispatch_kernel, n_blocks),
        grid_spec=grid_spec,
        out_shape=jax.ShapeDtypeStruct((n_rows, D_MODEL), jnp.float32),
        name="dispatch_rows",
        compiler_params=pltpu.CompilerParams(dimension_semantics=("arbitrary",),
                                             vmem_limit_bytes=VMEM_LIMIT_BYTES),
    )(size, goff, gdst, used, h2, pos)


def _expert_kernel(blk_ref, used_ref, x_ref, w1_ref, b1_ref, w2_ref, b2_ref, y_ref, w1_bf, w2_bf, prev_e):
    i = pl.program_id(0)
    bf16 = jnp.bfloat16
    e = blk_ref[i]

    @pl.when(i == 0)
    def _():
        prev_e[0] = jnp.int32(-1)

    @pl.when(e != prev_e[0])
    def _():
        w1_bf[...] = w1_ref[0].astype(bf16)
        w2_bf[...] = w2_ref[0].astype(bf16)
        prev_e[0] = e

    @pl.when(i >= used_ref[0])
    def _():
        y_ref[...] = jnp.zeros_like(y_ref)

    @pl.when(i < used_ref[0])
    def _():
        hu = jnp.dot(x_ref[...].astype(bf16), w1_bf[...], preferred_element_type=jnp.float32) + b1_ref[0]
        glu = jnp.minimum(hu[:, :D_FF], SWIGLU_LIMIT)
        lin = jnp.clip(hu[:, D_FF:], -SWIGLU_LIMIT, SWIGLU_LIMIT)
        act = glu * (1.0 / (1.0 + jnp.exp(-SWIGLU_ALPHA * glu))) * (lin + 1.0)
        y_ref[...] = jnp.dot(act.astype(bf16), w2_bf[...], preferred_element_type=jnp.float32) + b2_ref[0]


def _expert_call(blk, used, rows, w1, b1, w2, b2):
    n_rows = rows.shape[0]
    n_blocks = n_rows // ROW_BLOCK
    last = lambda i, b, u: jnp.minimum(i, u[0] - 1)
    grid_spec = pltpu.PrefetchScalarGridSpec(
        num_scalar_prefetch=2,
        grid=(n_blocks,),
        in_specs=[
            pl.BlockSpec((ROW_BLOCK, D_MODEL), lambda i, b, u: (last(i, b, u), 0)),
            pl.BlockSpec((1, D_MODEL, 2 * D_FF), lambda i, b, u: (b[i], 0, 0)),
            pl.BlockSpec((1, 1, 2 * D_FF), lambda i, b, u: (b[i], 0, 0)),
            pl.BlockSpec((1, D_FF, D_MODEL), lambda i, b, u: (b[i], 0, 0)),
            pl.BlockSpec((1, 1, D_MODEL), lambda i, b, u: (b[i], 0, 0)),
        ],
        out_specs=pl.BlockSpec((ROW_BLOCK, D_MODEL), lambda i, b, u: (i, 0)),
        scratch_shapes=[
            pltpu.VMEM((D_MODEL, 2 * D_FF), jnp.bfloat16),
            pltpu.VMEM((D_FF, D_MODEL), jnp.bfloat16),
            pltpu.SMEM((1,), jnp.int32),
        ])
    return pl.pallas_call(
        _expert_kernel,
        grid_spec=grid_spec,
        out_shape=jax.ShapeDtypeStruct((n_rows, D_MODEL), jnp.float32),
        name="expert_ffn",
        compiler_params=pltpu.CompilerParams(dimension_semantics=("arbitrary",),
                                             vmem_limit_bytes=VMEM_LIMIT_BYTES),
    )(blk, used, rows, w1, b1.reshape(N_EXPERTS, 1, 2 * D_FF), w2, b2.reshape(N_EXPERTS, 1, D_MODEL))


def _combine_kernel(size_ref, goff_ref, gdst_ref,
                    x1_ref, pos_ref, gate_ref, y_ref, gfin_ref, out_ref,
                    stage, inflight, sems):
    t = pl.program_id(0)
    n_tiles = pl.num_programs(0)
    slot = t % 2

    def fetch(tile, slot_):
        def make_copy(off, src, n, sem):
            return pltpu.make_async_copy(y_ref.at[pl.ds(src, n)], stage.at[slot_, pl.ds(off, n)], sem)
        inflight[slot_] = _start_group_copies(tile, size_ref, goff_ref, gdst_ref, sems.at[slot_], make_copy)

    @pl.when(t == 0)
    def _():
        stage[...] = jnp.zeros_like(stage)
        fetch(0, 0)

    @pl.when(t + 1 < n_tiles)
    def _():
        fetch(t + 1, 1 - slot)

    _wait_rows(y_ref, inflight[slot], sems.at[slot])

    pos = pos_ref[...]
    gate = gate_ref[...]
    r = lax.broadcasted_iota(jnp.int32, (STAGE_ROWS, pos.shape[1]), 0)
    w = jnp.where(r == pos[0:1], gate[0:1], 0.0)
    for k in range(1, TOP_K):
        w = w + jnp.where(r == pos[k:k + 1], gate[k:k + 1], 0.0)
    moe = lax.dot_general(w.astype(jnp.bfloat16), stage[slot].astype(jnp.bfloat16),
                          (((0,), (0,)), ((), ())), preferred_element_type=jnp.float32)
    out_ref[...] = _rms(x1_ref[...] + moe, gfin_ref[...])


def _combine_call(size, goff, gdst, x1, pos, gate, y_rows, g_final):
    n_tok = x1.shape[0]
    ts = ROUTE_TILE
    grid_spec = pltpu.PrefetchScalarGridSpec(
        num_scalar_prefetch=3,
        grid=(n_tok // ts,),
        in_specs=[
            pl.BlockSpec((ts, D_MODEL), lambda t, *_: (t, 0)),
            pl.BlockSpec((TOP_K, ts), lambda t, *_: (0, t)),
            pl.BlockSpec((TOP_K, ts), lambda t, *_: (0, t)),
            pl.BlockSpec(memory_space=pl.ANY),
            pl.BlockSpec((1, D_MODEL), lambda t, *_: (0, 0)),
        ],
        out_specs=pl.BlockSpec((ts, D_MODEL), lambda t, *_: (t, 0)),
        scratch_shapes=[
            pltpu.VMEM((2, STAGE_ROWS, D_MODEL), jnp.float32),
            pltpu.SMEM((2,), jnp.int32),
            pltpu.SemaphoreType.DMA((2,)),
        ])
    return pl.pallas_call(
        _combine_kernel,
        grid_spec=grid_spec,
        out_shape=jax.ShapeDtypeStruct((n_tok, D_MODEL), jnp.float32),
        name="combine_norm",
        compiler_params=pltpu.CompilerParams(dimension_semantics=("arbitrary",),
                                             vmem_limit_bytes=VMEM_LIMIT_BYTES),
    )(size, goff, gdst, x1, pos, gate, y_rows, g_final)


def _attention_bias():
    slopes = jnp.exp2(-8.0 * jnp.arange(1, N_Q_HEADS + 1, dtype=jnp.float32) / N_Q_HEADS)
    qi = jnp.arange(BLOCK)[:, None]
    kj = jnp.arange(BAND)[None, :]
    dist = qi + BLOCK - kj
    valid = (dist >= 0) & (dist < WINDOW)
    b = -slopes[:, None, None] * dist.astype(jnp.float32)[None]
    b = jnp.where(valid[None], b, -jnp.inf)
    b = b.reshape(N_KV_HEADS, GROUP, BLOCK, BAND).transpose(0, 2, 1, 3)
    return b.reshape(N_KV_HEADS, BLOCK, GROUP * BAND)


def kernel(x, g_mix, w_in, conv_w, sinks, g_attn_out, g_conv_out, w_out, g_ffn, w_router, b_router,
           w1, b1, w2, b2, g_final):
    bsz, s_len, d = x.shape
    n_tok = bsz * s_len
    n_tiles = n_tok // ROUTE_TILE
    n_rows = n_tok * TOP_K + n_tiles * N_EXPERTS * SUBLANES + N_EXPERTS * ROW_BLOCK
    n_blocks = n_rows // ROW_BLOCK
    x2d = x.reshape(n_tok, d)
    bias = _attention_bias()
    l = 0
    x1, h2, pos, gate, cnt = _mixer_call(
        x2d, sinks[l], g_mix[l].reshape(1, d), w_in[l].astype(jnp.bfloat16), conv_w[l], bias,
        g_attn_out[l].reshape(1, ATTN_WIDTH), g_conv_out[l].reshape(1, CONV_WIDTH),
        w_out[l].astype(jnp.bfloat16), g_ffn[l].reshape(1, d), w_router[l].T,
        b_router[l].reshape(N_EXPERTS, 1), s_len)
    size, goff, gdst, blk, used = _tables_call(cnt, n_blocks)
    size, goff, gdst = size.reshape(-1), goff.reshape(-1), gdst.reshape(-1)
    blk, used = blk.reshape(-1), used.reshape(-1)[:1]
    rows = _dispatch_call(size, goff, gdst, used, h2, pos, n_rows)
    y_rows = _expert_call(blk, used, rows, w1[l], b1[l], w2[l], b2[l])
    out = _combine_call(size, goff, gdst, x1, pos, gate, y_rows, g_final.reshape(1, d))
    return out.reshape(bsz, s_len, d)
```

```python
import functools
import math

import jax
import jax.numpy as jnp
from jax import lax
from jax.experimental import pallas as pl
from jax.experimental.pallas import tpu as pltpu

D_MODEL = 1024
N_Q_HEADS = 8
N_KV_HEADS = 2
HEAD_DIM = 64
GROUP = N_Q_HEADS // N_KV_HEADS
ATTN_WIDTH = N_Q_HEADS * HEAD_DIM
KV_WIDTH = N_KV_HEADS * HEAD_DIM
CONV_WIDTH = D_MODEL - ATTN_WIDTH
QKV_COLS = ATTN_WIDTH + 2 * KV_WIDTH
IN_COLS = QKV_COLS + 3 * CONV_WIDTH
WINDOW = 128
BLOCK = 128
BAND = 2 * BLOCK
N_EXPERTS = 32
TOP_K = 4
D_FF = D_MODEL
SWIGLU_ALPHA = 1.702
SWIGLU_LIMIT = 7.0
EPS = 1e-5

LANES = 128
SUBLANES = 8
VMEM_LIMIT_BYTES = 56 * 1024 * 1024

TOKEN_TILE = 512
ROUTE_TILE = 256
ROW_BLOCK = 512
STAGE_ROWS = -(-(TOP_K * ROUTE_TILE + N_EXPERTS * (SUBLANES - 1)) // LANES) * LANES


def _rms(x, g):
    return x * lax.rsqrt(jnp.mean(x * x, axis=-1, keepdims=True) + EPS) * g


def _block_diag4(lo_tile, hi_tile):
    z = jnp.zeros_like(lo_tile)
    return jnp.concatenate([
        jnp.concatenate([lo_tile, z], axis=1),
        jnp.concatenate([hi_tile, z], axis=1),
        jnp.concatenate([z, lo_tile], axis=1),
        jnp.concatenate([z, hi_tile], axis=1)], axis=0)


def _mixer_kernel(tiles_per_seq,
                  sink_ref,
                  x_ref, gmix_ref, win_ref, convw_ref, bias_ref, gattn_ref, gconv_ref, wout_ref,
                  gffn_ref, wrt_ref, br_ref,
                  x1_ref, h2_ref, pos_ref, gate_ref, cnt_ref,
                  p_scr, kv_scr, u_scr, attn_scr, tri_scr):
    tm = x_ref.shape[0]
    ts = ROUTE_TILE
    i = pl.program_id(0)
    first = (i % tiles_per_seq) == 0
    f32, bf16 = jnp.float32, jnp.bfloat16

    @pl.when(i == 0)
    def _():
        cnt_ref[...] = jnp.zeros_like(cnt_ref)
        r = lax.broadcasted_iota(jnp.int32, (ts, ts), 0)
        c = lax.broadcasted_iota(jnp.int32, (ts, ts), 1)
        tri_scr[...] = jnp.where(r < c, 1.0, 0.0).astype(bf16)

    @pl.when(first)
    def _():
        kv_scr[0:BLOCK, :] = jnp.zeros((BLOCK, kv_scr.shape[1]), bf16)
        u_scr[0:SUBLANES, :] = jnp.zeros((SUBLANES, CONV_WIDTH), f32)

    x = x_ref[...]
    h = _rms(x, gmix_ref[...])
    p_scr[...] = jnp.dot(h.astype(bf16), win_ref[...], preferred_element_type=f32)

    kk = p_scr[:, ATTN_WIDTH:ATTN_WIDTH + KV_WIDTH]
    vv = p_scr[:, ATTN_WIDTH + KV_WIDTH:QKV_COLS]
    kr = pltpu.roll(kk, HEAD_DIM, axis=1)
    vr = pltpu.roll(vv, HEAD_DIM, axis=1)
    kv_scr[BLOCK:BLOCK + tm, :] = jnp.concatenate([kk, kr, vv, vr], axis=1).astype(bf16)

    lane = lax.broadcasted_iota(jnp.int32, (BAND, LANES), 1)
    lo = lane < HEAD_DIM
    col = lax.broadcasted_iota(jnp.int32, (BLOCK, GROUP * BAND), 1)
    prev_half = (col % BAND) < BLOCK
    scale = 1.0 / math.sqrt(HEAD_DIM)
    zero_t = jnp.zeros((BAND, LANES), bf16)

    for j in range(tm // BLOCK):
        band = kv_scr[j * BLOCK:j * BLOCK + BAND, :]
        bk, bkr = band[:, 0:LANES], band[:, LANES:2 * LANES]
        bv, bvr = band[:, 2 * LANES:3 * LANES], band[:, 3 * LANES:4 * LANES]
        for hh in range(N_KV_HEADS):
            if hh == 0:
                k_lo, k_hi = jnp.where(lo, bk, zero_t), jnp.where(lo, zero_t, bkr)
                v_lo, v_hi = jnp.where(lo, bv, zero_t), jnp.where(lo, zero_t, bvr)
            else:
                k_lo, k_hi = jnp.where(lo, bkr, zero_t), jnp.where(lo, zero_t, bk)
                v_lo, v_hi = jnp.where(lo, bvr, zero_t), jnp.where(lo, zero_t, bv)
            k4 = _block_diag4(k_lo, k_hi)
            v4 = _block_diag4(v_lo, v_hi)
            qh = (p_scr[j * BLOCK:(j + 1) * BLOCK, hh * 2 * LANES:(hh + 1) * 2 * LANES] * scale).astype(bf16)
            s = lax.dot_general(qh, k4, (((1,), (1,)), ((), ())), preferred_element_type=f32)
            s = s + bias_ref[hh]
            if j == 0:
                s = jnp.where(jnp.logical_and(first, prev_half), -jnp.inf, s)
            probs = []
            for g in range(GROUP):
                sg = s[:, g * BAND:(g + 1) * BAND]
                sink = sink_ref[hh * GROUP + g]
                m = jnp.maximum(jnp.max(sg, axis=-1, keepdims=True), sink)
                e = jnp.exp(sg - m)
                den = jnp.sum(e, axis=-1, keepdims=True) + jnp.exp(sink - m)
                probs.append((e * (1.0 / den)).astype(bf16))
            pmat = jnp.concatenate(probs, axis=1)
            o = jnp.dot(pmat, v4, preferred_element_type=f32)
            attn_scr[j * BLOCK:(j + 1) * BLOCK, hh * 2 * LANES:(hh + 1) * 2 * LANES] = o

    bg = p_scr[:, QKV_COLS:QKV_COLS + CONV_WIDTH]
    cg = p_scr[:, QKV_COLS + CONV_WIDTH:QKV_COLS + 2 * CONV_WIDTH]
    xin = p_scr[:, QKV_COLS + 2 * CONV_WIDTH:IN_COLS]
    u = cg * xin
    u_scr[SUBLANES:SUBLANES + tm, :] = u
    u1 = u_scr[SUBLANES - 1:SUBLANES - 1 + tm, :]
    u2 = u_scr[SUBLANES - 2:SUBLANES - 2 + tm, :]
    y = convw_ref[0:1, :] * u2 + convw_ref[1:2, :] * u1 + convw_ref[2:3, :] * u
    conv = bg * y
    kv_scr[0:BLOCK, :] = kv_scr[tm:tm + BLOCK, :]
    u_scr[0:SUBLANES, :] = u_scr[tm:tm + SUBLANES, :]

    mix = jnp.concatenate([_rms(attn_scr[...], gattn_ref[...]), _rms(conv, gconv_ref[...])], axis=1)
    x1 = x + jnp.dot(mix.astype(bf16), wout_ref[...], preferred_element_type=f32)
    x1_ref[...] = x1
    h2 = _rms(x1, gffn_ref[...])
    h2_ref[...] = h2.astype(bf16)

    logits = lax.dot_general(wrt_ref[...], h2, (((1,), (1,)), ((), ())),
                             precision=lax.Precision.HIGHEST, preferred_element_type=f32) + br_ref[...]
    row = lax.broadcasted_iota(jnp.int32, (N_EXPERTS, tm), 0)
    vals, idxs = [], []
    l = logits
    for _ in range(TOP_K):
        m = jnp.max(l, axis=0, keepdims=True)
        idx = jnp.min(jnp.where(l == m, row, N_EXPERTS), axis=0, keepdims=True)
        vals.append(m)
        idxs.append(idx)
        l = jnp.where(row == idx, -jnp.inf, l)
    es = [jnp.exp(v - vals[0]) for v in vals]
    inv = 1.0 / (es[0] + es[1] + es[2] + es[3])
    gate_ref[...] = jnp.concatenate([e * inv for e in es], axis=0)

    onehots = [row == idx for idx in idxs]
    sel = jnp.logical_or(jnp.logical_or(onehots[0], onehots[1]), jnp.logical_or(onehots[2], onehots[3]))
    oh = jnp.where(sel, 1.0, 0.0)

    er = lax.broadcasted_iota(jnp.int32, (N_EXPERTS, N_EXPERTS), 0)
    ec = lax.broadcasted_iota(jnp.int32, (N_EXPERTS, N_EXPERTS), 1)
    lower = jnp.where(ec < er, 1.0, 0.0).astype(bf16)
    cnt_lane = lax.broadcasted_iota(jnp.int32, cnt_ref.shape, 1)
    poss = []
    for sub in range(tm // ts):
        sl = slice(sub * ts, (sub + 1) * ts)
        oh_s = oh[:, sl]
        cum = jnp.dot(oh_s.astype(bf16), tri_scr[...], preferred_element_type=f32)
        cnt = jnp.sum(oh_s, axis=1, keepdims=True)
        units = jnp.floor((cnt + (SUBLANES - 1.0)) * (1.0 / SUBLANES))
        goff = SUBLANES * jnp.dot(lower, jnp.broadcast_to(units, (N_EXPERTS, LANES)).astype(bf16),
                                  preferred_element_type=f32)[:, 0:1]
        pos = goff + cum
        poss.append(jnp.concatenate(
            [jnp.sum(jnp.where(o[:, sl], pos, 0.0), axis=0, keepdims=True) for o in onehots], axis=0))
        tile_id = i * (tm // ts) + sub
        cnt_ref[...] = jnp.where(cnt_lane == tile_id, cnt.astype(jnp.int32), cnt_ref[...])
    pos_ref[...] = jnp.concatenate(poss, axis=1).astype(jnp.int32)


def _mixer_call(x2d, sinks, g_mix, w_in, conv_w, bias, g_attn, g_conv, w_out, g_ffn, wrt, br, seq_len):
    n_tok = x2d.shape[0]
    tm = TOKEN_TILE
    n_tiles = n_tok // tm
    assert n_tok // ROUTE_TILE == LANES, "one lane of the count table per routing tile"
    const = lambda shape: pl.BlockSpec(shape, lambda i, *_: (0,) * len(shape))
    grid_spec = pltpu.PrefetchScalarGridSpec(
        num_scalar_prefetch=0,
        grid=(n_tiles,),
        in_specs=[
            pl.BlockSpec(memory_space=pltpu.SMEM),
            pl.BlockSpec((tm, D_MODEL), lambda i: (i, 0)),
            const((1, D_MODEL)), const((D_MODEL, IN_COLS)), const((3, CONV_WIDTH)),
            const((N_KV_HEADS, BLOCK, GROUP * BAND)),
            const((1, ATTN_WIDTH)), const((1, CONV_WIDTH)), const((D_MODEL, D_MODEL)),
            const((1, D_MODEL)), const((N_EXPERTS, D_MODEL)), const((N_EXPERTS, 1)),
        ],
        out_specs=[
            pl.BlockSpec((tm, D_MODEL), lambda i: (i, 0)),
            pl.BlockSpec((tm, D_MODEL), lambda i: (i, 0)),
            pl.BlockSpec((TOP_K, tm), lambda i: (0, i)),
            pl.BlockSpec((TOP_K, tm), lambda i: (0, i)),
            pl.BlockSpec((N_EXPERTS, LANES), lambda i: (0, 0)),
        ],
        scratch_shapes=[
            pltpu.VMEM((tm, IN_COLS), jnp.float32),
            pltpu.VMEM((BLOCK + tm, 4 * LANES), jnp.bfloat16),
            pltpu.VMEM((SUBLANES + tm, CONV_WIDTH), jnp.float32),
            pltpu.VMEM((tm, ATTN_WIDTH), jnp.float32),
            pltpu.VMEM((ROUTE_TILE, ROUTE_TILE), jnp.bfloat16),
        ])
    out_shape = [
        jax.ShapeDtypeStruct((n_tok, D_MODEL), jnp.float32),
        jax.ShapeDtypeStruct((n_tok, D_MODEL), jnp.bfloat16),
        jax.ShapeDtypeStruct((TOP_K, n_tok), jnp.int32),
        jax.ShapeDtypeStruct((TOP_K, n_tok), jnp.float32),
        jax.ShapeDtypeStruct((N_EXPERTS, LANES), jnp.int32),
    ]
    return pl.pallas_call(
        functools.partial(_mixer_kernel, seq_len // tm),
        grid_spec=grid_spec, out_shape=out_shape, name="mixer_router",
        compiler_params=pltpu.CompilerParams(dimension_semantics=("arbitrary",),
                                             vmem_limit_bytes=VMEM_LIMIT_BYTES),
    )(sinks, x2d, g_mix, w_in, conv_w, bias, g_attn, g_conv, w_out, g_ffn, wrt, br)


def _excl_cumsum(a, axis):
    idx = lax.broadcasted_iota(jnp.int32, a.shape, axis)
    inc = a
    step = 1
    while step < a.shape[axis]:
        inc = inc + jnp.where(idx >= step, pltpu.roll(inc, step, axis), 0)
        step *= 2
    return inc - a


def _tables_kernel(cnt_ref, size_ref, goff_ref, gdst_ref, blk_ref, used_ref):
    size = ((cnt_ref[...] + (SUBLANES - 1)) // SUBLANES) * SUBLANES
    size_ref[...] = size
    goff_ref[...] = _excl_cumsum(size, 0)
    total = jnp.broadcast_to(jnp.sum(size, axis=1, keepdims=True), size.shape)
    region = ((total + (ROW_BLOCK - 1)) // ROW_BLOCK) * ROW_BLOCK
    pstart = _excl_cumsum(region, 0)
    gdst_ref[...] = pstart + _excl_cumsum(size, 1)
    pend = (pstart + region)[:, 0:1]
    start = lax.broadcasted_iota(jnp.int32, (N_EXPERTS, blk_ref.shape[1]), 1) * ROW_BLOCK
    owner = jnp.sum(jnp.where(pend <= start, 1, 0), axis=0, keepdims=True)
    blk_ref[...] = jnp.minimum(owner, N_EXPERTS - 1)
    used_ref[...] = jnp.broadcast_to(jnp.max(pend, axis=0, keepdims=True) // ROW_BLOCK, used_ref.shape)


def _tables_call(cnt, n_blocks):
    nb_pad = -(-n_blocks // LANES) * LANES
    tbl = jax.ShapeDtypeStruct((N_EXPERTS, LANES), jnp.int32)
    return pl.pallas_call(
        _tables_kernel,
        out_shape=[tbl, tbl, tbl, jax.ShapeDtypeStruct((1, nb_pad), jnp.int32),
                   jax.ShapeDtypeStruct((1, LANES), jnp.int32)],
        name="group_tables",
    )(cnt)


def _start_group_copies(tile, size_ref, goff_ref, gdst_ref, sem, make_copy):
    def body(e, rows):
        n = pl.multiple_of(size_ref[e * LANES + tile], SUBLANES)
        @pl.when(n > 0)
        def _():
            off = pl.multiple_of(goff_ref[e * LANES + tile], SUBLANES)
            dst = pl.multiple_of(gdst_ref[e * LANES + tile], SUBLANES)
            make_copy(off, dst, n, sem).start()
        return rows + n
    return lax.fori_loop(0, N_EXPERTS, body, jnp.int32(0))


def _wait_rows(rows_hbm, n, sem):
    @pl.when(n > 0)
    def _():
        m = pl.multiple_of(n, SUBLANES)
        half = rows_hbm.shape[0] // (2 * SUBLANES) * SUBLANES
        pltpu.make_async_copy(rows_hbm.at[pl.ds(0, m)], rows_hbm.at[pl.ds(half, m)], sem).wait()


def _dispatch_kernel(n_blocks,
                     size_ref, goff_ref, gdst_ref, used_ref,
                     h2_ref, pos_ref, rows_ref,
                     stage, zbuf, inflight, sems, zsem):
    t = pl.program_id(0)
    n_tiles = pl.num_programs(0)
    slot = t % 2

    @pl.when(t == 0)
    def _():
        zbuf[...] = jnp.zeros_like(zbuf)
        def zero_block(start):
            pltpu.make_async_copy(
                zbuf, rows_ref.at[pl.ds(pl.multiple_of(start, ROW_BLOCK), ROW_BLOCK)], zsem).start()
        def zfill(e, n):
            first = gdst_ref[e * LANES]
            last = gdst_ref[e * LANES + LANES - 1] + size_ref[e * LANES + LANES - 1]
            @pl.when(last > first)
            def _():
                zero_block((last + ROW_BLOCK - 1) // ROW_BLOCK * ROW_BLOCK - ROW_BLOCK)
            return n + jnp.where(last > first, ROW_BLOCK, 0)
        n_zero = lax.fori_loop(0, N_EXPERTS, zfill, jnp.int32(0))
        def ztail(b, _):
            zero_block(b * ROW_BLOCK)
            return 0
        lax.fori_loop(used_ref[0], n_blocks, ztail, 0)
        _wait_rows(rows_ref, n_zero + (n_blocks - used_ref[0]) * ROW_BLOCK, zsem)

    @pl.when(t >= 2)
    def _():
        _wait_rows(rows_ref, inflight[slot], sems.at[slot])

    pos = pos_ref[...]
    r = lax.broadcasted_iota(jnp.int32, (STAGE_ROWS, pos.shape[1]), 0)
    hit = jnp.logical_or(jnp.logical_or(r == pos[0:1], r == pos[1:2]),
                         jnp.logical_or(r == pos[2:3], r == pos[3:4]))
    perm = jnp.where(hit, 1.0, 0.0).astype(jnp.bfloat16)
    stage[slot] = jnp.dot(perm, h2_ref[...], preferred_element_type=jnp.float32)

    def make_copy(off, dst, n, sem):
        return pltpu.make_async_copy(stage.at[slot, pl.ds(off, n)], rows_ref.at[pl.ds(dst, n)], sem)
    inflight[slot] = _start_group_copies(t, size_ref, goff_ref, gdst_ref, sems.at[slot], make_copy)

    @pl.when(t == n_tiles - 1)
    def _():
        _wait_rows(rows_ref, inflight[1 - slot], sems.at[1 - slot])
        _wait_rows(rows_ref, inflight[slot], sems.at[slot])


def _dispatch_call(size, goff, gdst, used, h2, pos, n_rows):
    n_tok = h2.shape[0]
    ts = ROUTE_TILE
    n_blocks = n_rows // ROW_BLOCK
    grid_spec = pltpu.PrefetchScalarGridSpec(
        num_scalar_prefetch=4,
        grid=(n_tok // ts,),
        in_specs=[
            pl.BlockSpec((ts, D_MODEL), lambda t, *_: (t, 0)),
            pl.BlockSpec((TOP_K, ts), lambda t, *_: (0, t)),
        ],
        out_specs=pl.BlockSpec(memory_space=pl.ANY),
        scratch_shapes=[
            pltpu.VMEM((2, STAGE_ROWS, D_MODEL), jnp.float32),
            pltpu.VMEM((ROW_BLOCK, D_MODEL), jnp.float32),
            pltpu.SMEM((2,), jnp.int32),
            pltpu.SemaphoreType.DMA((2,)),
            pltpu.SemaphoreType.DMA(()),
        ])
    return pl.pallas_call(
        functools.partial(_dispatch_kernel, n_blocks),
        grid_spec=grid_spec,
        out_shape=jax.ShapeDtypeStruct((n_rows, D_MODEL), jnp.float32),
        name="dispatch_rows",
        compiler_params=pltpu.CompilerParams(dimension_semantics=("arbitrary",),
                                             vmem_limit_bytes=VMEM_LIMIT_BYTES),
    )(size, goff, gdst, used, h2, pos)


def _expert_kernel(blk_ref, used_ref, x_ref, w1_ref, b1_ref, w2_ref, b2_ref, y_ref, w1_bf, w2_bf, prev_e):
    i = pl.program_id(0)
    bf16 = jnp.bfloat16
    e = blk_ref[i]

    @pl.when(i == 0)
    def _():
        prev_e[0] = jnp.int32(-1)

    @pl.when(e != prev_e[0])
    def _():
        w1_bf[...] = w1_ref[0].astype(bf16)
        w2_bf[...] = w2_ref[0].astype(bf16)
        prev_e[0] = e

    @pl.when(i >= used_ref[0])
    def _():
        y_ref[...] = jnp.zeros_like(y_ref)

    @pl.when(i < used_ref[0])
    def _():
        hu = jnp.dot(x_ref[...].astype(bf16), w1_bf[...], preferred_element_type=jnp.float32) + b1_ref[0]
        glu = jnp.minimum(hu[:, :D_FF], SWIGLU_LIMIT)
        lin = jnp.clip(hu[:, D_FF:], -SWIGLU_LIMIT, SWIGLU_LIMIT)
        act = glu * (1.0 / (1.0 + jnp.exp(-SWIGLU_ALPHA * glu))) * (lin + 1.0)
        y_ref[...] = jnp.dot(act.astype(bf16), w2_bf[...], preferred_element_type=jnp.float32) + b2_ref[0]


def _expert_call(blk, used, rows, w1, b1, w2, b2):
    n_rows = rows.shape[0]
    n_blocks = n_rows // ROW_BLOCK
    last = lambda i, b, u: jnp.maximum(jnp.minimum(i, u[0] - 1), 0)
    grid_spec = pltpu.PrefetchScalarGridSpec(
        num_scalar_prefetch=2,
        grid=(n_blocks,),
        in_specs=[
            pl.BlockSpec((ROW_BLOCK, D_MODEL), lambda i, b, u: (last(i, b, u), 0)),
            pl.BlockSpec((1, D_MODEL, 2 * D_FF), lambda i, b, u: (b[i], 0, 0)),
            pl.BlockSpec((1, 1, 2 * D_FF), lambda i, b, u: (b[i], 0, 0)),
            pl.BlockSpec((1, D_FF, D_MODEL), lambda i, b, u: (b[i], 0, 0)),
            pl.BlockSpec((1, 1, D_MODEL), lambda i, b, u: (b[i], 0, 0)),
        ],
        out_specs=pl.BlockSpec((ROW_BLOCK, D_MODEL), lambda i, b, u: (i, 0)),
        scratch_shapes=[
            pltpu.VMEM((D_MODEL, 2 * D_FF), jnp.bfloat16),
            pltpu.VMEM((D_FF, D_MODEL), jnp.bfloat16),
            pltpu.SMEM((1,), jnp.int32),
        ])
    return pl.pallas_call(
        _expert_kernel,
        grid_spec=grid_spec,
        out_shape=jax.ShapeDtypeStruct((n_rows, D_MODEL), jnp.float32),
        name="expert_ffn",
        compiler_params=pltpu.CompilerParams(dimension_semantics=("arbitrary",),
                                             vmem_limit_bytes=VMEM_LIMIT_BYTES),
    )(blk, used, rows, w1, b1.reshape(N_EXPERTS, 1, 2 * D_FF), w2, b2.reshape(N_EXPERTS, 1, D_MODEL))


def _combine_kernel(size_ref, goff_ref, gdst_ref,
                    x1_ref, pos_ref, gate_ref, y_ref, gfin_ref, out_ref,
                    stage, inflight, sems):
    t = pl.program_id(0)
    n_tiles = pl.num_programs(0)
    slot = t % 2

    def fetch(tile, slot_):
        def make_copy(off, src, n, sem):
            return pltpu.make_async_copy(y_ref.at[pl.ds(src, n)], stage.at[slot_, pl.ds(off, n)], sem)
        inflight[slot_] = _start_group_copies(tile, size_ref, goff_ref, gdst_ref, sems.at[slot_], make_copy)

    @pl.when(t == 0)
    def _():
        stage[...] = jnp.zeros_like(stage)
        fetch(0, 0)

    @pl.when(t + 1 < n_tiles)
    def _():
        fetch(t + 1, 1 - slot)

    _wait_rows(y_ref, inflight[slot], sems.at[slot])

    pos = pos_ref[...]
    gate = gate_ref[...]
    r = lax.broadcasted_iota(jnp.int32, (STAGE_ROWS, pos.shape[1]), 0)
    w = jnp.where(r == pos[0:1], gate[0:1], 0.0)
    for k in range(1, TOP_K):
        w = w + jnp.where(r == pos[k:k + 1], gate[k:k + 1], 0.0)
    moe = lax.dot_general(w.astype(jnp.bfloat16), stage[slot].astype(jnp.bfloat16),
                          (((0,), (0,)), ((), ())), preferred_element_type=jnp.float32)
    out_ref[...] = _rms(x1_ref[...] + moe, gfin_ref[...])


def _combine_call(size, goff, gdst, x1, pos, gate, y_rows, g_final):
    n_tok = x1.shape[0]
    ts = ROUTE_TILE
    grid_spec = pltpu.PrefetchScalarGridSpec(
        num_scalar_prefetch=3,
        grid=(n_tok // ts,),
        in_specs=[
            pl.BlockSpec((ts, D_MODEL), lambda t, *_: (t, 0)),
            pl.BlockSpec((TOP_K, ts), lambda t, *_: (0, t)),
            pl.BlockSpec((TOP_K, ts), lambda t, *_: (0, t)),
            pl.BlockSpec(memory_space=pl.ANY),
            pl.BlockSpec((1, D_MODEL), lambda t, *_: (0, 0)),
        ],
        out_specs=pl.BlockSpec((ts, D_MODEL), lambda t, *_: (t, 0)),
        scratch_shapes=[
            pltpu.VMEM((2, STAGE_ROWS, D_MODEL), jnp.float32),
            pltpu.SMEM((2,), jnp.int32),
            pltpu.SemaphoreType.DMA((2,)),
        ])
    return pl.pallas_call(
        _combine_kernel,
        grid_spec=grid_spec,
        out_shape=jax.ShapeDtypeStruct((n_tok, D_MODEL), jnp.float32),
        name="combine_norm",
        compiler_params=pltpu.CompilerParams(dimension_semantics=("arbitrary",),
                                             vmem_limit_bytes=VMEM_LIMIT_BYTES),
    )(size, goff, gdst, x1, pos, gate, y_rows, g_final)


def _attention_bias():
    slopes = jnp.exp2(-8.0 * jnp.arange(1, N_Q_HEADS + 1, dtype=jnp.float32) / N_Q_HEADS)
    qi = jnp.arange(BLOCK)[:, None]
    kj = jnp.arange(BAND)[None, :]
    dist = qi + BLOCK - kj
    valid = (dist >= 0) & (dist < WINDOW)
    b = -slopes[:, None, None] * dist.astype(jnp.float32)[None]
    b = jnp.where(valid[None], b, -jnp.inf)
    b = b.reshape(N_KV_HEADS, GROUP, BLOCK, BAND).transpose(0, 2, 1, 3)
    return b.reshape(N_KV_HEADS, BLOCK, GROUP * BAND)


def kernel(x, g_mix, w_in, conv_w, sinks, g_attn_out, g_conv_out, w_out, g_ffn, w_router, b_router,
           w1, b1, w2, b2, g_final):
    bsz, s_len, d = x.shape
    n_tok = bsz * s_len
    n_tiles = n_tok // ROUTE_TILE
    n_rows = n_tok * TOP_K + n_tiles * N_EXPERTS * SUBLANES + N_EXPERTS * ROW_BLOCK
    n_blocks = n_rows // ROW_BLOCK
    x2d = x.reshape(n_tok, d)
    bias = _attention_bias()
    l = 0
    x1, h2, pos, gate, cnt = _mixer_call(
        x2d, sinks[l], g_mix[l].reshape(1, d), w_in[l].astype(jnp.bfloat16), conv_w[l], bias,
        g_attn_out[l].reshape(1, ATTN_WIDTH), g_conv_out[l].reshape(1, CONV_WIDTH),
        w_out[l].astype(jnp.bfloat16), g_ffn[l].reshape(1, d), w_router[l].T,
        b_router[l].reshape(N_EXPERTS, 1), s_len)
    size, goff, gdst, blk, used = _tables_call(cnt, n_blocks)
    size, goff, gdst = size.reshape(-1), goff.reshape(-1), gdst.reshape(-1)
    blk, used = blk.reshape(-1), used.reshape(-1)[:1]
    rows = _dispatch_call(size, goff, gdst, used, h2, pos, n_rows)
    y_rows = _expert_call(blk, used, rows, w1[l], b1[l], w2[l], b2[l])
    out = _combine_call(size, goff, gdst, x1, pos, gate, y_rows, g_final.reshape(1, d))
    return out.reshape(bsz, s_len, d)
```

```python
import functools
import math

import jax
import jax.numpy as jnp
from jax import lax
from jax.experimental import pallas as pl
from jax.experimental.pallas import tpu as pltpu

D_MODEL = 1024
N_Q_HEADS = 8
N_KV_HEADS = 2
HEAD_DIM = 64
GROUP = N_Q_HEADS // N_KV_HEADS
ATTN_WIDTH = N_Q_HEADS * HEAD_DIM
KV_WIDTH = N_KV_HEADS * HEAD_DIM
CONV_WIDTH = D_MODEL - ATTN_WIDTH
QKV_COLS = ATTN_WIDTH + 2 * KV_WIDTH
IN_COLS = QKV_COLS + 3 * CONV_WIDTH
WINDOW = 128
BLOCK = 128
BAND = 2 * BLOCK
N_EXPERTS = 32
TOP_K = 4
D_FF = D_MODEL
SWIGLU_ALPHA = 1.702
SWIGLU_LIMIT = 7.0
EPS = 1e-5

LANES = 128
SUBLANES = 8
VMEM_LIMIT_BYTES = 56 * 1024 * 1024

TOKEN_TILE = 512
ROUTE_TILE = 256
ROW_BLOCK = 512
STAGE_ROWS = -(-(TOP_K * ROUTE_TILE + N_EXPERTS * (SUBLANES - 1)) // LANES) * LANES


def _rms(x, g):
    return x * lax.rsqrt(jnp.mean(x * x, axis=-1, keepdims=True) + EPS) * g


def _excl_cumsum(a, axis):
    idx = lax.broadcasted_iota(jnp.int32, a.shape, axis)
    inc = a
    step = 1
    while step < a.shape[axis]:
        inc = inc + jnp.where(idx >= step, pltpu.roll(inc, step, axis), 0)
        step *= 2
    return inc - a


def _mixer_kernel(tiles_per_seq,
                  sink_ref,
                  x_ref, gmix_ref, win_ref, convw_ref, bias_ref, gattn_ref, gconv_ref, wout_ref,
                  gffn_ref, wr_ref, br_ref,
                  x1_ref, h2_ref, pos_ref, gate_ref, cnt_ref,
                  p_scr, k_scr, vt_scr, u_scr, attn_t_scr, tri_scr):
    tm = x_ref.shape[0]
    ts = ROUTE_TILE
    i = pl.program_id(0)
    first = (i % tiles_per_seq) == 0
    f32, bf16 = jnp.float32, jnp.bfloat16

    @pl.when(i == 0)
    def _():
        cnt_ref[...] = jnp.zeros_like(cnt_ref)
        r = lax.broadcasted_iota(jnp.int32, (ts, ts), 0)
        c = lax.broadcasted_iota(jnp.int32, (ts, ts), 1)
        tri_scr[...] = jnp.where(r < c, 1.0, 0.0).astype(bf16)

    @pl.when(first)
    def _():
        k_scr[0:BLOCK, :] = jnp.zeros((BLOCK, k_scr.shape[1]), bf16)
        vt_scr[:, 0:BLOCK] = jnp.zeros((KV_WIDTH, BLOCK), bf16)
        u_scr[0:SUBLANES, :] = jnp.zeros((SUBLANES, CONV_WIDTH), f32)

    x = x_ref[...]
    h = _rms(x, gmix_ref[...])
    p_scr[...] = jnp.dot(h.astype(bf16), win_ref[...], preferred_element_type=f32)

    kk = p_scr[:, ATTN_WIDTH:ATTN_WIDTH + KV_WIDTH]
    kr = pltpu.roll(kk, HEAD_DIM, axis=1)
    k_scr[BLOCK:BLOCK + tm, :] = jnp.concatenate([kk, kr], axis=1).astype(bf16)
    vt_scr[:, BLOCK:BLOCK + tm] = jnp.transpose(p_scr[:, ATTN_WIDTH + KV_WIDTH:QKV_COLS]).astype(bf16)

    lane = lax.broadcasted_iota(jnp.int32, (BAND, LANES), 1)
    lo = lane < HEAD_DIM
    qgrp = lax.broadcasted_iota(jnp.int32, (BLOCK, 2 * LANES), 1) // HEAD_DIM
    key_row = lax.broadcasted_iota(jnp.int32, (BAND, GROUP * BLOCK), 0)
    scale = 1.0 / math.sqrt(HEAD_DIM)
    zero_q = jnp.zeros((BLOCK, 2 * LANES), bf16)
    nt = (((1,), (1,)), ((), ()))

    for j in range(tm // BLOCK):
        rows = slice(j * BLOCK, (j + 1) * BLOCK)
        kband = k_scr[j * BLOCK:j * BLOCK + BAND, :]
        bk, bkr = kband[:, 0:LANES], kband[:, LANES:2 * LANES]
        vt_band = vt_scr[:, j * BLOCK:j * BLOCK + BAND]
        for hh in range(N_KV_HEADS):
            k_both = jnp.where(lo, bk, bkr) if hh == 0 else jnp.where(lo, bkr, bk)
            kmat = jnp.concatenate([k_both, k_both], axis=1)
            qh = (p_scr[rows, hh * 2 * LANES:(hh + 1) * 2 * LANES] * scale).astype(bf16)
            qm = jnp.concatenate([jnp.where(qgrp == g, qh, zero_q) for g in range(GROUP)], axis=0)
            s = lax.dot_general(kmat, qm, nt, preferred_element_type=f32) + bias_ref[hh]
            if j == 0:
                s = jnp.where(jnp.logical_and(first, key_row < BLOCK), -jnp.inf, s)
            sink = jnp.concatenate(
                [jnp.full((1, BLOCK), sink_ref[hh * GROUP + g], f32) for g in range(GROUP)], axis=1)
            m = jnp.maximum(jnp.max(s, axis=0, keepdims=True), sink)
            e = jnp.exp(s - m)
            inv = 1.0 / (jnp.sum(e, axis=0, keepdims=True) + jnp.exp(sink - m))
            o_t = jnp.dot(vt_band, e.astype(bf16), preferred_element_type=f32)
            o_t = o_t[hh * HEAD_DIM:(hh + 1) * HEAD_DIM, :] * inv
            for g in range(GROUP):
                head = hh * GROUP + g
                attn_t_scr[head * HEAD_DIM:(head + 1) * HEAD_DIM, rows] = o_t[:, g * BLOCK:(g + 1) * BLOCK]

    bg = p_scr[:, QKV_COLS:QKV_COLS + CONV_WIDTH]
    cg = p_scr[:, QKV_COLS + CONV_WIDTH:QKV_COLS + 2 * CONV_WIDTH]
    xin = p_scr[:, QKV_COLS + 2 * CONV_WIDTH:IN_COLS]
    u = cg * xin
    u_scr[SUBLANES:SUBLANES + tm, :] = u
    u1 = u_scr[SUBLANES - 1:SUBLANES - 1 + tm, :]
    u2 = u_scr[SUBLANES - 2:SUBLANES - 2 + tm, :]
    y = convw_ref[0:1, :] * u2 + convw_ref[1:2, :] * u1 + convw_ref[2:3, :] * u
    conv = bg * y
    k_scr[0:BLOCK, :] = k_scr[tm:tm + BLOCK, :]
    vt_scr[:, 0:BLOCK] = vt_scr[:, tm:tm + BLOCK]
    u_scr[0:SUBLANES, :] = u_scr[tm:tm + SUBLANES, :]

    mix = jnp.concatenate([_rms(jnp.transpose(attn_t_scr[...]), gattn_ref[...]), _rms(conv, gconv_ref[...])], axis=1)
    x1 = x + jnp.dot(mix.astype(bf16), wout_ref[...], preferred_element_type=f32)
    x1_ref[...] = x1
    h2 = _rms(x1, gffn_ref[...])
    h2_hi = h2.astype(bf16)
    h2_ref[...] = h2_hi

    h2_lo = (h2 - h2_hi.astype(f32)).astype(bf16)
    nt = (((1,), (1,)), ((), ()))
    by_hi = lax.dot_general(wr_ref[...], h2_hi, nt, preferred_element_type=f32)
    by_lo = lax.dot_general(wr_ref[0:N_EXPERTS, :], h2_lo, nt, preferred_element_type=f32)
    logits = by_hi[0:N_EXPERTS] + by_hi[N_EXPERTS:] + by_lo + br_ref[...]
    row = lax.broadcasted_iota(jnp.int32, (N_EXPERTS, tm), 0)
    vals, idxs = [], []
    l = logits
    for _ in range(TOP_K):
        m = jnp.max(l, axis=0, keepdims=True)
        idx = jnp.min(jnp.where(l == m, row, N_EXPERTS), axis=0, keepdims=True)
        vals.append(m)
        idxs.append(idx)
        l = jnp.where(row == idx, -jnp.inf, l)
    es = [jnp.exp(v - vals[0]) for v in vals]
    inv = 1.0 / (es[0] + es[1] + es[2] + es[3])
    gate_ref[...] = jnp.concatenate([e * inv for e in es], axis=0)

    onehots = [row == idx for idx in idxs]
    sel = jnp.logical_or(jnp.logical_or(onehots[0], onehots[1]), jnp.logical_or(onehots[2], onehots[3]))
    oh = jnp.where(sel, 1.0, 0.0)

    cnt_lane = lax.broadcasted_iota(jnp.int32, cnt_ref.shape, 1)
    poss = []
    for sub in range(tm // ts):
        sl = slice(sub * ts, (sub + 1) * ts)
        oh_s = oh[:, sl]
        cum = jnp.dot(oh_s.astype(bf16), tri_scr[...], preferred_element_type=f32)
        cnt = jnp.sum(oh_s, axis=1, keepdims=True)
        padded = SUBLANES * jnp.floor((cnt + (SUBLANES - 1.0)) * (1.0 / SUBLANES))
        goff = _excl_cumsum(jnp.broadcast_to(padded, (N_EXPERTS, LANES)), 0)[:, 0:1]
        pos = goff + cum
        poss.append(jnp.concatenate(
            [jnp.sum(jnp.where(o[:, sl], pos, 0.0), axis=0, keepdims=True) for o in onehots], axis=0))
        tile_id = i * (tm // ts) + sub
        cnt_ref[...] = jnp.where(cnt_lane == tile_id, cnt.astype(jnp.int32), cnt_ref[...])
    pos_ref[...] = jnp.concatenate(poss, axis=1).astype(jnp.int32)


def _mixer_call(x2d, sinks, g_mix, w_in, conv_w, bias, g_attn, g_conv, w_out, g_ffn, wr_split, br, seq_len):
    n_tok = x2d.shape[0]
    tm = TOKEN_TILE
    n_tiles = n_tok // tm
    assert n_tok // ROUTE_TILE == LANES, "one lane of the count table per routing tile"
    const = lambda shape: pl.BlockSpec(shape, lambda i, *_: (0,) * len(shape))
    grid_spec = pltpu.PrefetchScalarGridSpec(
        num_scalar_prefetch=0,
        grid=(n_tiles,),
        in_specs=[
            pl.BlockSpec(memory_space=pltpu.SMEM),
            pl.BlockSpec((tm, D_MODEL), lambda i: (i, 0)),
            const((1, D_MODEL)), const((D_MODEL, IN_COLS)), const((3, CONV_WIDTH)),
            const((N_KV_HEADS, BAND, GROUP * BLOCK)),
            const((1, ATTN_WIDTH)), const((1, CONV_WIDTH)), const((D_MODEL, D_MODEL)),
            const((1, D_MODEL)), const((2 * N_EXPERTS, D_MODEL)), const((N_EXPERTS, 1)),
        ],
        out_specs=[
            pl.BlockSpec((tm, D_MODEL), lambda i: (i, 0)),
            pl.BlockSpec((tm, D_MODEL), lambda i: (i, 0)),
            pl.BlockSpec((TOP_K, tm), lambda i: (0, i)),
            pl.BlockSpec((TOP_K, tm), lambda i: (0, i)),
            pl.BlockSpec((N_EXPERTS, LANES), lambda i: (0, 0)),
        ],
        scratch_shapes=[
            pltpu.VMEM((tm, IN_COLS), jnp.float32),
            pltpu.VMEM((BLOCK + tm, 2 * LANES), jnp.bfloat16),
            pltpu.VMEM((KV_WIDTH, BLOCK + tm), jnp.bfloat16),
            pltpu.VMEM((SUBLANES + tm, CONV_WIDTH), jnp.float32),
            pltpu.VMEM((ATTN_WIDTH, tm), jnp.float32),
            pltpu.VMEM((ROUTE_TILE, ROUTE_TILE), jnp.bfloat16),
        ])
    out_shape = [
        jax.ShapeDtypeStruct((n_tok, D_MODEL), jnp.float32),
        jax.ShapeDtypeStruct((n_tok, D_MODEL), jnp.bfloat16),
        jax.ShapeDtypeStruct((TOP_K, n_tok), jnp.int32),
        jax.ShapeDtypeStruct((TOP_K, n_tok), jnp.float32),
        jax.ShapeDtypeStruct((N_EXPERTS, LANES), jnp.int32),
    ]
    return pl.pallas_call(
        functools.partial(_mixer_kernel, seq_len // tm),
        grid_spec=grid_spec, out_shape=out_shape, name="mixer_router",
        compiler_params=pltpu.CompilerParams(dimension_semantics=("arbitrary",),
                                             vmem_limit_bytes=VMEM_LIMIT_BYTES),
    )(sinks, x2d, g_mix, w_in, conv_w, bias, g_attn, g_conv, w_out, g_ffn, wr_split, br)


def _tables_kernel(cnt_ref, size_ref, goff_ref, gdst_ref, blk_ref, used_ref):
    size = ((cnt_ref[...] + (SUBLANES - 1)) // SUBLANES) * SUBLANES
    size_ref[...] = size
    goff_ref[...] = _excl_cumsum(size, 0)
    total = jnp.broadcast_to(jnp.sum(size, axis=1, keepdims=True), size.shape)
    region = ((total + (ROW_BLOCK - 1)) // ROW_BLOCK) * ROW_BLOCK
    pstart = _excl_cumsum(region, 0)
    gdst_ref[...] = pstart + _excl_cumsum(size, 1)
    pend = (pstart + region)[:, 0:1]
    start = lax.broadcasted_iota(jnp.int32, (N_EXPERTS, blk_ref.shape[1]), 1) * ROW_BLOCK
    owner = jnp.sum(jnp.where(pend <= start, 1, 0), axis=0, keepdims=True)
    blk_ref[...] = jnp.minimum(owner, N_EXPERTS - 1)
    used_ref[...] = jnp.broadcast_to(jnp.max(pend, axis=0, keepdims=True) // ROW_BLOCK, used_ref.shape)


def _tables_call(cnt, n_blocks):
    nb_pad = -(-n_blocks // LANES) * LANES
    tbl = jax.ShapeDtypeStruct((N_EXPERTS, LANES), jnp.int32)
    return pl.pallas_call(
        _tables_kernel,
        out_shape=[tbl, tbl, tbl, jax.ShapeDtypeStruct((1, nb_pad), jnp.int32),
                   jax.ShapeDtypeStruct((1, LANES), jnp.int32)],
        name="group_tables",
    )(cnt)


def _start_group_copies(tile, size_ref, goff_ref, gdst_ref, sem, make_copy):
    def body(e, rows):
        n = pl.multiple_of(size_ref[e * LANES + tile], SUBLANES)
        @pl.when(n > 0)
        def _():
            off = pl.multiple_of(goff_ref[e * LANES + tile], SUBLANES)
            dst = pl.multiple_of(gdst_ref[e * LANES + tile], SUBLANES)
            make_copy(off, dst, n, sem).start()
        return rows + n
    return lax.fori_loop(0, N_EXPERTS, body, jnp.int32(0))


def _wait_rows(rows_hbm, n, sem):
    @pl.when(n > 0)
    def _():
        m = pl.multiple_of(n, SUBLANES)
        half = rows_hbm.shape[0] // (2 * SUBLANES) * SUBLANES
        pltpu.make_async_copy(rows_hbm.at[pl.ds(0, m)], rows_hbm.at[pl.ds(half, m)], sem).wait()


def _dispatch_kernel(n_blocks,
                     size_ref, goff_ref, gdst_ref, used_ref,
                     h2_ref, pos_ref, rows_ref,
                     stage, zbuf, inflight, sems, zsem):
    t = pl.program_id(0)
    n_tiles = pl.num_programs(0)
    slot = t % 2

    @pl.when(t == 0)
    def _():
        zbuf[...] = jnp.zeros_like(zbuf)
        def zero_block(start):
            pltpu.make_async_copy(
                zbuf, rows_ref.at[pl.ds(pl.multiple_of(start, ROW_BLOCK), ROW_BLOCK)], zsem).start()
        def zfill(e, n):
            first = gdst_ref[e * LANES]
            last = gdst_ref[e * LANES + LANES - 1] + size_ref[e * LANES + LANES - 1]
            @pl.when(last > first)
            def _():
                zero_block((last + ROW_BLOCK - 1) // ROW_BLOCK * ROW_BLOCK - ROW_BLOCK)
            return n + jnp.where(last > first, ROW_BLOCK, 0)
        n_zero = lax.fori_loop(0, N_EXPERTS, zfill, jnp.int32(0))
        def ztail(b, _):
            zero_block(b * ROW_BLOCK)
            return 0
        lax.fori_loop(used_ref[0], n_blocks, ztail, 0)
        _wait_rows(rows_ref, n_zero + (n_blocks - used_ref[0]) * ROW_BLOCK, zsem)

    @pl.when(t >= 2)
    def _():
        _wait_rows(rows_ref, inflight[slot], sems.at[slot])

    pos = pos_ref[...]
    r = lax.broadcasted_iota(jnp.int32, (STAGE_ROWS, pos.shape[1]), 0)
    hit = jnp.logical_or(jnp.logical_or(r == pos[0:1], r == pos[1:2]),
                         jnp.logical_or(r == pos[2:3], r == pos[3:4]))
    perm = jnp.where(hit, 1.0, 0.0).astype(jnp.bfloat16)
    stage[slot] = jnp.dot(perm, h2_ref[...], preferred_element_type=jnp.float32)

    def make_copy(off, dst, n, sem):
        return pltpu.make_async_copy(stage.at[slot, pl.ds(off, n)], rows_ref.at[pl.ds(dst, n)], sem)
    inflight[slot] = _start_group_copies(t, size_ref, goff_ref, gdst_ref, sems.at[slot], make_copy)

    @pl.when(t == n_tiles - 1)
    def _():
        _wait_rows(rows_ref, inflight[1 - slot], sems.at[1 - slot])
        _wait_rows(rows_ref, inflight[slot], sems.at[slot])


def _dispatch_call(size, goff, gdst, used, h2, pos, n_rows):
    n_tok = h2.shape[0]
    ts = ROUTE_TILE
    n_blocks = n_rows // ROW_BLOCK
    grid_spec = pltpu.PrefetchScalarGridSpec(
        num_scalar_prefetch=4,
        grid=(n_tok // ts,),
        in_specs=[
            pl.BlockSpec((ts, D_MODEL), lambda t, *_: (t, 0)),
            pl.BlockSpec((TOP_K, ts), lambda t, *_: (0, t)),
        ],
        out_specs=pl.BlockSpec(memory_space=pl.ANY),
        scratch_shapes=[
            pltpu.VMEM((2, STAGE_ROWS, D_MODEL), jnp.float32),
            pltpu.VMEM((ROW_BLOCK, D_MODEL), jnp.float32),
            pltpu.SMEM((2,), jnp.int32),
            pltpu.SemaphoreType.DMA((2,)),
            pltpu.SemaphoreType.DMA(()),
        ])
    return pl.pallas_call(
        functools.partial(_dispatch_kernel, n_blocks),
        grid_spec=grid_spec,
        out_shape=jax.ShapeDtypeStruct((n_rows, D_MODEL), jnp.float32),
        name="dispatch_rows",
        compiler_params=pltpu.CompilerParams(dimension_semantics=("arbitrary",),
                                             vmem_limit_bytes=VMEM_LIMIT_BYTES),
    )(size, goff, gdst, used, h2, pos)


def _expert_kernel(blk_ref, used_ref, x_ref, w1_ref, b1_ref, w2_ref, b2_ref, y_ref, w1_bf, w2_bf, prev_e):
    i = pl.program_id(0)
    bf16 = jnp.bfloat16
    e = blk_ref[i]

    @pl.when(i == 0)
    def _():
        prev_e[0] = jnp.int32(-1)

    @pl.when(e != prev_e[0])
    def _():
        w1_bf[...] = w1_ref[0].astype(bf16)
        w2_bf[...] = w2_ref[0].astype(bf16)
        prev_e[0] = e

    @pl.when(i >= used_ref[0])
    def _():
        y_ref[...] = jnp.zeros_like(y_ref)

    @pl.when(i < used_ref[0])
    def _():
        hu = jnp.dot(x_ref[...].astype(bf16), w1_bf[...], preferred_element_type=jnp.float32) + b1_ref[0]
        glu = jnp.minimum(hu[:, :D_FF], SWIGLU_LIMIT)
        lin = jnp.clip(hu[:, D_FF:], -SWIGLU_LIMIT, SWIGLU_LIMIT)
        act = glu * (1.0 / (1.0 + jnp.exp(-SWIGLU_ALPHA * glu))) * (lin + 1.0)
        y_ref[...] = jnp.dot(act.astype(bf16), w2_bf[...], preferred_element_type=jnp.float32) + b2_ref[0]


def _expert_call(blk, used, rows, w1, b1, w2, b2):
    n_rows = rows.shape[0]
    n_blocks = n_rows // ROW_BLOCK
    last = lambda i, b, u: jnp.maximum(jnp.minimum(i, u[0] - 1), 0)
    grid_spec = pltpu.PrefetchScalarGridSpec(
        num_scalar_prefetch=2,
        grid=(n_blocks,),
        in_specs=[
            pl.BlockSpec((ROW_BLOCK, D_MODEL), lambda i, b, u: (last(i, b, u), 0)),
            pl.BlockSpec((1, D_MODEL, 2 * D_FF), lambda i, b, u: (b[i], 0, 0)),
            pl.BlockSpec((1, 1, 2 * D_FF), lambda i, b, u: (b[i], 0, 0)),
            pl.BlockSpec((1, D_FF, D_MODEL), lambda i, b, u: (b[i], 0, 0)),
            pl.BlockSpec((1, 1, D_MODEL), lambda i, b, u: (b[i], 0, 0)),
        ],
        out_specs=pl.BlockSpec((ROW_BLOCK, D_MODEL), lambda i, b, u: (i, 0)),
        scratch_shapes=[
            pltpu.VMEM((D_MODEL, 2 * D_FF), jnp.bfloat16),
            pltpu.VMEM((D_FF, D_MODEL), jnp.bfloat16),
            pltpu.SMEM((1,), jnp.int32),
        ])
    return pl.pallas_call(
        _expert_kernel,
        grid_spec=grid_spec,
        out_shape=jax.ShapeDtypeStruct((n_rows, D_MODEL), jnp.float32),
        name="expert_ffn",
        compiler_params=pltpu.CompilerParams(dimension_semantics=("arbitrary",),
                                             vmem_limit_bytes=VMEM_LIMIT_BYTES),
    )(blk, used, rows, w1, b1.reshape(N_EXPERTS, 1, 2 * D_FF), w2, b2.reshape(N_EXPERTS, 1, D_MODEL))


def _combine_kernel(size_ref, goff_ref, gdst_ref,
                    x1_ref, pos_ref, gate_ref, y_ref, gfin_ref, out_ref,
                    stage, inflight, sems):
    t = pl.program_id(0)
    n_tiles = pl.num_programs(0)
    slot = t % 2

    def fetch(tile, slot_):
        def make_copy(off, src, n, sem):
            return pltpu.make_async_copy(y_ref.at[pl.ds(src, n)], stage.at[slot_, pl.ds(off, n)], sem)
        inflight[slot_] = _start_group_copies(tile, size_ref, goff_ref, gdst_ref, sems.at[slot_], make_copy)

    @pl.when(t == 0)
    def _():
        stage[...] = jnp.zeros_like(stage)
        fetch(0, 0)

    @pl.when(t + 1 < n_tiles)
    def _():
        fetch(t + 1, 1 - slot)

    _wait_rows(y_ref, inflight[slot], sems.at[slot])

    pos = pos_ref[...]
    gate = gate_ref[...]
    r = lax.broadcasted_iota(jnp.int32, (STAGE_ROWS, pos.shape[1]), 0)
    w = jnp.where(r == pos[0:1], gate[0:1], 0.0)
    for k in range(1, TOP_K):
        w = w + jnp.where(r == pos[k:k + 1], gate[k:k + 1], 0.0)
    moe = lax.dot_general(w.astype(jnp.bfloat16), stage[slot].astype(jnp.bfloat16),
                          (((0,), (0,)), ((), ())), preferred_element_type=jnp.float32)
    out_ref[...] = _rms(x1_ref[...] + moe, gfin_ref[...])


def _combine_call(size, goff, gdst, x1, pos, gate, y_rows, g_final):
    n_tok = x1.shape[0]
    ts = ROUTE_TILE
    grid_spec = pltpu.PrefetchScalarGridSpec(
        num_scalar_prefetch=3,
        grid=(n_tok // ts,),
        in_specs=[
            pl.BlockSpec((ts, D_MODEL), lambda t, *_: (t, 0)),
            pl.BlockSpec((TOP_K, ts), lambda t, *_: (0, t)),
            pl.BlockSpec((TOP_K, ts), lambda t, *_: (0, t)),
            pl.BlockSpec(memory_space=pl.ANY),
            pl.BlockSpec((1, D_MODEL), lambda t, *_: (0, 0)),
        ],
        out_specs=pl.BlockSpec((ts, D_MODEL), lambda t, *_: (t, 0)),
        scratch_shapes=[
            pltpu.VMEM((2, STAGE_ROWS, D_MODEL), jnp.float32),
            pltpu.SMEM((2,), jnp.int32),
            pltpu.SemaphoreType.DMA((2,)),
        ])
    return pl.pallas_call(
        _combine_kernel,
        grid_spec=grid_spec,
        out_shape=jax.ShapeDtypeStruct((n_tok, D_MODEL), jnp.float32),
        name="combine_norm",
        compiler_params=pltpu.CompilerParams(dimension_semantics=("arbitrary",),
                                             vmem_limit_bytes=VMEM_LIMIT_BYTES),
    )(size, goff, gdst, x1, pos, gate, y_rows, g_final)


def _attention_bias():
    slopes = jnp.exp2(-8.0 * jnp.arange(1, N_Q_HEADS + 1, dtype=jnp.float32) / N_Q_HEADS)
    qi = jnp.arange(BLOCK)[:, None]
    kj = jnp.arange(BAND)[None, :]
    dist = qi + BLOCK - kj
    valid = (dist >= 0) & (dist < WINDOW)
    b = -slopes[:, None, None] * dist.astype(jnp.float32)[None]
    b = jnp.where(valid[None], b, -jnp.inf)
    b = b.reshape(N_KV_HEADS, GROUP, BLOCK, BAND).transpose(0, 3, 1, 2)
    return b.reshape(N_KV_HEADS, BAND, GROUP * BLOCK)


def kernel(x, g_mix, w_in, conv_w, sinks, g_attn_out, g_conv_out, w_out, g_ffn, w_router, b_router,
           w1, b1, w2, b2, g_final):
    bsz, s_len, d = x.shape
    n_tok = bsz * s_len
    n_tiles = n_tok // ROUTE_TILE
    n_rows = n_tok * TOP_K + n_tiles * N_EXPERTS * SUBLANES + N_EXPERTS * ROW_BLOCK
    n_blocks = n_rows // ROW_BLOCK
    x2d = x.reshape(n_tok, d)
    bias = _attention_bias()
    l = 0
    wrt = w_router[l].T
    wr_hi = wrt.astype(jnp.bfloat16)
    wr_lo = (wrt - wr_hi.astype(jnp.float32)).astype(jnp.bfloat16)
    wr_split = jnp.concatenate([wr_hi, wr_lo], axis=0)
    x1, h2, pos, gate, cnt = _mixer_call(
        x2d, sinks[l], g_mix[l].reshape(1, d), w_in[l].astype(jnp.bfloat16), conv_w[l], bias,
        g_attn_out[l].reshape(1, ATTN_WIDTH), g_conv_out[l].reshape(1, CONV_WIDTH),
        w_out[l].astype(jnp.bfloat16), g_ffn[l].reshape(1, d), wr_split,
        b_router[l].reshape(N_EXPERTS, 1), s_len)
    size, goff, gdst, blk, used = _tables_call(cnt, n_blocks)
    size, goff, gdst = size.reshape(-1), goff.reshape(-1), gdst.reshape(-1)
    blk, used = blk.reshape(-1), used.reshape(-1)[:1]
    rows = _dispatch_call(size, goff, gdst, used, h2, pos, n_rows)
    y_rows = _expert_call(blk, used, rows, w1[l], b1[l], w2[l], b2[l])
    out = _combine_call(size, goff, gdst, x1, pos, gate, y_rows, g_final.reshape(1, d))
    return out.reshape(bsz, s_len, d)
```

```python
import functools
import math

import jax
import jax.numpy as jnp
from jax import lax
from jax.experimental import pallas as pl
from jax.experimental.pallas import tpu as pltpu

D_MODEL = 1024
N_Q_HEADS = 8
N_KV_HEADS = 2
HEAD_DIM = 64
GROUP = N_Q_HEADS // N_KV_HEADS
ATTN_WIDTH = N_Q_HEADS * HEAD_DIM
KV_WIDTH = N_KV_HEADS * HEAD_DIM
CONV_WIDTH = D_MODEL - ATTN_WIDTH
QKV_COLS = ATTN_WIDTH + 2 * KV_WIDTH
IN_COLS = QKV_COLS + 3 * CONV_WIDTH
WINDOW = 128
BLOCK = 128
BAND = 2 * BLOCK
N_EXPERTS = 32
TOP_K = 4
D_FF = D_MODEL
SWIGLU_ALPHA = 1.702
SWIGLU_LIMIT = 7.0
EPS = 1e-5

LANES = 128
SUBLANES = 8
VMEM_LIMIT_BYTES = 56 * 1024 * 1024

TOKEN_TILE = 512
ROUTE_TILE = 256
ROW_BLOCK = 512
STAGE_ROWS = -(-(TOP_K * ROUTE_TILE + N_EXPERTS * (SUBLANES - 1)) // LANES) * LANES
PACK_ROWS = 16
COMBINE_ROWS = -(-(TOP_K * ROUTE_TILE + N_EXPERTS * (SUBLANES - 1 + PACK_ROWS)) // LANES) * LANES


def _rms(x, g):
    return x * lax.rsqrt(jnp.mean(x * x, axis=-1, keepdims=True) + EPS) * g


def _excl_cumsum(a, axis):
    idx = lax.broadcasted_iota(jnp.int32, a.shape, axis)
    inc = a
    step = 1
    while step < a.shape[axis]:
        inc = inc + jnp.where(idx >= step, pltpu.roll(inc, step, axis), 0)
        step *= 2
    return inc - a


def _mixer_kernel(tiles_per_seq,
                  sink_ref,
                  x_ref, gmix_ref, win_ref, convw_ref, bias_ref, gattn_ref, gconv_ref, wout_ref,
                  gffn_ref, wr_ref, br_ref,
                  x1_ref, h2_ref, pos_ref, gate_ref, cnt_ref,
                  p_scr, k_scr, vt_scr, u_scr, attn_t_scr, tri_scr):
    tm = x_ref.shape[0]
    ts = ROUTE_TILE
    i = pl.program_id(0)
    first = (i % tiles_per_seq) == 0
    f32, bf16 = jnp.float32, jnp.bfloat16

    @pl.when(i == 0)
    def _():
        cnt_ref[...] = jnp.zeros_like(cnt_ref)
        r = lax.broadcasted_iota(jnp.int32, (ts, ts), 0)
        c = lax.broadcasted_iota(jnp.int32, (ts, ts), 1)
        tri_scr[...] = jnp.where(r < c, 1.0, 0.0).astype(bf16)

    @pl.when(first)
    def _():
        k_scr[0:BLOCK, :] = jnp.zeros((BLOCK, k_scr.shape[1]), bf16)
        vt_scr[:, 0:BLOCK] = jnp.zeros((KV_WIDTH, BLOCK), bf16)
        u_scr[0:SUBLANES, :] = jnp.zeros((SUBLANES, CONV_WIDTH), f32)

    x = x_ref[...]
    h = _rms(x, gmix_ref[...])
    p_scr[...] = jnp.dot(h.astype(bf16), win_ref[...], preferred_element_type=f32)

    kk = p_scr[:, ATTN_WIDTH:ATTN_WIDTH + KV_WIDTH]
    kr = pltpu.roll(kk, HEAD_DIM, axis=1)
    k_scr[BLOCK:BLOCK + tm, :] = jnp.concatenate([kk, kr], axis=1).astype(bf16)
    vt_scr[:, BLOCK:BLOCK + tm] = jnp.transpose(p_scr[:, ATTN_WIDTH + KV_WIDTH:QKV_COLS]).astype(bf16)

    lane = lax.broadcasted_iota(jnp.int32, (BAND, LANES), 1)
    lo = lane < HEAD_DIM
    qgrp = lax.broadcasted_iota(jnp.int32, (BLOCK, 2 * LANES), 1) // HEAD_DIM
    key_row = lax.broadcasted_iota(jnp.int32, (BAND, GROUP * BLOCK), 0)
    scale = 1.0 / math.sqrt(HEAD_DIM)
    zero_q = jnp.zeros((BLOCK, 2 * LANES), bf16)
    nt = (((1,), (1,)), ((), ()))

    for j in range(tm // BLOCK):
        rows = slice(j * BLOCK, (j + 1) * BLOCK)
        kband = k_scr[j * BLOCK:j * BLOCK + BAND, :]
        bk, bkr = kband[:, 0:LANES], kband[:, LANES:2 * LANES]
        vt_band = vt_scr[:, j * BLOCK:j * BLOCK + BAND]
        for hh in range(N_KV_HEADS):
            k_both = jnp.where(lo, bk, bkr) if hh == 0 else jnp.where(lo, bkr, bk)
            kmat = jnp.concatenate([k_both, k_both], axis=1)
            qh = (p_scr[rows, hh * 2 * LANES:(hh + 1) * 2 * LANES] * scale).astype(bf16)
            qm = jnp.concatenate([jnp.where(qgrp == g, qh, zero_q) for g in range(GROUP)], axis=0)
            s = lax.dot_general(kmat, qm, nt, preferred_element_type=f32) + bias_ref[hh]
            if j == 0:
                s = jnp.where(jnp.logical_and(first, key_row < BLOCK), -jnp.inf, s)
            sink = jnp.concatenate(
                [jnp.full((1, BLOCK), sink_ref[hh * GROUP + g], f32) for g in range(GROUP)], axis=1)
            m = jnp.maximum(jnp.max(s, axis=0, keepdims=True), sink)
            e = jnp.exp(s - m)
            inv = 1.0 / (jnp.sum(e, axis=0, keepdims=True) + jnp.exp(sink - m))
            o_t = jnp.dot(vt_band, e.astype(bf16), preferred_element_type=f32)
            o_t = o_t[hh * HEAD_DIM:(hh + 1) * HEAD_DIM, :] * inv
            for g in range(GROUP):
                head = hh * GROUP + g
                attn_t_scr[head * HEAD_DIM:(head + 1) * HEAD_DIM, rows] = o_t[:, g * BLOCK:(g + 1) * BLOCK]

    bg = p_scr[:, QKV_COLS:QKV_COLS + CONV_WIDTH]
    cg = p_scr[:, QKV_COLS + CONV_WIDTH:QKV_COLS + 2 * CONV_WIDTH]
    xin = p_scr[:, QKV_COLS + 2 * CONV_WIDTH:IN_COLS]
    u = cg * xin
    u_scr[SUBLANES:SUBLANES + tm, :] = u
    u1 = u_scr[SUBLANES - 1:SUBLANES - 1 + tm, :]
    u2 = u_scr[SUBLANES - 2:SUBLANES - 2 + tm, :]
    y = convw_ref[0:1, :] * u2 + convw_ref[1:2, :] * u1 + convw_ref[2:3, :] * u
    conv = bg * y
    k_scr[0:BLOCK, :] = k_scr[tm:tm + BLOCK, :]
    vt_scr[:, 0:BLOCK] = vt_scr[:, tm:tm + BLOCK]
    u_scr[0:SUBLANES, :] = u_scr[tm:tm + SUBLANES, :]

    mix = jnp.concatenate([_rms(jnp.transpose(attn_t_scr[...]), gattn_ref[...]), _rms(conv, gconv_ref[...])], axis=1)
    x1 = x + jnp.dot(mix.astype(bf16), wout_ref[...], preferred_element_type=f32)
    x1_ref[...] = x1
    h2 = _rms(x1, gffn_ref[...])
    h2_hi = h2.astype(bf16)
    h2_ref[...] = h2_hi

    h2_lo = (h2 - h2_hi.astype(f32)).astype(bf16)
    nt = (((1,), (1,)), ((), ()))
    by_hi = lax.dot_general(wr_ref[...], h2_hi, nt, preferred_element_type=f32)
    by_lo = lax.dot_general(wr_ref[0:N_EXPERTS, :], h2_lo, nt, preferred_element_type=f32)
    logits = by_hi[0:N_EXPERTS] + by_hi[N_EXPERTS:] + by_lo + br_ref[...]
    row = lax.broadcasted_iota(jnp.int32, (N_EXPERTS, tm), 0)
    vals, idxs = [], []
    l = logits
    for _ in range(TOP_K):
        m = jnp.max(l, axis=0, keepdims=True)
        idx = jnp.min(jnp.where(l == m, row, N_EXPERTS), axis=0, keepdims=True)
        vals.append(m)
        idxs.append(idx)
        l = jnp.where(row == idx, -jnp.inf, l)
    es = [jnp.exp(v - vals[0]) for v in vals]
    inv = 1.0 / (es[0] + es[1] + es[2] + es[3])
    gate_ref[...] = jnp.concatenate([e * inv for e in es], axis=0)

    onehots = [row == idx for idx in idxs]
    sel = jnp.logical_or(jnp.logical_or(onehots[0], onehots[1]), jnp.logical_or(onehots[2], onehots[3]))
    oh = jnp.where(sel, 1.0, 0.0)

    cnt_lane = lax.broadcasted_iota(jnp.int32, cnt_ref.shape, 1)
    poss = []
    for sub in range(tm // ts):
        sl = slice(sub * ts, (sub + 1) * ts)
        oh_s = oh[:, sl]
        cum = jnp.dot(oh_s.astype(bf16), tri_scr[...], preferred_element_type=f32)
        cnt = jnp.sum(oh_s, axis=1, keepdims=True)
        padded = SUBLANES * jnp.floor((cnt + (SUBLANES - 1.0)) * (1.0 / SUBLANES))
        goff = _excl_cumsum(jnp.broadcast_to(padded, (N_EXPERTS, LANES)), 0)[:, 0:1]
        pos = goff + cum
        poss.append(jnp.concatenate(
            [jnp.sum(jnp.where(o[:, sl], pos, 0.0), axis=0, keepdims=True) for o in onehots], axis=0))
        tile_id = i * (tm // ts) + sub
        cnt_ref[...] = jnp.where(cnt_lane == tile_id, cnt.astype(jnp.int32), cnt_ref[...])
    pos_ref[...] = jnp.concatenate(poss, axis=1).astype(jnp.int32)


def _mixer_call(x2d, sinks, g_mix, w_in, conv_w, bias, g_attn, g_conv, w_out, g_ffn, wr_split, br, seq_len):
    n_tok = x2d.shape[0]
    tm = TOKEN_TILE
    n_tiles = n_tok // tm
    assert n_tok // ROUTE_TILE == LANES, "one lane of the count table per routing tile"
    const = lambda shape: pl.BlockSpec(shape, lambda i, *_: (0,) * len(shape))
    grid_spec = pltpu.PrefetchScalarGridSpec(
        num_scalar_prefetch=0,
        grid=(n_tiles,),
        in_specs=[
            pl.BlockSpec(memory_space=pltpu.SMEM),
            pl.BlockSpec((tm, D_MODEL), lambda i: (i, 0)),
            const((1, D_MODEL)), const((D_MODEL, IN_COLS)), const((3, CONV_WIDTH)),
            const((N_KV_HEADS, BAND, GROUP * BLOCK)),
            const((1, ATTN_WIDTH)), const((1, CONV_WIDTH)), const((D_MODEL, D_MODEL)),
            const((1, D_MODEL)), const((2 * N_EXPERTS, D_MODEL)), const((N_EXPERTS, 1)),
        ],
        out_specs=[
            pl.BlockSpec((tm, D_MODEL), lambda i: (i, 0)),
            pl.BlockSpec((tm, D_MODEL), lambda i: (i, 0)),
            pl.BlockSpec((TOP_K, tm), lambda i: (0, i)),
            pl.BlockSpec((TOP_K, tm), lambda i: (0, i)),
            pl.BlockSpec((N_EXPERTS, LANES), lambda i: (0, 0)),
        ],
        scratch_shapes=[
            pltpu.VMEM((tm, IN_COLS), jnp.float32),
            pltpu.VMEM((BLOCK + tm, 2 * LANES), jnp.bfloat16),
            pltpu.VMEM((KV_WIDTH, BLOCK + tm), jnp.bfloat16),
            pltpu.VMEM((SUBLANES + tm, CONV_WIDTH), jnp.float32),
            pltpu.VMEM((ATTN_WIDTH, tm), jnp.float32),
            pltpu.VMEM((ROUTE_TILE, ROUTE_TILE), jnp.bfloat16),
        ])
    out_shape = [
        jax.ShapeDtypeStruct((n_tok, D_MODEL), jnp.float32),
        jax.ShapeDtypeStruct((n_tok, D_MODEL), jnp.bfloat16),
        jax.ShapeDtypeStruct((TOP_K, n_tok), jnp.int32),
        jax.ShapeDtypeStruct((TOP_K, n_tok), jnp.float32),
        jax.ShapeDtypeStruct((N_EXPERTS, LANES), jnp.int32),
    ]
    return pl.pallas_call(
        functools.partial(_mixer_kernel, seq_len // tm),
        grid_spec=grid_spec, out_shape=out_shape, name="mixer_router",
        compiler_params=pltpu.CompilerParams(dimension_semantics=("arbitrary",),
                                             vmem_limit_bytes=VMEM_LIMIT_BYTES),
    )(sinks, x2d, g_mix, w_in, conv_w, bias, g_attn, g_conv, w_out, g_ffn, wr_split, br)


def _tables_kernel(cnt_ref, size_ref, goff_ref, gdst_ref, blk_ref, used_ref,
                   span_ref, soff_ref, src_ref, delta_ref):
    size = ((cnt_ref[...] + (SUBLANES - 1)) // SUBLANES) * SUBLANES
    size_ref[...] = size
    goff = _excl_cumsum(size, 0)
    goff_ref[...] = goff
    total = jnp.broadcast_to(jnp.sum(size, axis=1, keepdims=True), size.shape)
    region = ((total + (ROW_BLOCK - 1)) // ROW_BLOCK) * ROW_BLOCK
    pstart = _excl_cumsum(region, 0)
    gdst = pstart + _excl_cumsum(size, 1)
    gdst_ref[...] = gdst
    src = (gdst // PACK_ROWS) * PACK_ROWS
    span = jnp.where(size > 0, ((gdst + size + (PACK_ROWS - 1)) // PACK_ROWS) * PACK_ROWS - src, 0)
    soff = _excl_cumsum(span, 0)
    span_ref[...] = span
    soff_ref[...] = soff
    src_ref[...] = src
    delta_ref[...] = soff + (gdst - src) - goff
    pend = (pstart + region)[:, 0:1]
    start = lax.broadcasted_iota(jnp.int32, (N_EXPERTS, blk_ref.shape[1]), 1) * ROW_BLOCK
    owner = jnp.sum(jnp.where(pend <= start, 1, 0), axis=0, keepdims=True)
    blk_ref[...] = jnp.minimum(owner, N_EXPERTS - 1)
    used_ref[...] = jnp.broadcast_to(jnp.max(pend, axis=0, keepdims=True) // ROW_BLOCK, used_ref.shape)


def _tables_call(cnt, n_blocks):
    nb_pad = -(-n_blocks // LANES) * LANES
    tbl = jax.ShapeDtypeStruct((N_EXPERTS, LANES), jnp.int32)
    return pl.pallas_call(
        _tables_kernel,
        out_shape=[tbl, tbl, tbl, jax.ShapeDtypeStruct((1, nb_pad), jnp.int32),
                   jax.ShapeDtypeStruct((1, LANES), jnp.int32), tbl, tbl, tbl, tbl],
        name="group_tables",
    )(cnt)


def _start_group_copies(tile, size_ref, goff_ref, gdst_ref, sem, make_copy, align):
    def body(e, rows):
        n = pl.multiple_of(size_ref[e * LANES + tile], align)
        @pl.when(n > 0)
        def _():
            off = pl.multiple_of(goff_ref[e * LANES + tile], align)
            dst = pl.multiple_of(gdst_ref[e * LANES + tile], align)
            make_copy(off, dst, n, sem).start()
        return rows + n
    return lax.fori_loop(0, N_EXPERTS, body, jnp.int32(0))


def _wait_rows(rows_hbm, n, sem, align):
    @pl.when(n > 0)
    def _():
        m = pl.multiple_of(n, align)
        half = rows_hbm.shape[0] // (2 * align) * align
        pltpu.make_async_copy(rows_hbm.at[pl.ds(0, m)], rows_hbm.at[pl.ds(half, m)], sem).wait()


def _dispatch_kernel(n_blocks,
                     size_ref, goff_ref, gdst_ref, used_ref,
                     h2_ref, pos_ref, rows_ref,
                     stage, zbuf, inflight, sems, zsem):
    t = pl.program_id(0)
    n_tiles = pl.num_programs(0)
    slot = t % 2

    @pl.when(t == 0)
    def _():
        zbuf[...] = jnp.zeros_like(zbuf)
        def zero_block(start):
            pltpu.make_async_copy(
                zbuf, rows_ref.at[pl.ds(pl.multiple_of(start, ROW_BLOCK), ROW_BLOCK)], zsem).start()
        def zfill(e, n):
            first = gdst_ref[e * LANES]
            last = gdst_ref[e * LANES + LANES - 1] + size_ref[e * LANES + LANES - 1]
            @pl.when(last > first)
            def _():
                zero_block((last + ROW_BLOCK - 1) // ROW_BLOCK * ROW_BLOCK - ROW_BLOCK)
            return n + jnp.where(last > first, ROW_BLOCK, 0)
        n_zero = lax.fori_loop(0, N_EXPERTS, zfill, jnp.int32(0))
        def ztail(b, _):
            zero_block(b * ROW_BLOCK)
            return 0
        lax.fori_loop(used_ref[0], n_blocks, ztail, 0)
        _wait_rows(rows_ref, n_zero + (n_blocks - used_ref[0]) * ROW_BLOCK, zsem, SUBLANES)

    @pl.when(t >= 2)
    def _():
        _wait_rows(rows_ref, inflight[slot], sems.at[slot], SUBLANES)

    pos = pos_ref[...]
    r = lax.broadcasted_iota(jnp.int32, (STAGE_ROWS, pos.shape[1]), 0)
    hit = jnp.logical_or(jnp.logical_or(r == pos[0:1], r == pos[1:2]),
                         jnp.logical_or(r == pos[2:3], r == pos[3:4]))
    perm = jnp.where(hit, 1.0, 0.0).astype(jnp.bfloat16)
    stage[slot] = jnp.dot(perm, h2_ref[...], preferred_element_type=jnp.float32)

    def make_copy(off, dst, n, sem):
        return pltpu.make_async_copy(stage.at[slot, pl.ds(off, n)], rows_ref.at[pl.ds(dst, n)], sem)
    inflight[slot] = _start_group_copies(t, size_ref, goff_ref, gdst_ref, sems.at[slot], make_copy, SUBLANES)

    @pl.when(t == n_tiles - 1)
    def _():
        _wait_rows(rows_ref, inflight[1 - slot], sems.at[1 - slot], SUBLANES)
        _wait_rows(rows_ref, inflight[slot], sems.at[slot], SUBLANES)


def _dispatch_call(size, goff, gdst, used, h2, pos, n_rows):
    n_tok = h2.shape[0]
    ts = ROUTE_TILE
    n_blocks = n_rows // ROW_BLOCK
    grid_spec = pltpu.PrefetchScalarGridSpec(
        num_scalar_prefetch=4,
        grid=(n_tok // ts,),
        in_specs=[
            pl.BlockSpec((ts, D_MODEL), lambda t, *_: (t, 0)),
            pl.BlockSpec((TOP_K, ts), lambda t, *_: (0, t)),
        ],
        out_specs=pl.BlockSpec(memory_space=pl.ANY),
        scratch_shapes=[
            pltpu.VMEM((2, STAGE_ROWS, D_MODEL), jnp.float32),
            pltpu.VMEM((ROW_BLOCK, D_MODEL), jnp.float32),
            pltpu.SMEM((2,), jnp.int32),
            pltpu.SemaphoreType.DMA((2,)),
            pltpu.SemaphoreType.DMA(()),
        ])
    return pl.pallas_call(
        functools.partial(_dispatch_kernel, n_blocks),
        grid_spec=grid_spec,
        out_shape=jax.ShapeDtypeStruct((n_rows, D_MODEL), jnp.float32),
        name="dispatch_rows",
        compiler_params=pltpu.CompilerParams(dimension_semantics=("arbitrary",),
                                             vmem_limit_bytes=VMEM_LIMIT_BYTES),
    )(size, goff, gdst, used, h2, pos)


def _expert_kernel(blk_ref, used_ref, x_ref, w1_ref, b1_ref, w2_ref, b2_ref, y_ref, w1_bf, w2_bf, prev_e):
    i = pl.program_id(0)
    bf16 = jnp.bfloat16
    e = blk_ref[i]

    @pl.when(i == 0)
    def _():
        prev_e[0] = jnp.int32(-1)

    @pl.when(e != prev_e[0])
    def _():
        w1_bf[...] = w1_ref[0].astype(bf16)
        w2_bf[...] = w2_ref[0].astype(bf16)
        prev_e[0] = e

    @pl.when(i >= used_ref[0])
    def _():
        y_ref[...] = jnp.zeros_like(y_ref)

    @pl.when(i < used_ref[0])
    def _():
        hu = jnp.dot(x_ref[...].astype(bf16), w1_bf[...], preferred_element_type=jnp.float32) + b1_ref[0]
        glu = jnp.minimum(hu[:, :D_FF], SWIGLU_LIMIT)
        lin = jnp.clip(hu[:, D_FF:], -SWIGLU_LIMIT, SWIGLU_LIMIT)
        act = glu * (1.0 / (1.0 + jnp.exp(-SWIGLU_ALPHA * glu))) * (lin + 1.0)
        y = jnp.dot(act.astype(bf16), w2_bf[...], preferred_element_type=jnp.float32) + b2_ref[0]
        y_ref[...] = y.astype(bf16)


def _expert_call(blk, used, rows, w1, b1, w2, b2):
    n_rows = rows.shape[0]
    n_blocks = n_rows // ROW_BLOCK
    last = lambda i, b, u: jnp.maximum(jnp.minimum(i, u[0] - 1), 0)
    grid_spec = pltpu.PrefetchScalarGridSpec(
        num_scalar_prefetch=2,
        grid=(n_blocks,),
        in_specs=[
            pl.BlockSpec((ROW_BLOCK, D_MODEL), lambda i, b, u: (last(i, b, u), 0)),
            pl.BlockSpec((1, D_MODEL, 2 * D_FF), lambda i, b, u: (b[i], 0, 0)),
            pl.BlockSpec((1, 1, 2 * D_FF), lambda i, b, u: (b[i], 0, 0)),
            pl.BlockSpec((1, D_FF, D_MODEL), lambda i, b, u: (b[i], 0, 0)),
            pl.BlockSpec((1, 1, D_MODEL), lambda i, b, u: (b[i], 0, 0)),
        ],
        out_specs=pl.BlockSpec((ROW_BLOCK, D_MODEL), lambda i, b, u: (i, 0)),
        scratch_shapes=[
            pltpu.VMEM((D_MODEL, 2 * D_FF), jnp.bfloat16),
            pltpu.VMEM((D_FF, D_MODEL), jnp.bfloat16),
            pltpu.SMEM((1,), jnp.int32),
        ])
    return pl.pallas_call(
        _expert_kernel,
        grid_spec=grid_spec,
        out_shape=jax.ShapeDtypeStruct((n_rows, D_MODEL), jnp.bfloat16),
        name="expert_ffn",
        compiler_params=pltpu.CompilerParams(dimension_semantics=("arbitrary",),
                                             vmem_limit_bytes=VMEM_LIMIT_BYTES),
    )(blk, used, rows, w1, b1.reshape(N_EXPERTS, 1, 2 * D_FF), w2, b2.reshape(N_EXPERTS, 1, D_MODEL))


def _combine_kernel(span_ref, soff_ref, src_ref, goff_ref, delta_ref,
                    x1_ref, pos_ref, gate_ref, y_ref, gfin_ref, out_ref,
                    stage, inflight, sems):
    t = pl.program_id(0)
    n_tiles = pl.num_programs(0)
    slot = t % 2

    def fetch(tile, slot_):
        def make_copy(off, src, n, sem):
            return pltpu.make_async_copy(y_ref.at[pl.ds(src, n)], stage.at[slot_, pl.ds(off, n)], sem)
        inflight[slot_] = _start_group_copies(tile, span_ref, soff_ref, src_ref, sems.at[slot_], make_copy,
                                              PACK_ROWS)

    @pl.when(t == 0)
    def _():
        stage[...] = jnp.zeros_like(stage)
        fetch(0, 0)

    @pl.when(t + 1 < n_tiles)
    def _():
        fetch(t + 1, 1 - slot)

    _wait_rows(y_ref, inflight[slot], sems.at[slot], PACK_ROWS)

    pos = pos_ref[...]
    shift = jnp.zeros_like(pos)
    prev = jnp.int32(0)
    for e in range(N_EXPERTS):
        d = delta_ref[e * LANES + t]
        shift = shift + jnp.where(pos >= goff_ref[e * LANES + t], d - prev, 0)
        prev = d
    pos = pos + shift
    gate = gate_ref[...]
    r = lax.broadcasted_iota(jnp.int32, (COMBINE_ROWS, pos.shape[1]), 0)
    w = jnp.where(r == pos[0:1], gate[0:1], 0.0)
    for k in range(1, TOP_K):
        w = w + jnp.where(r == pos[k:k + 1], gate[k:k + 1], 0.0)
    moe = lax.dot_general(w.astype(jnp.bfloat16), stage[slot],
                          (((0,), (0,)), ((), ())), preferred_element_type=jnp.float32)
    out_ref[...] = _rms(x1_ref[...] + moe, gfin_ref[...])


def _combine_call(span, soff, src, goff, delta, x1, pos, gate, y_rows, g_final):
    n_tok = x1.shape[0]
    ts = ROUTE_TILE
    grid_spec = pltpu.PrefetchScalarGridSpec(
        num_scalar_prefetch=5,
        grid=(n_tok // ts,),
        in_specs=[
            pl.BlockSpec((ts, D_MODEL), lambda t, *_: (t, 0)),
            pl.BlockSpec((TOP_K, ts), lambda t, *_: (0, t)),
            pl.BlockSpec((TOP_K, ts), lambda t, *_: (0, t)),
            pl.BlockSpec(memory_space=pl.ANY),
            pl.BlockSpec((1, D_MODEL), lambda t, *_: (0, 0)),
        ],
        out_specs=pl.BlockSpec((ts, D_MODEL), lambda t, *_: (t, 0)),
        scratch_shapes=[
            pltpu.VMEM((2, COMBINE_ROWS, D_MODEL), jnp.bfloat16),
            pltpu.SMEM((2,), jnp.int32),
            pltpu.SemaphoreType.DMA((2,)),
        ])
    return pl.pallas_call(
        _combine_kernel,
        grid_spec=grid_spec,
        out_shape=jax.ShapeDtypeStruct((n_tok, D_MODEL), jnp.float32),
        name="combine_norm",
        compiler_params=pltpu.CompilerParams(dimension_semantics=("arbitrary",),
                                             vmem_limit_bytes=VMEM_LIMIT_BYTES),
    )(span, soff, src, goff, delta, x1, pos, gate, y_rows, g_final)


def _attention_bias():
    slopes = jnp.exp2(-8.0 * jnp.arange(1, N_Q_HEADS + 1, dtype=jnp.float32) / N_Q_HEADS)
    qi = jnp.arange(BLOCK)[:, None]
    kj = jnp.arange(BAND)[None, :]
    dist = qi + BLOCK - kj
    valid = (dist >= 0) & (dist < WINDOW)
    b = -slopes[:, None, None] * dist.astype(jnp.float32)[None]
    b = jnp.where(valid[None], b, -jnp.inf)
    b = b.reshape(N_KV_HEADS, GROUP, BLOCK, BAND).transpose(0, 3, 1, 2)
    return b.reshape(N_KV_HEADS, BAND, GROUP * BLOCK)


def kernel(x, g_mix, w_in, conv_w, sinks, g_attn_out, g_conv_out, w_out, g_ffn, w_router, b_router,
           w1, b1, w2, b2, g_final):
    bsz, s_len, d = x.shape
    n_tok = bsz * s_len
    n_tiles = n_tok // ROUTE_TILE
    n_rows = n_tok * TOP_K + n_tiles * N_EXPERTS * SUBLANES + N_EXPERTS * ROW_BLOCK
    n_blocks = n_rows // ROW_BLOCK
    x2d = x.reshape(n_tok, d)
    bias = _attention_bias()
    l = 0
    wrt = w_router[l].T
    wr_hi = wrt.astype(jnp.bfloat16)
    wr_lo = (wrt - wr_hi.astype(jnp.float32)).astype(jnp.bfloat16)
    wr_split = jnp.concatenate([wr_hi, wr_lo], axis=0)
    x1, h2, pos, gate, cnt = _mixer_call(
        x2d, sinks[l], g_mix[l].reshape(1, d), w_in[l].astype(jnp.bfloat16), conv_w[l], bias,
        g_attn_out[l].reshape(1, ATTN_WIDTH), g_conv_out[l].reshape(1, CONV_WIDTH),
        w_out[l].astype(jnp.bfloat16), g_ffn[l].reshape(1, d), wr_split,
        b_router[l].reshape(N_EXPERTS, 1), s_len)
    size, goff, gdst, blk, used, span, soff, src, delta = _tables_call(cnt, n_blocks)
    size, goff, gdst = size.reshape(-1), goff.reshape(-1), gdst.reshape(-1)
    span, soff, src, delta = span.reshape(-1), soff.reshape(-1), src.reshape(-1), delta.reshape(-1)
    blk, used = blk.reshape(-1), used.reshape(-1)[:1]
    rows = _dispatch_call(size, goff, gdst, used, h2, pos, n_rows)
    y_rows = _expert_call(blk, used, rows, w1[l], b1[l], w2[l], b2[l])
    out = _combine_call(span, soff, src, goff, delta, x1, pos, gate, y_rows, g_final.reshape(1, d))
    return out.reshape(bsz, s_len, d)
```

```python
import functools
import math

import jax
import jax.numpy as jnp
from jax import lax
from jax.experimental import pallas as pl
from jax.experimental.pallas import tpu as pltpu

D_MODEL = 1024
N_Q_HEADS = 8
N_KV_HEADS = 2
HEAD_DIM = 64
GROUP = N_Q_HEADS // N_KV_HEADS
ATTN_WIDTH = N_Q_HEADS * HEAD_DIM
KV_WIDTH = N_KV_HEADS * HEAD_DIM
CONV_WIDTH = D_MODEL - ATTN_WIDTH
QKV_COLS = ATTN_WIDTH + 2 * KV_WIDTH
IN_COLS = QKV_COLS + 3 * CONV_WIDTH
WINDOW = 128
BLOCK = 128
BAND = 2 * BLOCK
N_EXPERTS = 32
TOP_K = 4
D_FF = D_MODEL
SWIGLU_ALPHA = 1.702
SWIGLU_LIMIT = 7.0
EPS = 1e-5

LANES = 128
SUBLANES = 8
VMEM_LIMIT_BYTES = 56 * 1024 * 1024

TOKEN_TILE = 1024
SUB_TILE = 512
ROUTE_TILE = 256
ROW_BLOCK = 512
STAGE_ROWS = -(-(TOP_K * ROUTE_TILE + N_EXPERTS * (SUBLANES - 1)) // LANES) * LANES
PACK_ROWS = 16
COMBINE_ROWS = -(-(TOP_K * ROUTE_TILE + N_EXPERTS * (SUBLANES - 1 + PACK_ROWS)) // LANES) * LANES


def _rms(x, g):
    return x * lax.rsqrt(jnp.mean(x * x, axis=-1, keepdims=True) + EPS) * g


def _excl_cumsum(a, axis):
    idx = lax.broadcasted_iota(jnp.int32, a.shape, axis)
    inc = a
    step = 1
    while step < a.shape[axis]:
        inc = inc + jnp.where(idx >= step, pltpu.roll(inc, step, axis), 0)
        step *= 2
    return inc - a


def _mixer_kernel(tiles_per_seq,
                  sink_ref,
                  x_ref, gmix_ref, win_ref, convw_ref, bias_ref, gattn_ref, gconv_ref, wout_ref,
                  gffn_ref, wr_ref, br_ref,
                  x1_ref, h2_ref, pos_ref, gate_ref, cnt_ref,
                  p_scr, k_scr, vt_scr, u_scr, attn_t_scr, tri_scr):
    tm = x_ref.shape[0]
    sub_tm = SUB_TILE
    ts = ROUTE_TILE
    i = pl.program_id(0)
    first = (i % tiles_per_seq) == 0
    f32, bf16 = jnp.float32, jnp.bfloat16

    @pl.when(i == 0)
    def _():
        cnt_ref[...] = jnp.zeros_like(cnt_ref)
        r = lax.broadcasted_iota(jnp.int32, (ts, ts), 0)
        c = lax.broadcasted_iota(jnp.int32, (ts, ts), 1)
        tri_scr[...] = jnp.where(r < c, 1.0, 0.0).astype(bf16)

    @pl.when(first)
    def _():
        k_scr[0:BLOCK, :] = jnp.zeros((BLOCK, k_scr.shape[1]), bf16)
        vt_scr[:, 0:BLOCK] = jnp.zeros((KV_WIDTH, BLOCK), bf16)
        u_scr[0:SUBLANES, :] = jnp.zeros((SUBLANES, CONV_WIDTH), f32)

    for sub in range(tm // sub_tm):
        base = sub * sub_tm
        r0 = slice(base, base + sub_tm)
        x = x_ref[r0, :]
        h = _rms(x, gmix_ref[...])
        p_scr[r0, :] = jnp.dot(h.astype(bf16), win_ref[...], preferred_element_type=f32)

        kk = p_scr[r0, ATTN_WIDTH:ATTN_WIDTH + KV_WIDTH]
        kr = pltpu.roll(kk, HEAD_DIM, axis=1)
        k_scr[BLOCK + base:BLOCK + base + sub_tm, :] = jnp.concatenate([kk, kr], axis=1).astype(bf16)
        vt_scr[:, BLOCK + base:BLOCK + base + sub_tm] = jnp.transpose(
            p_scr[r0, ATTN_WIDTH + KV_WIDTH:QKV_COLS]).astype(bf16)

        lane = lax.broadcasted_iota(jnp.int32, (BAND, LANES), 1)
        lo = lane < HEAD_DIM
        qgrp = lax.broadcasted_iota(jnp.int32, (BLOCK, 2 * LANES), 1) // HEAD_DIM
        key_row = lax.broadcasted_iota(jnp.int32, (BAND, GROUP * BLOCK), 0)
        scale = 1.0 / math.sqrt(HEAD_DIM)
        zero_q = jnp.zeros((BLOCK, 2 * LANES), bf16)
        nt = (((1,), (1,)), ((), ()))

        for j in range(sub * (sub_tm // BLOCK), (sub + 1) * (sub_tm // BLOCK)):
            rows = slice(j * BLOCK, (j + 1) * BLOCK)
            kband = k_scr[j * BLOCK:j * BLOCK + BAND, :]
            bk, bkr = kband[:, 0:LANES], kband[:, LANES:2 * LANES]
            vt_band = vt_scr[:, j * BLOCK:j * BLOCK + BAND]
            for hh in range(N_KV_HEADS):
                k_both = jnp.where(lo, bk, bkr) if hh == 0 else jnp.where(lo, bkr, bk)
                kmat = jnp.concatenate([k_both, k_both], axis=1)
                qh = (p_scr[rows, hh * 2 * LANES:(hh + 1) * 2 * LANES] * scale).astype(bf16)
                qm = jnp.concatenate([jnp.where(qgrp == g, qh, zero_q) for g in range(GROUP)], axis=0)
                s = lax.dot_general(kmat, qm, nt, preferred_element_type=f32) + bias_ref[hh]
                if j == 0:
                    s = jnp.where(jnp.logical_and(first, key_row < BLOCK), -jnp.inf, s)
                sink = jnp.concatenate(
                    [jnp.full((1, BLOCK), sink_ref[hh * GROUP + g], f32) for g in range(GROUP)], axis=1)
                m = jnp.maximum(jnp.max(s, axis=0, keepdims=True), sink)
                e = jnp.exp(s - m)
                inv = 1.0 / (jnp.sum(e, axis=0, keepdims=True) + jnp.exp(sink - m))
                o_t = jnp.dot(vt_band, e.astype(bf16), preferred_element_type=f32)
                o_t = o_t[hh * HEAD_DIM:(hh + 1) * HEAD_DIM, :] * inv
                for g in range(GROUP):
                    head = hh * GROUP + g
                    attn_t_scr[head * HEAD_DIM:(head + 1) * HEAD_DIM, rows] = o_t[:, g * BLOCK:(g + 1) * BLOCK]

        bg = p_scr[r0, QKV_COLS:QKV_COLS + CONV_WIDTH]
        cg = p_scr[r0, QKV_COLS + CONV_WIDTH:QKV_COLS + 2 * CONV_WIDTH]
        xin = p_scr[r0, QKV_COLS + 2 * CONV_WIDTH:IN_COLS]
        u = cg * xin
        u_scr[SUBLANES + base:SUBLANES + base + sub_tm, :] = u
        u1 = u_scr[SUBLANES - 1 + base:SUBLANES - 1 + base + sub_tm, :]
        u2 = u_scr[SUBLANES - 2 + base:SUBLANES - 2 + base + sub_tm, :]
        y = convw_ref[0:1, :] * u2 + convw_ref[1:2, :] * u1 + convw_ref[2:3, :] * u
        conv = bg * y
        mix = jnp.concatenate(
            [_rms(jnp.transpose(attn_t_scr[:, r0]), gattn_ref[...]), _rms(conv, gconv_ref[...])], axis=1)
        x1 = x + jnp.dot(mix.astype(bf16), wout_ref[...], preferred_element_type=f32)
        x1_ref[r0, :] = x1
        h2 = _rms(x1, gffn_ref[...])
        h2_hi = h2.astype(bf16)
        h2_ref[r0, :] = h2_hi

        h2_lo = (h2 - h2_hi.astype(f32)).astype(bf16)
        by_hi = lax.dot_general(wr_ref[...], h2_hi, nt, preferred_element_type=f32)
        by_lo = lax.dot_general(wr_ref[0:N_EXPERTS, :], h2_lo, nt, preferred_element_type=f32)
        logits = by_hi[0:N_EXPERTS] + by_hi[N_EXPERTS:] + by_lo + br_ref[...]
        row = lax.broadcasted_iota(jnp.int32, (N_EXPERTS, sub_tm), 0)
        vals, idxs = [], []
        l = logits
        for _ in range(TOP_K):
            m = jnp.max(l, axis=0, keepdims=True)
            idx = jnp.min(jnp.where(l == m, row, N_EXPERTS), axis=0, keepdims=True)
            vals.append(m)
            idxs.append(idx)
            l = jnp.where(row == idx, -jnp.inf, l)
        es = [jnp.exp(v - vals[0]) for v in vals]
        inv = 1.0 / (es[0] + es[1] + es[2] + es[3])
        gate_ref[:, r0] = jnp.concatenate([e * inv for e in es], axis=0)

        onehots = [row == idx for idx in idxs]
        sel = jnp.logical_or(jnp.logical_or(onehots[0], onehots[1]), jnp.logical_or(onehots[2], onehots[3]))
        oh = jnp.where(sel, 1.0, 0.0)

        cnt_lane = lax.broadcasted_iota(jnp.int32, cnt_ref.shape, 1)
        poss = []
        for rt in range(sub_tm // ts):
            sl = slice(rt * ts, (rt + 1) * ts)
            oh_s = oh[:, sl]
            cum = jnp.dot(oh_s.astype(bf16), tri_scr[...], preferred_element_type=f32)
            cnt = jnp.sum(oh_s, axis=1, keepdims=True)
            padded = SUBLANES * jnp.floor((cnt + (SUBLANES - 1.0)) * (1.0 / SUBLANES))
            goff = _excl_cumsum(jnp.broadcast_to(padded, (N_EXPERTS, LANES)), 0)[:, 0:1]
            pos = goff + cum
            poss.append(jnp.concatenate(
                [jnp.sum(jnp.where(o[:, sl], pos, 0.0), axis=0, keepdims=True) for o in onehots], axis=0))
            tile_id = i * (tm // ts) + sub * (sub_tm // ts) + rt
            cnt_ref[...] = jnp.where(cnt_lane == tile_id, cnt.astype(jnp.int32), cnt_ref[...])
        pos_ref[:, r0] = jnp.concatenate(poss, axis=1).astype(jnp.int32)

    k_scr[0:BLOCK, :] = k_scr[tm:tm + BLOCK, :]
    vt_scr[:, 0:BLOCK] = vt_scr[:, tm:tm + BLOCK]
    u_scr[0:SUBLANES, :] = u_scr[tm:tm + SUBLANES, :]


def _mixer_call(x2d, sinks, g_mix, w_in, conv_w, bias, g_attn, g_conv, w_out, g_ffn, wr_split, br, seq_len):
    n_tok = x2d.shape[0]
    tm = TOKEN_TILE
    n_tiles = n_tok // tm
    assert n_tok // ROUTE_TILE == LANES, "one lane of the count table per routing tile"
    const = lambda shape: pl.BlockSpec(shape, lambda i, *_: (0,) * len(shape))
    grid_spec = pltpu.PrefetchScalarGridSpec(
        num_scalar_prefetch=0,
        grid=(n_tiles,),
        in_specs=[
            pl.BlockSpec(memory_space=pltpu.SMEM),
            pl.BlockSpec((tm, D_MODEL), lambda i: (i, 0)),
            const((1, D_MODEL)), const((D_MODEL, IN_COLS)), const((3, CONV_WIDTH)),
            const((N_KV_HEADS, BAND, GROUP * BLOCK)),
            const((1, ATTN_WIDTH)), const((1, CONV_WIDTH)), const((D_MODEL, D_MODEL)),
            const((1, D_MODEL)), const((2 * N_EXPERTS, D_MODEL)), const((N_EXPERTS, 1)),
        ],
        out_specs=[
            pl.BlockSpec((tm, D_MODEL), lambda i: (i, 0)),
            pl.BlockSpec((tm, D_MODEL), lambda i: (i, 0)),
            pl.BlockSpec((TOP_K, tm), lambda i: (0, i)),
            pl.BlockSpec((TOP_K, tm), lambda i: (0, i)),
            pl.BlockSpec((N_EXPERTS, LANES), lambda i: (0, 0)),
        ],
        scratch_shapes=[
            pltpu.VMEM((tm, IN_COLS), jnp.float32),
            pltpu.VMEM((BLOCK + tm, 2 * LANES), jnp.bfloat16),
            pltpu.VMEM((KV_WIDTH, BLOCK + tm), jnp.bfloat16),
            pltpu.VMEM((SUBLANES + tm, CONV_WIDTH), jnp.float32),
            pltpu.VMEM((ATTN_WIDTH, tm), jnp.float32),
            pltpu.VMEM((ROUTE_TILE, ROUTE_TILE), jnp.bfloat16),
        ])
    out_shape = [
        jax.ShapeDtypeStruct((n_tok, D_MODEL), jnp.float32),
        jax.ShapeDtypeStruct((n_tok, D_MODEL), jnp.bfloat16),
        jax.ShapeDtypeStruct((TOP_K, n_tok), jnp.int32),
        jax.ShapeDtypeStruct((TOP_K, n_tok), jnp.float32),
        jax.ShapeDtypeStruct((N_EXPERTS, LANES), jnp.int32),
    ]
    return pl.pallas_call(
        functools.partial(_mixer_kernel, seq_len // tm),
        grid_spec=grid_spec, out_shape=out_shape, name="mixer_router",
        compiler_params=pltpu.CompilerParams(dimension_semantics=("arbitrary",),
                                             vmem_limit_bytes=VMEM_LIMIT_BYTES),
    )(sinks, x2d, g_mix, w_in, conv_w, bias, g_attn, g_conv, w_out, g_ffn, wr_split, br)


def _tables_kernel(cnt_ref, size_ref, goff_ref, gdst_ref, blk_ref, used_ref,
                   span_ref, soff_ref, src_ref, delta_ref):
    size = ((cnt_ref[...] + (SUBLANES - 1)) // SUBLANES) * SUBLANES
    size_ref[...] = size
    goff = _excl_cumsum(size, 0)
    goff_ref[...] = goff
    total = jnp.broadcast_to(jnp.sum(size, axis=1, keepdims=True), size.shape)
    region = ((total + (ROW_BLOCK - 1)) // ROW_BLOCK) * ROW_BLOCK
    pstart = _excl_cumsum(region, 0)
    gdst = pstart + _excl_cumsum(size, 1)
    gdst_ref[...] = gdst
    src = (gdst // PACK_ROWS) * PACK_ROWS
    span = jnp.where(size > 0, ((gdst + size + (PACK_ROWS - 1)) // PACK_ROWS) * PACK_ROWS - src, 0)
    soff = _excl_cumsum(span, 0)
    span_ref[...] = span
    soff_ref[...] = soff
    src_ref[...] = src
    delta_ref[...] = soff + (gdst - src) - goff
    pend = (pstart + region)[:, 0:1]
    start = lax.broadcasted_iota(jnp.int32, (N_EXPERTS, blk_ref.shape[1]), 1) * ROW_BLOCK
    owner = jnp.sum(jnp.where(pend <= start, 1, 0), axis=0, keepdims=True)
    blk_ref[...] = jnp.minimum(owner, N_EXPERTS - 1)
    used_ref[...] = jnp.broadcast_to(jnp.max(pend, axis=0, keepdims=True) // ROW_BLOCK, used_ref.shape)


def _tables_call(cnt, n_blocks):
    nb_pad = -(-n_blocks // LANES) * LANES
    tbl = jax.ShapeDtypeStruct((N_EXPERTS, LANES), jnp.int32)
    return pl.pallas_call(
        _tables_kernel,
        out_shape=[tbl, tbl, tbl, jax.ShapeDtypeStruct((1, nb_pad), jnp.int32),
                   jax.ShapeDtypeStruct((1, LANES), jnp.int32), tbl, tbl, tbl, tbl],
        name="group_tables",
    )(cnt)


def _start_group_copies(tile, size_ref, goff_ref, gdst_ref, sem, make_copy, align):
    def body(e, rows):
        n = pl.multiple_of(size_ref[e * LANES + tile], align)
        @pl.when(n > 0)
        def _():
            off = pl.multiple_of(goff_ref[e * LANES + tile], align)
            dst = pl.multiple_of(gdst_ref[e * LANES + tile], align)
            make_copy(off, dst, n, sem).start()
        return rows + n
    return lax.fori_loop(0, N_EXPERTS, body, jnp.int32(0))


def _wait_rows(rows_hbm, n, sem, align):
    @pl.when(n > 0)
    def _():
        m = pl.multiple_of(n, align)
        half = rows_hbm.shape[0] // (2 * align) * align
        pltpu.make_async_copy(rows_hbm.at[pl.ds(0, m)], rows_hbm.at[pl.ds(half, m)], sem).wait()


def _dispatch_kernel(n_blocks,
                     size_ref, goff_ref, gdst_ref, used_ref,
                     h2_ref, pos_ref, rows_ref,
                     stage, zbuf, inflight, sems, zsem):
    t = pl.program_id(0)
    n_tiles = pl.num_programs(0)
    slot = t % 2

    @pl.when(t == 0)
    def _():
        zbuf[...] = jnp.zeros_like(zbuf)
        def zero_block(start):
            pltpu.make_async_copy(
                zbuf, rows_ref.at[pl.ds(pl.multiple_of(start, ROW_BLOCK), ROW_BLOCK)], zsem).start()
        def zfill(e, n):
            first = gdst_ref[e * LANES]
            last = gdst_ref[e * LANES + LANES - 1] + size_ref[e * LANES + LANES - 1]
            @pl.when(last > first)
            def _():
                zero_block((last + ROW_BLOCK - 1) // ROW_BLOCK * ROW_BLOCK - ROW_BLOCK)
            return n + jnp.where(last > first, ROW_BLOCK, 0)
        n_zero = lax.fori_loop(0, N_EXPERTS, zfill, jnp.int32(0))
        def ztail(b, _):
            zero_block(b * ROW_BLOCK)
            return 0
        lax.fori_loop(used_ref[0], n_blocks, ztail, 0)
        _wait_rows(rows_ref, n_zero + (n_blocks - used_ref[0]) * ROW_BLOCK, zsem, SUBLANES)

    @pl.when(t >= 2)
    def _():
        _wait_rows(rows_ref, inflight[slot], sems.at[slot], SUBLANES)

    pos = pos_ref[...]
    r = lax.broadcasted_iota(jnp.int32, (STAGE_ROWS, pos.shape[1]), 0)
    hit = jnp.logical_or(jnp.logical_or(r == pos[0:1], r == pos[1:2]),
                         jnp.logical_or(r == pos[2:3], r == pos[3:4]))
    perm = jnp.where(hit, 1.0, 0.0).astype(jnp.bfloat16)
    stage[slot] = jnp.dot(perm, h2_ref[...], preferred_element_type=jnp.float32)

    def make_copy(off, dst, n, sem):
        return pltpu.make_async_copy(stage.at[slot, pl.ds(off, n)], rows_ref.at[pl.ds(dst, n)], sem)
    inflight[slot] = _start_group_copies(t, size_ref, goff_ref, gdst_ref, sems.at[slot], make_copy, SUBLANES)

    @pl.when(t == n_tiles - 1)
    def _():
        _wait_rows(rows_ref, inflight[1 - slot], sems.at[1 - slot], SUBLANES)
        _wait_rows(rows_ref, inflight[slot], sems.at[slot], SUBLANES)


def _dispatch_call(size, goff, gdst, used, h2, pos, n_rows):
    n_tok = h2.shape[0]
    ts = ROUTE_TILE
    n_blocks = n_rows // ROW_BLOCK
    grid_spec = pltpu.PrefetchScalarGridSpec(
        num_scalar_prefetch=4,
        grid=(n_tok // ts,),
        in_specs=[
            pl.BlockSpec((ts, D_MODEL), lambda t, *_: (t, 0)),
            pl.BlockSpec((TOP_K, ts), lambda t, *_: (0, t)),
        ],
        out_specs=pl.BlockSpec(memory_space=pl.ANY),
        scratch_shapes=[
            pltpu.VMEM((2, STAGE_ROWS, D_MODEL), jnp.float32),
            pltpu.VMEM((ROW_BLOCK, D_MODEL), jnp.float32),
            pltpu.SMEM((2,), jnp.int32),
            pltpu.SemaphoreType.DMA((2,)),
            pltpu.SemaphoreType.DMA(()),
        ])
    return pl.pallas_call(
        functools.partial(_dispatch_kernel, n_blocks),
        grid_spec=grid_spec,
        out_shape=jax.ShapeDtypeStruct((n_rows, D_MODEL), jnp.float32),
        name="dispatch_rows",
        compiler_params=pltpu.CompilerParams(dimension_semantics=("arbitrary",),
                                             vmem_limit_bytes=VMEM_LIMIT_BYTES),
    )(size, goff, gdst, used, h2, pos)


def _expert_kernel(blk_ref, used_ref, x_ref, w1_ref, b1_ref, w2_ref, b2_ref, y_ref, w1_bf, w2_bf, prev_e):
    i = pl.program_id(0)
    bf16 = jnp.bfloat16
    e = blk_ref[i]

    @pl.when(i == 0)
    def _():
        prev_e[0] = jnp.int32(-1)

    @pl.when(e != prev_e[0])
    def _():
        w1_bf[...] = w1_ref[0].astype(bf16)
        w2_bf[...] = w2_ref[0].astype(bf16)
        prev_e[0] = e

    @pl.when(i >= used_ref[0])
    def _():
        y_ref[...] = jnp.zeros_like(y_ref)

    @pl.when(i < used_ref[0])
    def _():
        hu = jnp.dot(x_ref[...].astype(bf16), w1_bf[...], preferred_element_type=jnp.float32) + b1_ref[0]
        glu = jnp.minimum(hu[:, :D_FF], SWIGLU_LIMIT)
        lin = jnp.clip(hu[:, D_FF:], -SWIGLU_LIMIT, SWIGLU_LIMIT)
        act = glu * (1.0 / (1.0 + jnp.exp(-SWIGLU_ALPHA * glu))) * (lin + 1.0)
        y = jnp.dot(act.astype(bf16), w2_bf[...], preferred_element_type=jnp.float32) + b2_ref[0]
        y_ref[...] = y.astype(bf16)


def _expert_call(blk, used, rows, w1, b1, w2, b2):
    n_rows = rows.shape[0]
    n_blocks = n_rows // ROW_BLOCK
    last = lambda i, b, u: jnp.maximum(jnp.minimum(i, u[0] - 1), 0)
    grid_spec = pltpu.PrefetchScalarGridSpec(
        num_scalar_prefetch=2,
        grid=(n_blocks,),
        in_specs=[
            pl.BlockSpec((ROW_BLOCK, D_MODEL), lambda i, b, u: (last(i, b, u), 0)),
            pl.BlockSpec((1, D_MODEL, 2 * D_FF), lambda i, b, u: (b[i], 0, 0)),
            pl.BlockSpec((1, 1, 2 * D_FF), lambda i, b, u: (b[i], 0, 0)),
            pl.BlockSpec((1, D_FF, D_MODEL), lambda i, b, u: (b[i], 0, 0)),
            pl.BlockSpec((1, 1, D_MODEL), lambda i, b, u: (b[i], 0, 0)),
        ],
        out_specs=pl.BlockSpec((ROW_BLOCK, D_MODEL), lambda i, b, u: (i, 0)),
        scratch_shapes=[
            pltpu.VMEM((D_MODEL, 2 * D_FF), jnp.bfloat16),
            pltpu.VMEM((D_FF, D_MODEL), jnp.bfloat16),
            pltpu.SMEM((1,), jnp.int32),
        ])
    return pl.pallas_call(
        _expert_kernel,
        grid_spec=grid_spec,
        out_shape=jax.ShapeDtypeStruct((n_rows, D_MODEL), jnp.bfloat16),
        name="expert_ffn",
        compiler_params=pltpu.CompilerParams(dimension_semantics=("arbitrary",),
                                             vmem_limit_bytes=VMEM_LIMIT_BYTES),
    )(blk, used, rows, w1, b1.reshape(N_EXPERTS, 1, 2 * D_FF), w2, b2.reshape(N_EXPERTS, 1, D_MODEL))


def _combine_kernel(span_ref, soff_ref, src_ref, goff_ref, delta_ref,
                    x1_ref, pos_ref, gate_ref, y_ref, gfin_ref, out_ref,
                    stage, inflight, sems):
    t = pl.program_id(0)
    n_tiles = pl.num_programs(0)
    slot = t % 2

    def fetch(tile, slot_):
        def make_copy(off, src, n, sem):
            return pltpu.make_async_copy(y_ref.at[pl.ds(src, n)], stage.at[slot_, pl.ds(off, n)], sem)
        inflight[slot_] = _start_group_copies(tile, span_ref, soff_ref, src_ref, sems.at[slot_], make_copy,
                                              PACK_ROWS)

    @pl.when(t == 0)
    def _():
        stage[...] = jnp.zeros_like(stage)
        fetch(0, 0)

    @pl.when(t + 1 < n_tiles)
    def _():
        fetch(t + 1, 1 - slot)

    _wait_rows(y_ref, inflight[slot], sems.at[slot], PACK_ROWS)

    pos = pos_ref[...]
    shift = jnp.zeros_like(pos)
    prev = jnp.int32(0)
    for e in range(N_EXPERTS):
        d = delta_ref[e * LANES + t]
        shift = shift + jnp.where(pos >= goff_ref[e * LANES + t], d - prev, 0)
        prev = d
    pos = pos + shift
    gate = gate_ref[...]
    r = lax.broadcasted_iota(jnp.int32, (COMBINE_ROWS, pos.shape[1]), 0)
    w = jnp.where(r == pos[0:1], gate[0:1], 0.0)
    for k in range(1, TOP_K):
        w = w + jnp.where(r == pos[k:k + 1], gate[k:k + 1], 0.0)
    moe = lax.dot_general(w.astype(jnp.bfloat16), stage[slot],
                          (((0,), (0,)), ((), ())), preferred_element_type=jnp.float32)
    out_ref[...] = _rms(x1_ref[...] + moe, gfin_ref[...])


def _combine_call(span, soff, src, goff, delta, x1, pos, gate, y_rows, g_final):
    n_tok = x1.shape[0]
    ts = ROUTE_TILE
    grid_spec = pltpu.PrefetchScalarGridSpec(
        num_scalar_prefetch=5,
        grid=(n_tok // ts,),
        in_specs=[
            pl.BlockSpec((ts, D_MODEL), lambda t, *_: (t, 0)),
            pl.BlockSpec((TOP_K, ts), lambda t, *_: (0, t)),
            pl.BlockSpec((TOP_K, ts), lambda t, *_: (0, t)),
            pl.BlockSpec(memory_space=pl.ANY),
            pl.BlockSpec((1, D_MODEL), lambda t, *_: (0, 0)),
        ],
        out_specs=pl.BlockSpec((ts, D_MODEL), lambda t, *_: (t, 0)),
        scratch_shapes=[
            pltpu.VMEM((2, COMBINE_ROWS, D_MODEL), jnp.bfloat16),
            pltpu.SMEM((2,), jnp.int32),
            pltpu.SemaphoreType.DMA((2,)),
        ])
    return pl.pallas_call(
        _combine_kernel,
        grid_spec=grid_spec,
        out_shape=jax.ShapeDtypeStruct((n_tok, D_MODEL), jnp.float32),
        name="combine_norm",
        compiler_params=pltpu.CompilerParams(dimension_semantics=("arbitrary",),
                                             vmem_limit_bytes=VMEM_LIMIT_BYTES),
    )(span, soff, src, goff, delta, x1, pos, gate, y_rows, g_final)


def _attention_bias():
    slopes = jnp.exp2(-8.0 * jnp.arange(1, N_Q_HEADS + 1, dtype=jnp.float32) / N_Q_HEADS)
    qi = jnp.arange(BLOCK)[:, None]
    kj = jnp.arange(BAND)[None, :]
    dist = qi + BLOCK - kj
    valid = (dist >= 0) & (dist < WINDOW)
    b = -slopes[:, None, None] * dist.astype(jnp.float32)[None]
    b = jnp.where(valid[None], b, -jnp.inf)
    b = b.reshape(N_KV_HEADS, GROUP, BLOCK, BAND).transpose(0, 3, 1, 2)
    return b.reshape(N_KV_HEADS, BAND, GROUP * BLOCK)


def kernel(x, g_mix, w_in, conv_w, sinks, g_attn_out, g_conv_out, w_out, g_ffn, w_router, b_router,
           w1, b1, w2, b2, g_final):
    bsz, s_len, d = x.shape
    n_tok = bsz * s_len
    n_tiles = n_tok // ROUTE_TILE
    n_rows = n_tok * TOP_K + n_tiles * N_EXPERTS * SUBLANES + N_EXPERTS * ROW_BLOCK
    n_blocks = n_rows // ROW_BLOCK
    x2d = x.reshape(n_tok, d)
    bias = _attention_bias()
    l = 0
    wrt = w_router[l].T
    wr_hi = wrt.astype(jnp.bfloat16)
    wr_lo = (wrt - wr_hi.astype(jnp.float32)).astype(jnp.bfloat16)
    wr_split = jnp.concatenate([wr_hi, wr_lo], axis=0)
    x1, h2, pos, gate, cnt = _mixer_call(
        x2d, sinks[l], g_mix[l].reshape(1, d), w_in[l].astype(jnp.bfloat16), conv_w[l], bias,
        g_attn_out[l].reshape(1, ATTN_WIDTH), g_conv_out[l].reshape(1, CONV_WIDTH),
        w_out[l].astype(jnp.bfloat16), g_ffn[l].reshape(1, d), wr_split,
        b_router[l].reshape(N_EXPERTS, 1), s_len)
    size, goff, gdst, blk, used, span, soff, src, delta = _tables_call(cnt, n_blocks)
    size, goff, gdst = size.reshape(-1), goff.reshape(-1), gdst.reshape(-1)
    span, soff, src, delta = span.reshape(-1), soff.reshape(-1), src.reshape(-1), delta.reshape(-1)
    blk, used = blk.reshape(-1), used.reshape(-1)[:1]
    rows = _dispatch_call(size, goff, gdst, used, h2, pos, n_rows)
    y_rows = _expert_call(blk, used, rows, w1[l], b1[l], w2[l], b2[l])
    out = _combine_call(span, soff, src, goff, delta, x1, pos, gate, y_rows, g_final.reshape(1, d))
    return out.reshape(bsz, s_len, d)
```

```python
import functools
import math

import jax
import jax.numpy as jnp
from jax import lax
from jax.experimental import pallas as pl
from jax.experimental.pallas import tpu as pltpu

D_MODEL = 1024
N_Q_HEADS = 8
N_KV_HEADS = 2
HEAD_DIM = 64
GROUP = N_Q_HEADS // N_KV_HEADS
ATTN_WIDTH = N_Q_HEADS * HEAD_DIM
KV_WIDTH = N_KV_HEADS * HEAD_DIM
CONV_WIDTH = D_MODEL - ATTN_WIDTH
QKV_COLS = ATTN_WIDTH + 2 * KV_WIDTH
IN_COLS = QKV_COLS + 3 * CONV_WIDTH
WINDOW = 128
BLOCK = 128
BAND = 2 * BLOCK
N_EXPERTS = 32
TOP_K = 4
D_FF = D_MODEL
SWIGLU_ALPHA = 1.702
SWIGLU_LIMIT = 7.0
EPS = 1e-5

LANES = 128
SUBLANES = 8
VMEM_LIMIT_BYTES = 56 * 1024 * 1024

TOKEN_TILE = 1024
SUB_TILE = 512
ROUTE_TILE = 256
ROW_BLOCK = 512
STAGE_ROWS = -(-(TOP_K * ROUTE_TILE + N_EXPERTS * (SUBLANES - 1)) // LANES) * LANES
PACK_ROWS = 16
COMBINE_ROWS = -(-(TOP_K * ROUTE_TILE + N_EXPERTS * (SUBLANES - 1 + PACK_ROWS)) // LANES) * LANES


def _rms(x, g):
    return x * lax.rsqrt(jnp.mean(x * x, axis=-1, keepdims=True) + EPS) * g


def _excl_cumsum(a, axis):
    idx = lax.broadcasted_iota(jnp.int32, a.shape, axis)
    inc = a
    step = 1
    while step < a.shape[axis]:
        inc = inc + jnp.where(idx >= step, pltpu.roll(inc, step, axis), 0)
        step *= 2
    return inc - a


def _mixer_kernel(tiles_per_seq,
                  sink_ref,
                  x_ref, gmix_ref, win_ref, convw_ref, bias_ref, gattn_ref, gconv_ref, wout_ref,
                  gffn_ref, wr_ref, br_ref,
                  x1_ref, h2_ref, pos_ref, gate_ref, cnt_ref,
                  p_scr, k_scr, vt_scr, u_scr, attn_t_scr, tri_scr):
    tm = x_ref.shape[0]
    sub_tm = SUB_TILE
    ts = ROUTE_TILE
    i = pl.program_id(0)
    first = (i % tiles_per_seq) == 0
    f32, bf16 = jnp.float32, jnp.bfloat16

    @pl.when(i == 0)
    def _():
        cnt_ref[...] = jnp.zeros_like(cnt_ref)
        r = lax.broadcasted_iota(jnp.int32, (ts, ts), 0)
        c = lax.broadcasted_iota(jnp.int32, (ts, ts), 1)
        tri_scr[...] = jnp.where(r < c, 1.0, 0.0).astype(bf16)

    @pl.when(first)
    def _():
        k_scr[0:BLOCK, :] = jnp.zeros((BLOCK, k_scr.shape[1]), bf16)
        vt_scr[:, 0:BLOCK] = jnp.zeros((KV_WIDTH, BLOCK), bf16)
        u_scr[0:SUBLANES, :] = jnp.zeros((SUBLANES, CONV_WIDTH), f32)

    for sub in range(tm // sub_tm):
        base = sub * sub_tm
        r0 = slice(base, base + sub_tm)
        x = x_ref[r0, :]
        h = _rms(x, gmix_ref[...])
        p_scr[r0, :] = jnp.dot(h.astype(bf16), win_ref[...], preferred_element_type=f32)

        kk = p_scr[r0, ATTN_WIDTH:ATTN_WIDTH + KV_WIDTH]
        kr = pltpu.roll(kk, HEAD_DIM, axis=1)
        k_scr[BLOCK + base:BLOCK + base + sub_tm, :] = jnp.concatenate([kk, kr], axis=1).astype(bf16)
        vt_scr[:, BLOCK + base:BLOCK + base + sub_tm] = jnp.transpose(
            p_scr[r0, ATTN_WIDTH + KV_WIDTH:QKV_COLS]).astype(bf16)

        lane = lax.broadcasted_iota(jnp.int32, (BAND, LANES), 1)
        lo = lane < HEAD_DIM
        qgrp = lax.broadcasted_iota(jnp.int32, (BLOCK, 2 * LANES), 1) // HEAD_DIM
        key_row = lax.broadcasted_iota(jnp.int32, (BAND, GROUP * BLOCK), 0)
        scale = 1.0 / math.sqrt(HEAD_DIM)
        zero_q = jnp.zeros((BLOCK, 2 * LANES), bf16)
        nt = (((1,), (1,)), ((), ()))

        for j in range(sub * (sub_tm // BLOCK), (sub + 1) * (sub_tm // BLOCK)):
            rows = slice(j * BLOCK, (j + 1) * BLOCK)
            kband = k_scr[j * BLOCK:j * BLOCK + BAND, :]
            bk, bkr = kband[:, 0:LANES], kband[:, LANES:2 * LANES]
            vt_band = vt_scr[:, j * BLOCK:j * BLOCK + BAND]
            for hh in range(N_KV_HEADS):
                k_both = jnp.where(lo, bk, bkr) if hh == 0 else jnp.where(lo, bkr, bk)
                kmat = jnp.concatenate([k_both, k_both], axis=1)
                qh = (p_scr[rows, hh * 2 * LANES:(hh + 1) * 2 * LANES] * scale).astype(bf16)
                qm = jnp.concatenate([jnp.where(qgrp == g, qh, zero_q) for g in range(GROUP)], axis=0)
                s = lax.dot_general(kmat, qm, nt, preferred_element_type=f32) + bias_ref[hh]
                if j == 0:
                    s = jnp.where(jnp.logical_and(first, key_row < BLOCK), -jnp.inf, s)
                sink = jnp.concatenate(
                    [jnp.full((1, BLOCK), sink_ref[hh * GROUP + g], f32) for g in range(GROUP)], axis=1)
                m = jnp.maximum(jnp.max(s, axis=0, keepdims=True), sink)
                e = jnp.exp(s - m)
                inv = 1.0 / (jnp.sum(e, axis=0, keepdims=True) + jnp.exp(sink - m))
                o_t = jnp.dot(vt_band, e.astype(bf16), preferred_element_type=f32)
                o_t = o_t[hh * HEAD_DIM:(hh + 1) * HEAD_DIM, :] * inv
                for g in range(GROUP):
                    head = hh * GROUP + g
                    attn_t_scr[head * HEAD_DIM:(head + 1) * HEAD_DIM, rows] = o_t[:, g * BLOCK:(g + 1) * BLOCK]

        bg = p_scr[r0, QKV_COLS:QKV_COLS + CONV_WIDTH]
        cg = p_scr[r0, QKV_COLS + CONV_WIDTH:QKV_COLS + 2 * CONV_WIDTH]
        xin = p_scr[r0, QKV_COLS + 2 * CONV_WIDTH:IN_COLS]
        u = cg * xin
        u_scr[SUBLANES + base:SUBLANES + base + sub_tm, :] = u
        u1 = u_scr[SUBLANES - 1 + base:SUBLANES - 1 + base + sub_tm, :]
        u2 = u_scr[SUBLANES - 2 + base:SUBLANES - 2 + base + sub_tm, :]
        y = convw_ref[0:1, :] * u2 + convw_ref[1:2, :] * u1 + convw_ref[2:3, :] * u
        conv = bg * y
        mix = jnp.concatenate(
            [_rms(jnp.transpose(attn_t_scr[:, r0]), gattn_ref[...]), _rms(conv, gconv_ref[...])], axis=1)
        x1 = x + jnp.dot(mix.astype(bf16), wout_ref[...], preferred_element_type=f32)
        x1_ref[r0, :] = x1
        h2 = _rms(x1, gffn_ref[...])
        h2_hi = h2.astype(bf16)
        h2_ref[r0, :] = h2_hi

        h2_lo = (h2 - h2_hi.astype(f32)).astype(bf16)
        by_hi = lax.dot_general(wr_ref[...], h2_hi, nt, preferred_element_type=f32)
        by_lo = lax.dot_general(wr_ref[0:N_EXPERTS, :], h2_lo, nt, preferred_element_type=f32)
        logits = by_hi[0:N_EXPERTS] + by_hi[N_EXPERTS:] + by_lo + br_ref[...]
        row = lax.broadcasted_iota(jnp.int32, (N_EXPERTS, sub_tm), 0)
        vals, idxs = [], []
        l = logits
        for _ in range(TOP_K):
            m = jnp.max(l, axis=0, keepdims=True)
            idx = jnp.min(jnp.where(l == m, row, N_EXPERTS), axis=0, keepdims=True)
            vals.append(m)
            idxs.append(idx)
            l = jnp.where(row == idx, -jnp.inf, l)
        es = [jnp.exp(v - vals[0]) for v in vals]
        inv = 1.0 / (es[0] + es[1] + es[2] + es[3])
        gate_ref[:, r0] = jnp.concatenate([e * inv for e in es], axis=0)

        onehots = [row == idx for idx in idxs]
        sel = jnp.logical_or(jnp.logical_or(onehots[0], onehots[1]), jnp.logical_or(onehots[2], onehots[3]))
        oh = jnp.where(sel, 1.0, 0.0)

        cnt_lane = lax.broadcasted_iota(jnp.int32, cnt_ref.shape, 1)
        poss = []
        for rt in range(sub_tm // ts):
            sl = slice(rt * ts, (rt + 1) * ts)
            oh_s = oh[:, sl]
            cum = jnp.dot(oh_s.astype(bf16), tri_scr[...], preferred_element_type=f32)
            cnt = jnp.sum(oh_s, axis=1, keepdims=True)
            padded = SUBLANES * jnp.floor((cnt + (SUBLANES - 1.0)) * (1.0 / SUBLANES))
            goff = _excl_cumsum(jnp.broadcast_to(padded, (N_EXPERTS, LANES)), 0)[:, 0:1]
            pos = goff + cum
            poss.append(jnp.concatenate(
                [jnp.sum(jnp.where(o[:, sl], pos, 0.0), axis=0, keepdims=True) for o in onehots], axis=0))
            tile_id = i * (tm // ts) + sub * (sub_tm // ts) + rt
            cnt_ref[...] = jnp.where(cnt_lane == tile_id, cnt.astype(jnp.int32), cnt_ref[...])
        pos_ref[:, r0] = jnp.concatenate(poss, axis=1).astype(jnp.int32)

    k_scr[0:BLOCK, :] = k_scr[tm:tm + BLOCK, :]
    vt_scr[:, 0:BLOCK] = vt_scr[:, tm:tm + BLOCK]
    u_scr[0:SUBLANES, :] = u_scr[tm:tm + SUBLANES, :]


def _mixer_call(x2d, sinks, g_mix, w_in, conv_w, bias, g_attn, g_conv, w_out, g_ffn, wr_split, br, seq_len):
    n_tok = x2d.shape[0]
    tm = TOKEN_TILE
    n_tiles = n_tok // tm
    assert n_tok // ROUTE_TILE == LANES, "one lane of the count table per routing tile"
    const = lambda shape: pl.BlockSpec(shape, lambda i, *_: (0,) * len(shape))
    grid_spec = pltpu.PrefetchScalarGridSpec(
        num_scalar_prefetch=0,
        grid=(n_tiles,),
        in_specs=[
            pl.BlockSpec(memory_space=pltpu.SMEM),
            pl.BlockSpec((tm, D_MODEL), lambda i: (i, 0)),
            const((1, D_MODEL)), const((D_MODEL, IN_COLS)), const((3, CONV_WIDTH)),
            const((N_KV_HEADS, BAND, GROUP * BLOCK)),
            const((1, ATTN_WIDTH)), const((1, CONV_WIDTH)), const((D_MODEL, D_MODEL)),
            const((1, D_MODEL)), const((2 * N_EXPERTS, D_MODEL)), const((N_EXPERTS, 1)),
        ],
        out_specs=[
            pl.BlockSpec((tm, D_MODEL), lambda i: (i, 0)),
            pl.BlockSpec((tm, D_MODEL), lambda i: (i, 0)),
            pl.BlockSpec((TOP_K, tm), lambda i: (0, i)),
            pl.BlockSpec((TOP_K, tm), lambda i: (0, i)),
            pl.BlockSpec((N_EXPERTS, LANES), lambda i: (0, 0)),
        ],
        scratch_shapes=[
            pltpu.VMEM((tm, IN_COLS), jnp.float32),
            pltpu.VMEM((BLOCK + tm, 2 * LANES), jnp.bfloat16),
            pltpu.VMEM((KV_WIDTH, BLOCK + tm), jnp.bfloat16),
            pltpu.VMEM((SUBLANES + tm, CONV_WIDTH), jnp.float32),
            pltpu.VMEM((ATTN_WIDTH, tm), jnp.float32),
            pltpu.VMEM((ROUTE_TILE, ROUTE_TILE), jnp.bfloat16),
        ])
    out_shape = [
        jax.ShapeDtypeStruct((n_tok, D_MODEL), jnp.float32),
        jax.ShapeDtypeStruct((n_tok, D_MODEL), jnp.bfloat16),
        jax.ShapeDtypeStruct((TOP_K, n_tok), jnp.int32),
        jax.ShapeDtypeStruct((TOP_K, n_tok), jnp.float32),
        jax.ShapeDtypeStruct((N_EXPERTS, LANES), jnp.int32),
    ]
    return pl.pallas_call(
        functools.partial(_mixer_kernel, seq_len // tm),
        grid_spec=grid_spec, out_shape=out_shape, name="mixer_router",
        compiler_params=pltpu.CompilerParams(dimension_semantics=("arbitrary",),
                                             vmem_limit_bytes=VMEM_LIMIT_BYTES),
    )(sinks, x2d, g_mix, w_in, conv_w, bias, g_attn, g_conv, w_out, g_ffn, wr_split, br)


def _tables_kernel(cnt_ref, size_ref, goff_ref, gdst_ref, blk_ref, used_ref,
                   span_ref, soff_ref, src_ref, delta_ref):
    size = ((cnt_ref[...] + (SUBLANES - 1)) // SUBLANES) * SUBLANES
    size_ref[...] = size
    goff = _excl_cumsum(size, 0)
    goff_ref[...] = goff
    total = jnp.broadcast_to(jnp.sum(size, axis=1, keepdims=True), size.shape)
    region = ((total + (ROW_BLOCK - 1)) // ROW_BLOCK) * ROW_BLOCK
    pstart = _excl_cumsum(region, 0)
    gdst = pstart + _excl_cumsum(size, 1)
    gdst_ref[...] = gdst
    src = (gdst // PACK_ROWS) * PACK_ROWS
    span = jnp.where(size > 0, ((gdst + size + (PACK_ROWS - 1)) // PACK_ROWS) * PACK_ROWS - src, 0)
    soff = _excl_cumsum(span, 0)
    span_ref[...] = span
    soff_ref[...] = soff
    src_ref[...] = src
    delta_ref[...] = soff + (gdst - src) - goff
    pend = (pstart + region)[:, 0:1]
    start = lax.broadcasted_iota(jnp.int32, (N_EXPERTS, blk_ref.shape[1]), 1) * ROW_BLOCK
    owner = jnp.sum(jnp.where(pend <= start, 1, 0), axis=0, keepdims=True)
    blk_ref[...] = jnp.minimum(owner, N_EXPERTS - 1)
    used_ref[...] = jnp.broadcast_to(jnp.max(pend, axis=0, keepdims=True) // ROW_BLOCK, used_ref.shape)


def _tables_call(cnt, n_blocks):
    nb_pad = -(-n_blocks // LANES) * LANES
    tbl = jax.ShapeDtypeStruct((N_EXPERTS, LANES), jnp.int32)
    return pl.pallas_call(
        _tables_kernel,
        out_shape=[tbl, tbl, tbl, jax.ShapeDtypeStruct((1, nb_pad), jnp.int32),
                   jax.ShapeDtypeStruct((1, LANES), jnp.int32), tbl, tbl, tbl, tbl],
        name="group_tables",
    )(cnt)


def _start_group_copies(tile, size_ref, goff_ref, gdst_ref, sem, make_copy, align):
    def body(e, rows):
        n = pl.multiple_of(size_ref[e * LANES + tile], align)
        @pl.when(n > 0)
        def _():
            off = pl.multiple_of(goff_ref[e * LANES + tile], align)
            dst = pl.multiple_of(gdst_ref[e * LANES + tile], align)
            make_copy(off, dst, n, sem).start()
        return rows + n
    return lax.fori_loop(0, N_EXPERTS, body, jnp.int32(0))


def _wait_rows(rows_hbm, n, sem, align):
    @pl.when(n > 0)
    def _():
        m = pl.multiple_of(n, align)
        half = rows_hbm.shape[0] // (2 * align) * align
        pltpu.make_async_copy(rows_hbm.at[pl.ds(0, m)], rows_hbm.at[pl.ds(half, m)], sem).wait()


def _dispatch_kernel(n_blocks,
                     size_ref, goff_ref, gdst_ref, used_ref,
                     h2_ref, pos_ref, rows_ref,
                     stage, zbuf, inflight, sems, zsem):
    t = pl.program_id(0)
    n_tiles = pl.num_programs(0)
    slot = t % 2

    @pl.when(t == 0)
    def _():
        zbuf[...] = jnp.zeros_like(zbuf)
        def zero_block(start):
            pltpu.make_async_copy(
                zbuf, rows_ref.at[pl.ds(pl.multiple_of(start, ROW_BLOCK), ROW_BLOCK)], zsem).start()
        def zfill(e, n):
            first = gdst_ref[e * LANES]
            last = gdst_ref[e * LANES + LANES - 1] + size_ref[e * LANES + LANES - 1]
            @pl.when(last > first)
            def _():
                zero_block((last + ROW_BLOCK - 1) // ROW_BLOCK * ROW_BLOCK - ROW_BLOCK)
            return n + jnp.where(last > first, ROW_BLOCK, 0)
        n_zero = lax.fori_loop(0, N_EXPERTS, zfill, jnp.int32(0))
        def ztail(b, _):
            zero_block(b * ROW_BLOCK)
            return 0
        lax.fori_loop(used_ref[0], n_blocks, ztail, 0)
        _wait_rows(rows_ref, n_zero + (n_blocks - used_ref[0]) * ROW_BLOCK, zsem, SUBLANES)

    @pl.when(t >= 2)
    def _():
        _wait_rows(rows_ref, inflight[slot], sems.at[slot], SUBLANES)

    pos = pos_ref[...]
    r = lax.broadcasted_iota(jnp.int32, (STAGE_ROWS, pos.shape[1]), 0)
    hit = jnp.logical_or(jnp.logical_or(r == pos[0:1], r == pos[1:2]),
                         jnp.logical_or(r == pos[2:3], r == pos[3:4]))
    perm = jnp.where(hit, 1.0, 0.0).astype(jnp.bfloat16)
    stage[slot] = jnp.dot(perm, h2_ref[...], preferred_element_type=jnp.float32)

    def make_copy(off, dst, n, sem):
        return pltpu.make_async_copy(stage.at[slot, pl.ds(off, n)], rows_ref.at[pl.ds(dst, n)], sem)
    inflight[slot] = _start_group_copies(t, size_ref, goff_ref, gdst_ref, sems.at[slot], make_copy, SUBLANES)

    @pl.when(t == n_tiles - 1)
    def _():
        _wait_rows(rows_ref, inflight[1 - slot], sems.at[1 - slot], SUBLANES)
        _wait_rows(rows_ref, inflight[slot], sems.at[slot], SUBLANES)


def _dispatch_call(size, goff, gdst, used, h2, pos, n_rows):
    n_tok = h2.shape[0]
    ts = ROUTE_TILE
    n_blocks = n_rows // ROW_BLOCK
    grid_spec = pltpu.PrefetchScalarGridSpec(
        num_scalar_prefetch=4,
        grid=(n_tok // ts,),
        in_specs=[
            pl.BlockSpec((ts, D_MODEL), lambda t, *_: (t, 0)),
            pl.BlockSpec((TOP_K, ts), lambda t, *_: (0, t)),
        ],
        out_specs=pl.BlockSpec(memory_space=pl.ANY),
        scratch_shapes=[
            pltpu.VMEM((2, STAGE_ROWS, D_MODEL), jnp.float32),
            pltpu.VMEM((ROW_BLOCK, D_MODEL), jnp.float32),
            pltpu.SMEM((2,), jnp.int32),
            pltpu.SemaphoreType.DMA((2,)),
            pltpu.SemaphoreType.DMA(()),
        ])
    return pl.pallas_call(
        functools.partial(_dispatch_kernel, n_blocks),
        grid_spec=grid_spec,
        out_shape=jax.ShapeDtypeStruct((n_rows, D_MODEL), jnp.float32),
        name="dispatch_rows",
        compiler_params=pltpu.CompilerParams(dimension_semantics=("arbitrary",),
                                             vmem_limit_bytes=VMEM_LIMIT_BYTES),
    )(size, goff, gdst, used, h2, pos)


def _expert_kernel(blk_ref, used_ref, x_ref, w1_hbm, b1_ref, w2_hbm, b2_ref, y_ref,
                   w1_f32, w2_f32, w1_bf, w2_bf, state, sems):
    i = pl.program_id(0)
    bf16 = jnp.bfloat16
    e = blk_ref[i]
    used = used_ref[0]

    def weight_copies(expert, slot):
        return (pltpu.make_async_copy(w1_hbm.at[expert], w1_f32.at[slot], sems.at[0, slot]),
                pltpu.make_async_copy(w2_hbm.at[expert], w2_f32.at[slot], sems.at[1, slot]))

    @pl.when(i == 0)
    def _():
        state[0] = jnp.int32(-1)
        state[1] = jnp.int32(0)
        @pl.when(used > 0)
        def _():
            for cp in weight_copies(e, 0):
                cp.start()

    @pl.when(jnp.logical_and(i < used, e != state[0]))
    def _():
        slot = state[1]
        for cp in weight_copies(e, slot):
            cp.wait()
        w1_bf[...] = w1_f32[slot].astype(bf16)
        w2_bf[...] = w2_f32[slot].astype(bf16)
        nxt = lax.while_loop(lambda j: jnp.logical_and(j < used, blk_ref[jnp.minimum(j, used - 1)] == e),
                             lambda j: j + 1, i + 1)
        @pl.when(nxt < used)
        def _():
            for cp in weight_copies(blk_ref[nxt], 1 - slot):
                cp.start()
        state[0] = e
        state[1] = 1 - slot

    @pl.when(i >= used)
    def _():
        y_ref[...] = jnp.zeros_like(y_ref)

    @pl.when(i < used)
    def _():
        hu = jnp.dot(x_ref[...].astype(bf16), w1_bf[...], preferred_element_type=jnp.float32) + b1_ref[0]
        glu = jnp.minimum(hu[:, :D_FF], SWIGLU_LIMIT)
        lin = jnp.clip(hu[:, D_FF:], -SWIGLU_LIMIT, SWIGLU_LIMIT)
        act = glu * (1.0 / (1.0 + jnp.exp(-SWIGLU_ALPHA * glu))) * (lin + 1.0)
        y = jnp.dot(act.astype(bf16), w2_bf[...], preferred_element_type=jnp.float32) + b2_ref[0]
        y_ref[...] = y.astype(bf16)


def _expert_call(blk, used, rows, w1, b1, w2, b2):
    n_rows = rows.shape[0]
    n_blocks = n_rows // ROW_BLOCK
    last = lambda i, b, u: jnp.maximum(jnp.minimum(i, u[0] - 1), 0)
    grid_spec = pltpu.PrefetchScalarGridSpec(
        num_scalar_prefetch=2,
        grid=(n_blocks,),
        in_specs=[
            pl.BlockSpec((ROW_BLOCK, D_MODEL), lambda i, b, u: (last(i, b, u), 0)),
            pl.BlockSpec(memory_space=pl.ANY),
            pl.BlockSpec((1, 1, 2 * D_FF), lambda i, b, u: (b[i], 0, 0)),
            pl.BlockSpec(memory_space=pl.ANY),
            pl.BlockSpec((1, 1, D_MODEL), lambda i, b, u: (b[i], 0, 0)),
        ],
        out_specs=pl.BlockSpec((ROW_BLOCK, D_MODEL), lambda i, b, u: (i, 0)),
        scratch_shapes=[
            pltpu.VMEM((2, D_MODEL, 2 * D_FF), jnp.float32),
            pltpu.VMEM((2, D_FF, D_MODEL), jnp.float32),
            pltpu.VMEM((D_MODEL, 2 * D_FF), jnp.bfloat16),
            pltpu.VMEM((D_FF, D_MODEL), jnp.bfloat16),
            pltpu.SMEM((2,), jnp.int32),
            pltpu.SemaphoreType.DMA((2, 2)),
        ])
    return pl.pallas_call(
        _expert_kernel,
        grid_spec=grid_spec,
        out_shape=jax.ShapeDtypeStruct((n_rows, D_MODEL), jnp.bfloat16),
        name="expert_ffn",
        compiler_params=pltpu.CompilerParams(dimension_semantics=("arbitrary",),
                                             vmem_limit_bytes=VMEM_LIMIT_BYTES),
    )(blk, used, rows, w1, b1.reshape(N_EXPERTS, 1, 2 * D_FF), w2, b2.reshape(N_EXPERTS, 1, D_MODEL))


def _combine_kernel(span_ref, soff_ref, src_ref, goff_ref, delta_ref,
                    x1_ref, pos_ref, gate_ref, y_ref, gfin_ref, out_ref,
                    stage, inflight, sems):
    t = pl.program_id(0)
    n_tiles = pl.num_programs(0)
    slot = t % 2

    def fetch(tile, slot_):
        def make_copy(off, src, n, sem):
            return pltpu.make_async_copy(y_ref.at[pl.ds(src, n)], stage.at[slot_, pl.ds(off, n)], sem)
        inflight[slot_] = _start_group_copies(tile, span_ref, soff_ref, src_ref, sems.at[slot_], make_copy,
                                              PACK_ROWS)

    @pl.when(t == 0)
    def _():
        stage[...] = jnp.zeros_like(stage)
        fetch(0, 0)

    @pl.when(t + 1 < n_tiles)
    def _():
        fetch(t + 1, 1 - slot)

    _wait_rows(y_ref, inflight[slot], sems.at[slot], PACK_ROWS)

    pos = pos_ref[...]
    shift = jnp.zeros_like(pos)
    prev = jnp.int32(0)
    for e in range(N_EXPERTS):
        d = delta_ref[e * LANES + t]
        shift = shift + jnp.where(pos >= goff_ref[e * LANES + t], d - prev, 0)
        prev = d
    pos = pos + shift
    gate = gate_ref[...]
    r = lax.broadcasted_iota(jnp.int32, (COMBINE_ROWS, pos.shape[1]), 0)
    w = jnp.zeros(r.shape, jnp.float32)
    for k in range(TOP_K):
        w = jnp.where(r == pos[k:k + 1], gate[k:k + 1], w)
    moe = lax.dot_general(w.astype(jnp.bfloat16), stage[slot],
                          (((0,), (0,)), ((), ())), preferred_element_type=jnp.float32)
    out_ref[...] = _rms(x1_ref[...] + moe, gfin_ref[...])


def _combine_call(span, soff, src, goff, delta, x1, pos, gate, y_rows, g_final):
    n_tok = x1.shape[0]
    ts = ROUTE_TILE
    grid_spec = pltpu.PrefetchScalarGridSpec(
        num_scalar_prefetch=5,
        grid=(n_tok // ts,),
        in_specs=[
            pl.BlockSpec((ts, D_MODEL), lambda t, *_: (t, 0)),
            pl.BlockSpec((TOP_K, ts), lambda t, *_: (0, t)),
            pl.BlockSpec((TOP_K, ts), lambda t, *_: (0, t)),
            pl.BlockSpec(memory_space=pl.ANY),
            pl.BlockSpec((1, D_MODEL), lambda t, *_: (0, 0)),
        ],
        out_specs=pl.BlockSpec((ts, D_MODEL), lambda t, *_: (t, 0)),
        scratch_shapes=[
            pltpu.VMEM((2, COMBINE_ROWS, D_MODEL), jnp.bfloat16),
            pltpu.SMEM((2,), jnp.int32),
            pltpu.SemaphoreType.DMA((2,)),
        ])
    return pl.pallas_call(
        _combine_kernel,
        grid_spec=grid_spec,
        out_shape=jax.ShapeDtypeStruct((n_tok, D_MODEL), jnp.float32),
        name="combine_norm",
        compiler_params=pltpu.CompilerParams(dimension_semantics=("arbitrary",),
                                             vmem_limit_bytes=VMEM_LIMIT_BYTES),
    )(span, soff, src, goff, delta, x1, pos, gate, y_rows, g_final)


def _attention_bias():
    slopes = jnp.exp2(-8.0 * jnp.arange(1, N_Q_HEADS + 1, dtype=jnp.float32) / N_Q_HEADS)
    qi = jnp.arange(BLOCK)[:, None]
    kj = jnp.arange(BAND)[None, :]
    dist = qi + BLOCK - kj
    valid = (dist >= 0) & (dist < WINDOW)
    b = -slopes[:, None, None] * dist.astype(jnp.float32)[None]
    b = jnp.where(valid[None], b, -jnp.inf)
    b = b.reshape(N_KV_HEADS, GROUP, BLOCK, BAND).transpose(0, 3, 1, 2)
    return b.reshape(N_KV_HEADS, BAND, GROUP * BLOCK)


def kernel(x, g_mix, w_in, conv_w, sinks, g_attn_out, g_conv_out, w_out, g_ffn, w_router, b_router,
           w1, b1, w2, b2, g_final):
    bsz, s_len, d = x.shape
    n_tok = bsz * s_len
    n_tiles = n_tok // ROUTE_TILE
    n_rows = n_tok * TOP_K + n_tiles * N_EXPERTS * SUBLANES + N_EXPERTS * ROW_BLOCK
    n_blocks = n_rows // ROW_BLOCK
    x2d = x.reshape(n_tok, d)
    bias = _attention_bias()
    l = 0
    wrt = w_router[l].T
    wr_hi = wrt.astype(jnp.bfloat16)
    wr_lo = (wrt - wr_hi.astype(jnp.float32)).astype(jnp.bfloat16)
    wr_split = jnp.concatenate([wr_hi, wr_lo], axis=0)
    x1, h2, pos, gate, cnt = _mixer_call(
        x2d, sinks[l], g_mix[l].reshape(1, d), w_in[l].astype(jnp.bfloat16), conv_w[l], bias,
        g_attn_out[l].reshape(1, ATTN_WIDTH), g_conv_out[l].reshape(1, CONV_WIDTH),
        w_out[l].astype(jnp.bfloat16), g_ffn[l].reshape(1, d), wr_split,
        b_router[l].reshape(N_EXPERTS, 1), s_len)
    size, goff, gdst, blk, used, span, soff, src, delta = _tables_call(cnt, n_blocks)
    size, goff, gdst = size.reshape(-1), goff.reshape(-1), gdst.reshape(-1)
    span, soff, src, delta = span.reshape(-1), soff.reshape(-1), src.reshape(-1), delta.reshape(-1)
    blk, used = blk.reshape(-1), used.reshape(-1)[:1]
    rows = _dispatch_call(size, goff, gdst, used, h2, pos, n_rows)
    y_rows = _expert_call(blk, used, rows, w1[l], b1[l], w2[l], b2[l])
    out = _combine_call(span, soff, src, goff, delta, x1, pos, gate, y_rows, g_final.reshape(1, d))
    return out.reshape(bsz, s_len, d)
```

```python
import functools
import math

import jax
import jax.numpy as jnp
from jax import lax
from jax.experimental import pallas as pl
from jax.experimental.pallas import tpu as pltpu

D_MODEL = 1024
N_Q_HEADS = 8
N_KV_HEADS = 2
HEAD_DIM = 64
GROUP = N_Q_HEADS // N_KV_HEADS
ATTN_WIDTH = N_Q_HEADS * HEAD_DIM
KV_WIDTH = N_KV_HEADS * HEAD_DIM
CONV_WIDTH = D_MODEL - ATTN_WIDTH
QKV_COLS = ATTN_WIDTH + 2 * KV_WIDTH
IN_COLS = QKV_COLS + 3 * CONV_WIDTH
WINDOW = 128
BLOCK = 128
BAND = 2 * BLOCK
N_EXPERTS = 32
TOP_K = 4
D_FF = D_MODEL
SWIGLU_ALPHA = 1.702
SWIGLU_LIMIT = 7.0
EPS = 1e-5

LANES = 128
SUBLANES = 8
VMEM_LIMIT_BYTES = 56 * 1024 * 1024

TOKEN_TILE = 1024
SUB_TILE = 512
ROUTE_TILE = 256
ROW_BLOCK = 512
STAGE_ROWS = -(-(TOP_K * ROUTE_TILE + N_EXPERTS * (SUBLANES - 1)) // LANES) * LANES
PACK_ROWS = 16
COMBINE_ROWS = -(-(TOP_K * ROUTE_TILE + N_EXPERTS * (SUBLANES - 1 + PACK_ROWS)) // LANES) * LANES


def _rms(x, g):
    return x * lax.rsqrt(jnp.mean(x * x, axis=-1, keepdims=True) + EPS) * g


def _excl_cumsum(a, axis):
    idx = lax.broadcasted_iota(jnp.int32, a.shape, axis)
    inc = a
    step = 1
    while step < a.shape[axis]:
        inc = inc + jnp.where(idx >= step, pltpu.roll(inc, step, axis), 0)
        step *= 2
    return inc - a


def _mixer_kernel(tiles_per_seq,
                  sink_ref,
                  x_ref, gmix_ref, win_ref, convw_ref, bias_ref, gattn_ref, gconv_ref, wout_ref,
                  gffn_ref, wr_ref, br_ref,
                  x1_ref, h2_ref, pos_ref, gate_ref, cnt_ref,
                  p_scr, h_scr, mix_scr, k_scr, vt_scr, u_scr, attn_t_scr, tri_scr):
    tm = x_ref.shape[0]
    sub_tm = SUB_TILE
    ts = ROUTE_TILE
    i = pl.program_id(0)
    first = (i % tiles_per_seq) == 0
    f32, bf16 = jnp.float32, jnp.bfloat16

    @pl.when(i == 0)
    def _():
        cnt_ref[...] = jnp.zeros_like(cnt_ref)
        r = lax.broadcasted_iota(jnp.int32, (ts, ts), 0)
        c = lax.broadcasted_iota(jnp.int32, (ts, ts), 1)
        tri_scr[...] = jnp.where(r < c, 1.0, 0.0).astype(bf16)

    @pl.when(first)
    def _():
        k_scr[0:BLOCK, :] = jnp.zeros((BLOCK, k_scr.shape[1]), bf16)
        vt_scr[:, 0:BLOCK] = jnp.zeros((KV_WIDTH, BLOCK), bf16)
        u_scr[0:SUBLANES, :] = jnp.zeros((SUBLANES, CONV_WIDTH), f32)

    n_sub = tm // sub_tm
    in_chunks = [(c, c + 2 * LANES) for c in range(0, IN_COLS, 2 * LANES)]
    out_chunks = [(c, c + 2 * LANES) for c in range(0, D_MODEL, 2 * LANES)]
    lane = lax.broadcasted_iota(jnp.int32, (BAND, LANES), 1)
    lo = lane < HEAD_DIM
    qgrp = lax.broadcasted_iota(jnp.int32, (BLOCK, 2 * LANES), 1) // HEAD_DIM
    key_row = lax.broadcasted_iota(jnp.int32, (BAND, GROUP * BLOCK), 0)
    scale = 1.0 / math.sqrt(HEAD_DIM)
    zero_q = jnp.zeros((BLOCK, 2 * LANES), bf16)
    nt = (((1,), (1,)), ((), ()))

    def rows_of(sub):
        return slice(sub * sub_tm, (sub + 1) * sub_tm)

    def prep(sub):
        r0 = rows_of(sub)
        h_scr[r0, :] = _rms(x_ref[r0, :], gmix_ref[...]).astype(bf16)

    def in_proj(sub, c0, c1):
        r0 = rows_of(sub)
        p_scr[r0, c0:c1] = jnp.dot(h_scr[r0, :], win_ref[:, c0:c1], preferred_element_type=f32)

    def kv_stage(sub):
        r0 = rows_of(sub)
        base = sub * sub_tm
        kk = p_scr[r0, ATTN_WIDTH:ATTN_WIDTH + KV_WIDTH]
        kr = pltpu.roll(kk, HEAD_DIM, axis=1)
        k_scr[BLOCK + base:BLOCK + base + sub_tm, :] = jnp.concatenate([kk, kr], axis=1).astype(bf16)
        vt_scr[:, BLOCK + base:BLOCK + base + sub_tm] = jnp.transpose(
            p_scr[r0, ATTN_WIDTH + KV_WIDTH:QKV_COLS]).astype(bf16)

    def attn_iter(j, hh):
        rows = slice(j * BLOCK, (j + 1) * BLOCK)
        kband = k_scr[j * BLOCK:j * BLOCK + BAND, :]
        bk, bkr = kband[:, 0:LANES], kband[:, LANES:2 * LANES]
        vt_band = vt_scr[:, j * BLOCK:j * BLOCK + BAND]
        k_both = jnp.where(lo, bk, bkr) if hh == 0 else jnp.where(lo, bkr, bk)
        kmat = jnp.concatenate([k_both, k_both], axis=1)
        qh = (p_scr[rows, hh * 2 * LANES:(hh + 1) * 2 * LANES] * scale).astype(bf16)
        qm = jnp.concatenate([jnp.where(qgrp == g, qh, zero_q) for g in range(GROUP)], axis=0)
        s = lax.dot_general(kmat, qm, nt, preferred_element_type=f32) + bias_ref[hh]
        if j == 0:
            s = jnp.where(jnp.logical_and(first, key_row < BLOCK), -jnp.inf, s)
        sink = jnp.concatenate(
            [jnp.full((1, BLOCK), sink_ref[hh * GROUP + g], f32) for g in range(GROUP)], axis=1)
        m = jnp.maximum(jnp.max(s, axis=0, keepdims=True), sink)
        e = jnp.exp(s - m)
        inv = 1.0 / (jnp.sum(e, axis=0, keepdims=True) + jnp.exp(sink - m))
        o_t = jnp.dot(vt_band, e.astype(bf16), preferred_element_type=f32)
        o_t = o_t[hh * HEAD_DIM:(hh + 1) * HEAD_DIM, :] * inv
        for g in range(GROUP):
            head = hh * GROUP + g
            attn_t_scr[head * HEAD_DIM:(head + 1) * HEAD_DIM, rows] = o_t[:, g * BLOCK:(g + 1) * BLOCK]

    def conv_mix(sub):
        r0 = rows_of(sub)
        base = sub * sub_tm
        bg = p_scr[r0, QKV_COLS:QKV_COLS + CONV_WIDTH]
        cg = p_scr[r0, QKV_COLS + CONV_WIDTH:QKV_COLS + 2 * CONV_WIDTH]
        xin = p_scr[r0, QKV_COLS + 2 * CONV_WIDTH:IN_COLS]
        u = cg * xin
        u_scr[SUBLANES + base:SUBLANES + base + sub_tm, :] = u
        u1 = u_scr[SUBLANES - 1 + base:SUBLANES - 1 + base + sub_tm, :]
        u2 = u_scr[SUBLANES - 2 + base:SUBLANES - 2 + base + sub_tm, :]
        y = convw_ref[0:1, :] * u2 + convw_ref[1:2, :] * u1 + convw_ref[2:3, :] * u
        conv = bg * y
        mix_scr[r0, :] = jnp.concatenate(
            [_rms(jnp.transpose(attn_t_scr[:, r0]), gattn_ref[...]), _rms(conv, gconv_ref[...])],
            axis=1).astype(bf16)

    def out_proj(sub, c0, c1):
        r0 = rows_of(sub)
        x1_ref[r0, c0:c1] = x_ref[r0, c0:c1] + jnp.dot(mix_scr[r0, :], wout_ref[:, c0:c1],
                                                      preferred_element_type=f32)

    def route(sub):
        r0 = rows_of(sub)
        h2 = _rms(x1_ref[r0, :], gffn_ref[...])
        h2_hi = h2.astype(bf16)
        h2_ref[r0, :] = h2_hi
        h2_lo = (h2 - h2_hi.astype(f32)).astype(bf16)
        by_hi = lax.dot_general(wr_ref[...], h2_hi, nt, preferred_element_type=f32)
        by_lo = lax.dot_general(wr_ref[0:N_EXPERTS, :], h2_lo, nt, preferred_element_type=f32)
        logits = by_hi[0:N_EXPERTS] + by_hi[N_EXPERTS:] + by_lo + br_ref[...]
        row = lax.broadcasted_iota(jnp.int32, (N_EXPERTS, sub_tm), 0)
        vals, idxs = [], []
        l = logits
        for _ in range(TOP_K):
            m = jnp.max(l, axis=0, keepdims=True)
            idx = jnp.min(jnp.where(l == m, row, N_EXPERTS), axis=0, keepdims=True)
            vals.append(m)
            idxs.append(idx)
            l = jnp.where(row == idx, -jnp.inf, l)
        es = [jnp.exp(v - vals[0]) for v in vals]
        inv = 1.0 / (es[0] + es[1] + es[2] + es[3])
        gate_ref[:, r0] = jnp.concatenate([e * inv for e in es], axis=0)

        onehots = [row == idx for idx in idxs]
        sel = jnp.logical_or(jnp.logical_or(onehots[0], onehots[1]), jnp.logical_or(onehots[2], onehots[3]))
        oh = jnp.where(sel, 1.0, 0.0)
        cnt_lane = lax.broadcasted_iota(jnp.int32, cnt_ref.shape, 1)
        poss = []
        for rt in range(sub_tm // ts):
            sl = slice(rt * ts, (rt + 1) * ts)
            oh_s = oh[:, sl]
            cum = jnp.dot(oh_s.astype(bf16), tri_scr[...], preferred_element_type=f32)
            cnt = jnp.sum(oh_s, axis=1, keepdims=True)
            padded = SUBLANES * jnp.floor((cnt + (SUBLANES - 1.0)) * (1.0 / SUBLANES))
            goff = _excl_cumsum(jnp.broadcast_to(padded, (N_EXPERTS, LANES)), 0)[:, 0:1]
            pos = goff + cum
            poss.append(jnp.concatenate(
                [jnp.sum(jnp.where(o[:, sl], pos, 0.0), axis=0, keepdims=True) for o in onehots], axis=0))
            tile_id = i * (tm // ts) + sub * (sub_tm // ts) + rt
            cnt_ref[...] = jnp.where(cnt_lane == tile_id, cnt.astype(jnp.int32), cnt_ref[...])
        pos_ref[:, r0] = jnp.concatenate(poss, axis=1).astype(jnp.int32)

    def interleave(main, fillers):
        after = [[] for _ in main]
        for n, filler in enumerate(fillers):
            after[n * len(main) // len(fillers)].append(filler)
        for stage, extra in zip(main, after):
            stage()
            for filler in extra:
                filler()

    def attention_stages(sub):
        blocks = range(sub * (sub_tm // BLOCK), (sub + 1) * (sub_tm // BLOCK))
        return [functools.partial(attn_iter, j, hh) for j in blocks for hh in range(N_KV_HEADS)]

    prep(0)
    for c0, c1 in in_chunks:
        in_proj(0, c0, c1)
    kv_stage(0)
    for sub in range(n_sub):
        fillers = []
        if sub > 0:
            fillers.append(functools.partial(conv_mix, sub - 1))
            fillers += [functools.partial(out_proj, sub - 1, c0, c1) for c0, c1 in out_chunks]
        if sub + 1 < n_sub:
            prep(sub + 1)
            fillers += [functools.partial(in_proj, sub + 1, c0, c1) for c0, c1 in in_chunks]
        interleave(attention_stages(sub), fillers)
        if sub + 1 < n_sub:
            kv_stage(sub + 1)
        if 0 < sub < n_sub - 1:
            route(sub - 1)
    last = n_sub - 1
    conv_mix(last)
    out_proj(last, *out_chunks[0])
    if last > 0:
        route(last - 1)
    for c0, c1 in out_chunks[1:]:
        out_proj(last, c0, c1)
    route(last)

    k_scr[0:BLOCK, :] = k_scr[tm:tm + BLOCK, :]
    vt_scr[:, 0:BLOCK] = vt_scr[:, tm:tm + BLOCK]
    u_scr[0:SUBLANES, :] = u_scr[tm:tm + SUBLANES, :]


def _mixer_call(x2d, sinks, g_mix, w_in, conv_w, bias, g_attn, g_conv, w_out, g_ffn, wr_split, br, seq_len):
    n_tok = x2d.shape[0]
    tm = TOKEN_TILE
    n_tiles = n_tok // tm
    assert n_tok // ROUTE_TILE == LANES, "one lane of the count table per routing tile"
    const = lambda shape: pl.BlockSpec(shape, lambda i, *_: (0,) * len(shape))
    grid_spec = pltpu.PrefetchScalarGridSpec(
        num_scalar_prefetch=0,
        grid=(n_tiles,),
        in_specs=[
            pl.BlockSpec(memory_space=pltpu.SMEM),
            pl.BlockSpec((tm, D_MODEL), lambda i: (i, 0)),
            const((1, D_MODEL)), const((D_MODEL, IN_COLS)), const((3, CONV_WIDTH)),
            const((N_KV_HEADS, BAND, GROUP * BLOCK)),
            const((1, ATTN_WIDTH)), const((1, CONV_WIDTH)), const((D_MODEL, D_MODEL)),
            const((1, D_MODEL)), const((2 * N_EXPERTS, D_MODEL)), const((N_EXPERTS, 1)),
        ],
        out_specs=[
            pl.BlockSpec((tm, D_MODEL), lambda i: (i, 0)),
            pl.BlockSpec((tm, D_MODEL), lambda i: (i, 0)),
            pl.BlockSpec((TOP_K, tm), lambda i: (0, i)),
            pl.BlockSpec((TOP_K, tm), lambda i: (0, i)),
            pl.BlockSpec((N_EXPERTS, LANES), lambda i: (0, 0)),
        ],
        scratch_shapes=[
            pltpu.VMEM((tm, IN_COLS), jnp.float32),
            pltpu.VMEM((tm, D_MODEL), jnp.bfloat16),
            pltpu.VMEM((tm, D_MODEL), jnp.bfloat16),
            pltpu.VMEM((BLOCK + tm, 2 * LANES), jnp.bfloat16),
            pltpu.VMEM((KV_WIDTH, BLOCK + tm), jnp.bfloat16),
            pltpu.VMEM((SUBLANES + tm, CONV_WIDTH), jnp.float32),
            pltpu.VMEM((ATTN_WIDTH, tm), jnp.float32),
            pltpu.VMEM((ROUTE_TILE, ROUTE_TILE), jnp.bfloat16),
        ])
    out_shape = [
        jax.ShapeDtypeStruct((n_tok, D_MODEL), jnp.float32),
        jax.ShapeDtypeStruct((n_tok, D_MODEL), jnp.bfloat16),
        jax.ShapeDtypeStruct((TOP_K, n_tok), jnp.int32),
        jax.ShapeDtypeStruct((TOP_K, n_tok), jnp.float32),
        jax.ShapeDtypeStruct((N_EXPERTS, LANES), jnp.int32),
    ]
    return pl.pallas_call(
        functools.partial(_mixer_kernel, seq_len // tm),
        grid_spec=grid_spec, out_shape=out_shape, name="mixer_router",
        compiler_params=pltpu.CompilerParams(dimension_semantics=("arbitrary",),
                                             vmem_limit_bytes=VMEM_LIMIT_BYTES),
    )(sinks, x2d, g_mix, w_in, conv_w, bias, g_attn, g_conv, w_out, g_ffn, wr_split, br)


def _tables_kernel(cnt_ref, size_ref, goff_ref, gdst_ref, blk_ref, used_ref,
                   span_ref, soff_ref, src_ref, delta_ref):
    size = ((cnt_ref[...] + (SUBLANES - 1)) // SUBLANES) * SUBLANES
    size_ref[...] = size
    goff = _excl_cumsum(size, 0)
    goff_ref[...] = goff
    total = jnp.broadcast_to(jnp.sum(size, axis=1, keepdims=True), size.shape)
    region = ((total + (ROW_BLOCK - 1)) // ROW_BLOCK) * ROW_BLOCK
    pstart = _excl_cumsum(region, 0)
    gdst = pstart + _excl_cumsum(size, 1)
    gdst_ref[...] = gdst
    src = (gdst // PACK_ROWS) * PACK_ROWS
    span = jnp.where(size > 0, ((gdst + size + (PACK_ROWS - 1)) // PACK_ROWS) * PACK_ROWS - src, 0)
    soff = _excl_cumsum(span, 0)
    span_ref[...] = span
    soff_ref[...] = soff
    src_ref[...] = src
    delta_ref[...] = soff + (gdst - src) - goff
    pend = (pstart + region)[:, 0:1]
    start = lax.broadcasted_iota(jnp.int32, (N_EXPERTS, blk_ref.shape[1]), 1) * ROW_BLOCK
    owner = jnp.sum(jnp.where(pend <= start, 1, 0), axis=0, keepdims=True)
    blk_ref[...] = jnp.minimum(owner, N_EXPERTS - 1)
    used_ref[...] = jnp.broadcast_to(jnp.max(pend, axis=0, keepdims=True) // ROW_BLOCK, used_ref.shape)


def _tables_call(cnt, n_blocks):
    nb_pad = -(-n_blocks // LANES) * LANES
    tbl = jax.ShapeDtypeStruct((N_EXPERTS, LANES), jnp.int32)
    return pl.pallas_call(
        _tables_kernel,
        out_shape=[tbl, tbl, tbl, jax.ShapeDtypeStruct((1, nb_pad), jnp.int32),
                   jax.ShapeDtypeStruct((1, LANES), jnp.int32), tbl, tbl, tbl, tbl],
        name="group_tables",
    )(cnt)


def _start_group_copies(tile, size_ref, goff_ref, gdst_ref, sem, make_copy, align):
    def body(e, rows):
        n = pl.multiple_of(size_ref[e * LANES + tile], align)
        @pl.when(n > 0)
        def _():
            off = pl.multiple_of(goff_ref[e * LANES + tile], align)
            dst = pl.multiple_of(gdst_ref[e * LANES + tile], align)
            make_copy(off, dst, n, sem).start()
        return rows + n
    return lax.fori_loop(0, N_EXPERTS, body, jnp.int32(0))


def _wait_rows(rows_hbm, n, sem, align):
    @pl.when(n > 0)
    def _():
        m = pl.multiple_of(n, align)
        half = rows_hbm.shape[0] // (2 * align) * align
        pltpu.make_async_copy(rows_hbm.at[pl.ds(0, m)], rows_hbm.at[pl.ds(half, m)], sem).wait()


def _dispatch_kernel(n_blocks,
                     size_ref, goff_ref, gdst_ref, used_ref,
                     h2_ref, pos_ref, rows_ref,
                     stage, zbuf, inflight, sems, zsem):
    t = pl.program_id(0)
    n_tiles = pl.num_programs(0)
    slot = t % 2

    @pl.when(t == 0)
    def _():
        zbuf[...] = jnp.zeros_like(zbuf)
        def zero_block(start):
            pltpu.make_async_copy(
                zbuf, rows_ref.at[pl.ds(pl.multiple_of(start, ROW_BLOCK), ROW_BLOCK)], zsem).start()
        def zfill(e, n):
            first = gdst_ref[e * LANES]
            last = gdst_ref[e * LANES + LANES - 1] + size_ref[e * LANES + LANES - 1]
            @pl.when(last > first)
            def _():
                zero_block((last + ROW_BLOCK - 1) // ROW_BLOCK * ROW_BLOCK - ROW_BLOCK)
            return n + jnp.where(last > first, ROW_BLOCK, 0)
        n_zero = lax.fori_loop(0, N_EXPERTS, zfill, jnp.int32(0))
        def ztail(b, _):
            zero_block(b * ROW_BLOCK)
            return 0
        lax.fori_loop(used_ref[0], n_blocks, ztail, 0)
        _wait_rows(rows_ref, n_zero + (n_blocks - used_ref[0]) * ROW_BLOCK, zsem, SUBLANES)

    @pl.when(t >= 2)
    def _():
        _wait_rows(rows_ref, inflight[slot], sems.at[slot], SUBLANES)

    pos = pos_ref[...]
    r = lax.broadcasted_iota(jnp.int32, (STAGE_ROWS, pos.shape[1]), 0)
    hit = jnp.logical_or(jnp.logical_or(r == pos[0:1], r == pos[1:2]),
                         jnp.logical_or(r == pos[2:3], r == pos[3:4]))
    perm = jnp.where(hit, 1.0, 0.0).astype(jnp.bfloat16)
    stage[slot] = jnp.dot(perm, h2_ref[...], preferred_element_type=jnp.float32)

    def make_copy(off, dst, n, sem):
        return pltpu.make_async_copy(stage.at[slot, pl.ds(off, n)], rows_ref.at[pl.ds(dst, n)], sem)
    inflight[slot] = _start_group_copies(t, size_ref, goff_ref, gdst_ref, sems.at[slot], make_copy, SUBLANES)

    @pl.when(t == n_tiles - 1)
    def _():
        _wait_rows(rows_ref, inflight[1 - slot], sems.at[1 - slot], SUBLANES)
        _wait_rows(rows_ref, inflight[slot], sems.at[slot], SUBLANES)


def _dispatch_call(size, goff, gdst, used, h2, pos, n_rows):
    n_tok = h2.shape[0]
    ts = ROUTE_TILE
    n_blocks = n_rows // ROW_BLOCK
    grid_spec = pltpu.PrefetchScalarGridSpec(
        num_scalar_prefetch=4,
        grid=(n_tok // ts,),
        in_specs=[
            pl.BlockSpec((ts, D_MODEL), lambda t, *_: (t, 0)),
            pl.BlockSpec((TOP_K, ts), lambda t, *_: (0, t)),
        ],
        out_specs=pl.BlockSpec(memory_space=pl.ANY),
        scratch_shapes=[
            pltpu.VMEM((2, STAGE_ROWS, D_MODEL), jnp.float32),
            pltpu.VMEM((ROW_BLOCK, D_MODEL), jnp.float32),
            pltpu.SMEM((2,), jnp.int32),
            pltpu.SemaphoreType.DMA((2,)),
            pltpu.SemaphoreType.DMA(()),
        ])
    return pl.pallas_call(
        functools.partial(_dispatch_kernel, n_blocks),
        grid_spec=grid_spec,
        out_shape=jax.ShapeDtypeStruct((n_rows, D_MODEL), jnp.float32),
        name="dispatch_rows",
        compiler_params=pltpu.CompilerParams(dimension_semantics=("arbitrary",),
                                             vmem_limit_bytes=VMEM_LIMIT_BYTES),
    )(size, goff, gdst, used, h2, pos)


def _expert_kernel(blk_ref, used_ref, x_ref, w1_hbm, b1_ref, w2_hbm, b2_ref, y_ref,
                   w1_f32, w2_f32, w1_bf, w2_bf, state, sems):
    i = pl.program_id(0)
    bf16 = jnp.bfloat16
    e = blk_ref[i]
    used = used_ref[0]

    def weight_copies(expert, slot):
        return (pltpu.make_async_copy(w1_hbm.at[expert], w1_f32.at[slot], sems.at[0, slot]),
                pltpu.make_async_copy(w2_hbm.at[expert], w2_f32.at[slot], sems.at[1, slot]))

    @pl.when(i == 0)
    def _():
        state[0] = jnp.int32(-1)
        state[1] = jnp.int32(0)
        @pl.when(used > 0)
        def _():
            for cp in weight_copies(e, 0):
                cp.start()

    @pl.when(jnp.logical_and(i < used, e != state[0]))
    def _():
        slot = state[1]
        for cp in weight_copies(e, slot):
            cp.wait()
        w1_bf[...] = w1_f32[slot].astype(bf16)
        w2_bf[...] = w2_f32[slot].astype(bf16)
        nxt = lax.while_loop(lambda j: jnp.logical_and(j < used, blk_ref[jnp.minimum(j, used - 1)] == e),
                             lambda j: j + 1, i + 1)
        @pl.when(nxt < used)
        def _():
            for cp in weight_copies(blk_ref[nxt], 1 - slot):
                cp.start()
        state[0] = e
        state[1] = 1 - slot

    @pl.when(i >= used)
    def _():
        y_ref[...] = jnp.zeros_like(y_ref)

    @pl.when(i < used)
    def _():
        hu = jnp.dot(x_ref[...].astype(bf16), w1_bf[...], preferred_element_type=jnp.float32) + b1_ref[0]
        glu = jnp.minimum(hu[:, :D_FF], SWIGLU_LIMIT)
        lin = jnp.clip(hu[:, D_FF:], -SWIGLU_LIMIT, SWIGLU_LIMIT)
        act = glu * (1.0 / (1.0 + jnp.exp(-SWIGLU_ALPHA * glu))) * (lin + 1.0)
        y = jnp.dot(act.astype(bf16), w2_bf[...], preferred_element_type=jnp.float32) + b2_ref[0]
        y_ref[...] = y.astype(bf16)


def _expert_call(blk, used, rows, w1, b1, w2, b2):
    n_rows = rows.shape[0]
    n_blocks = n_rows // ROW_BLOCK
    last = lambda i, b, u: jnp.maximum(jnp.minimum(i, u[0] - 1), 0)
    grid_spec = pltpu.PrefetchScalarGridSpec(
        num_scalar_prefetch=2,
        grid=(n_blocks,),
        in_specs=[
            pl.BlockSpec((ROW_BLOCK, D_MODEL), lambda i, b, u: (last(i, b, u), 0)),
            pl.BlockSpec(memory_space=pl.ANY),
            pl.BlockSpec((1, 1, 2 * D_FF), lambda i, b, u: (b[i], 0, 0)),
            pl.BlockSpec(memory_space=pl.ANY),
            pl.BlockSpec((1, 1, D_MODEL), lambda i, b, u: (b[i], 0, 0)),
        ],
        out_specs=pl.BlockSpec((ROW_BLOCK, D_MODEL), lambda i, b, u: (i, 0)),
        scratch_shapes=[
            pltpu.VMEM((2, D_MODEL, 2 * D_FF), jnp.float32),
            pltpu.VMEM((2, D_FF, D_MODEL), jnp.float32),
            pltpu.VMEM((D_MODEL, 2 * D_FF), jnp.bfloat16),
            pltpu.VMEM((D_FF, D_MODEL), jnp.bfloat16),
            pltpu.SMEM((2,), jnp.int32),
            pltpu.SemaphoreType.DMA((2, 2)),
        ])
    return pl.pallas_call(
        _expert_kernel,
        grid_spec=grid_spec,
        out_shape=jax.ShapeDtypeStruct((n_rows, D_MODEL), jnp.bfloat16),
        name="expert_ffn",
        compiler_params=pltpu.CompilerParams(dimension_semantics=("arbitrary",),
                                             vmem_limit_bytes=VMEM_LIMIT_BYTES),
    )(blk, used, rows, w1, b1.reshape(N_EXPERTS, 1, 2 * D_FF), w2, b2.reshape(N_EXPERTS, 1, D_MODEL))


def _combine_kernel(span_ref, soff_ref, src_ref, goff_ref, delta_ref,
                    x1_ref, pos_ref, gate_ref, y_ref, gfin_ref, out_ref,
                    stage, inflight, sems):
    t = pl.program_id(0)
    n_tiles = pl.num_programs(0)
    slot = t % 2

    def fetch(tile, slot_):
        def make_copy(off, src, n, sem):
            return pltpu.make_async_copy(y_ref.at[pl.ds(src, n)], stage.at[slot_, pl.ds(off, n)], sem)
        inflight[slot_] = _start_group_copies(tile, span_ref, soff_ref, src_ref, sems.at[slot_], make_copy,
                                              PACK_ROWS)

    @pl.when(t == 0)
    def _():
        stage[...] = jnp.zeros_like(stage)
        fetch(0, 0)

    @pl.when(t + 1 < n_tiles)
    def _():
        fetch(t + 1, 1 - slot)

    _wait_rows(y_ref, inflight[slot], sems.at[slot], PACK_ROWS)

    pos = pos_ref[...]
    shift = jnp.zeros_like(pos)
    prev = jnp.int32(0)
    for e in range(N_EXPERTS):
        d = delta_ref[e * LANES + t]
        shift = shift + jnp.where(pos >= goff_ref[e * LANES + t], d - prev, 0)
        prev = d
    pos = pos + shift
    gate = gate_ref[...]
    r = lax.broadcasted_iota(jnp.int32, (COMBINE_ROWS, pos.shape[1]), 0)
    w = jnp.zeros(r.shape, jnp.float32)
    for k in range(TOP_K):
        w = jnp.where(r == pos[k:k + 1], gate[k:k + 1], w)
    moe = lax.dot_general(w.astype(jnp.bfloat16), stage[slot],
                          (((0,), (0,)), ((), ())), preferred_element_type=jnp.float32)
    out_ref[...] = _rms(x1_ref[...] + moe, gfin_ref[...])


def _combine_call(span, soff, src, goff, delta, x1, pos, gate, y_rows, g_final):
    n_tok = x1.shape[0]
    ts = ROUTE_TILE
    grid_spec = pltpu.PrefetchScalarGridSpec(
        num_scalar_prefetch=5,
        grid=(n_tok // ts,),
        in_specs=[
            pl.BlockSpec((ts, D_MODEL), lambda t, *_: (t, 0)),
            pl.BlockSpec((TOP_K, ts), lambda t, *_: (0, t)),
            pl.BlockSpec((TOP_K, ts), lambda t, *_: (0, t)),
            pl.BlockSpec(memory_space=pl.ANY),
            pl.BlockSpec((1, D_MODEL), lambda t, *_: (0, 0)),
        ],
        out_specs=pl.BlockSpec((ts, D_MODEL), lambda t, *_: (t, 0)),
        scratch_shapes=[
            pltpu.VMEM((2, COMBINE_ROWS, D_MODEL), jnp.bfloat16),
            pltpu.SMEM((2,), jnp.int32),
            pltpu.SemaphoreType.DMA((2,)),
        ])
    return pl.pallas_call(
        _combine_kernel,
        grid_spec=grid_spec,
        out_shape=jax.ShapeDtypeStruct((n_tok, D_MODEL), jnp.float32),
        name="combine_norm",
        compiler_params=pltpu.CompilerParams(dimension_semantics=("arbitrary",),
                                             vmem_limit_bytes=VMEM_LIMIT_BYTES),
    )(span, soff, src, goff, delta, x1, pos, gate, y_rows, g_final)


def _attention_bias():
    slopes = jnp.exp2(-8.0 * jnp.arange(1, N_Q_HEADS + 1, dtype=jnp.float32) / N_Q_HEADS)
    qi = jnp.arange(BLOCK)[:, None]
    kj = jnp.arange(BAND)[None, :]
    dist = qi + BLOCK - kj
    valid = (dist >= 0) & (dist < WINDOW)
    b = -slopes[:, None, None] * dist.astype(jnp.float32)[None]
    b = jnp.where(valid[None], b, -jnp.inf)
    b = b.reshape(N_KV_HEADS, GROUP, BLOCK, BAND).transpose(0, 3, 1, 2)
    return b.reshape(N_KV_HEADS, BAND, GROUP * BLOCK)


def kernel(x, g_mix, w_in, conv_w, sinks, g_attn_out, g_conv_out, w_out, g_ffn, w_router, b_router,
           w1, b1, w2, b2, g_final):
    bsz, s_len, d = x.shape
    n_tok = bsz * s_len
    n_tiles = n_tok // ROUTE_TILE
    n_rows = n_tok * TOP_K + n_tiles * N_EXPERTS * SUBLANES + N_EXPERTS * ROW_BLOCK
    n_blocks = n_rows // ROW_BLOCK
    x2d = x.reshape(n_tok, d)
    bias = _attention_bias()
    l = 0
    wrt = w_router[l].T
    wr_hi = wrt.astype(jnp.bfloat16)
    wr_lo = (wrt - wr_hi.astype(jnp.float32)).astype(jnp.bfloat16)
    wr_split = jnp.concatenate([wr_hi, wr_lo], axis=0)
    x1, h2, pos, gate, cnt = _mixer_call(
        x2d, sinks[l], g_mix[l].reshape(1, d), w_in[l].astype(jnp.bfloat16), conv_w[l], bias,
        g_attn_out[l].reshape(1, ATTN_WIDTH), g_conv_out[l].reshape(1, CONV_WIDTH),
        w_out[l].astype(jnp.bfloat16), g_ffn[l].reshape(1, d), wr_split,
        b_router[l].reshape(N_EXPERTS, 1), s_len)
    size, goff, gdst, blk, used, span, soff, src, delta = _tables_call(cnt, n_blocks)
    size, goff, gdst = size.reshape(-1), goff.reshape(-1), gdst.reshape(-1)
    span, soff, src, delta = span.reshape(-1), soff.reshape(-1), src.reshape(-1), delta.reshape(-1)
    blk, used = blk.reshape(-1), used.reshape(-1)[:1]
    rows = _dispatch_call(size, goff, gdst, used, h2, pos, n_rows)
    y_rows = _expert_call(blk, used, rows, w1[l], b1[l], w2[l], b2[l])
    out = _combine_call(span, soff, src, goff, delta, x1, pos, gate, y_rows, g_final.reshape(1, d))
    return out.reshape(bsz, s_len, d)
```

```python
import functools
import math

import jax
import jax.numpy as jnp
from jax import lax
from jax.experimental import pallas as pl
from jax.experimental.pallas import tpu as pltpu

D_MODEL = 1024
N_Q_HEADS = 8
N_KV_HEADS = 2
HEAD_DIM = 64
GROUP = N_Q_HEADS // N_KV_HEADS
ATTN_WIDTH = N_Q_HEADS * HEAD_DIM
KV_WIDTH = N_KV_HEADS * HEAD_DIM
CONV_WIDTH = D_MODEL - ATTN_WIDTH
QKV_COLS = ATTN_WIDTH + 2 * KV_WIDTH
IN_COLS = QKV_COLS + 3 * CONV_WIDTH
WINDOW = 128
BLOCK = 128
BAND = 2 * BLOCK
N_EXPERTS = 32
TOP_K = 4
D_FF = D_MODEL
SWIGLU_ALPHA = 1.702
SWIGLU_LIMIT = 7.0
EPS = 1e-5

LANES = 128
SUBLANES = 8
VMEM_LIMIT_BYTES = 56 * 1024 * 1024

TOKEN_TILE = 1024
SUB_TILE = 512
ROUTE_TILE = 256
ROW_BLOCK = 512
STAGE_ROWS = -(-(TOP_K * ROUTE_TILE + N_EXPERTS * (SUBLANES - 1)) // LANES) * LANES
PACK_ROWS = 16
COMBINE_ROWS = -(-(TOP_K * ROUTE_TILE + N_EXPERTS * (SUBLANES - 1 + PACK_ROWS)) // LANES) * LANES


def _rms(x, g):
    return x * lax.rsqrt(jnp.mean(x * x, axis=-1, keepdims=True) + EPS) * g


def _excl_cumsum(a, axis):
    idx = lax.broadcasted_iota(jnp.int32, a.shape, axis)
    inc = a
    step = 1
    while step < a.shape[axis]:
        inc = inc + jnp.where(idx >= step, pltpu.roll(inc, step, axis), 0)
        step *= 2
    return inc - a


def _mixer_kernel(tiles_per_seq,
                  sink_ref,
                  x_ref, gmix_ref, win_ref, convw_ref, bias_ref, gattn_ref, gconv_ref, wout_ref,
                  gffn_ref, wr_ref, br_ref,
                  x1_ref, h2_ref, pos_ref, gate_ref, cnt_ref,
                  p_scr, h_scr, mix_scr, k_scr, vt_scr, u_scr, attn_t_scr, tri_scr):
    tm = x_ref.shape[0]
    sub_tm = SUB_TILE
    ts = ROUTE_TILE
    i = pl.program_id(0)
    first = (i % tiles_per_seq) == 0
    f32, bf16 = jnp.float32, jnp.bfloat16

    @pl.when(i == 0)
    def _():
        cnt_ref[...] = jnp.zeros_like(cnt_ref)
        r = lax.broadcasted_iota(jnp.int32, (ts, ts), 0)
        c = lax.broadcasted_iota(jnp.int32, (ts, ts), 1)
        tri_scr[...] = jnp.where(r < c, 1.0, 0.0).astype(bf16)

    @pl.when(first)
    def _():
        k_scr[0:BLOCK, :] = jnp.zeros((BLOCK, k_scr.shape[1]), bf16)
        vt_scr[:, 0:BLOCK] = jnp.zeros((KV_WIDTH, BLOCK), bf16)
        u_scr[0:SUBLANES, :] = jnp.zeros((SUBLANES, CONV_WIDTH), f32)

    n_sub = tm // sub_tm
    in_chunks = [(c, c + 2 * LANES) for c in range(0, IN_COLS, 2 * LANES)]
    out_chunks = [(c, c + 2 * LANES) for c in range(0, D_MODEL, 2 * LANES)]
    lane = lax.broadcasted_iota(jnp.int32, (BAND, LANES), 1)
    lo = lane < HEAD_DIM
    qgrp = lax.broadcasted_iota(jnp.int32, (BLOCK, 2 * LANES), 1) // HEAD_DIM
    key_row = lax.broadcasted_iota(jnp.int32, (BAND, GROUP * BLOCK), 0)
    scale = 1.0 / math.sqrt(HEAD_DIM)
    zero_q = jnp.zeros((BLOCK, 2 * LANES), bf16)
    nt = (((1,), (1,)), ((), ()))

    def rows_of(sub):
        return slice(sub * sub_tm, (sub + 1) * sub_tm)

    def prep(sub):
        r0 = rows_of(sub)
        h_scr[r0, :] = _rms(x_ref[r0, :], gmix_ref[...]).astype(bf16)

    def in_proj(sub, c0, c1):
        r0 = rows_of(sub)
        p_scr[r0, c0:c1] = jnp.dot(h_scr[r0, :], win_ref[:, c0:c1], preferred_element_type=f32)

    def kv_stage(sub):
        r0 = rows_of(sub)
        base = sub * sub_tm
        kk = p_scr[r0, ATTN_WIDTH:ATTN_WIDTH + KV_WIDTH]
        kr = pltpu.roll(kk, HEAD_DIM, axis=1)
        k_scr[BLOCK + base:BLOCK + base + sub_tm, :] = jnp.concatenate([kk, kr], axis=1).astype(bf16)
        vt_scr[:, BLOCK + base:BLOCK + base + sub_tm] = jnp.transpose(
            p_scr[r0, ATTN_WIDTH + KV_WIDTH:QKV_COLS]).astype(bf16)

    def attn_iter(j, hh):
        rows = slice(j * BLOCK, (j + 1) * BLOCK)
        kband = k_scr[j * BLOCK:j * BLOCK + BAND, :]
        bk, bkr = kband[:, 0:LANES], kband[:, LANES:2 * LANES]
        vt_band = vt_scr[:, j * BLOCK:j * BLOCK + BAND]
        k_both = jnp.where(lo, bk, bkr) if hh == 0 else jnp.where(lo, bkr, bk)
        kmat = jnp.concatenate([k_both, k_both], axis=1)
        qh = (p_scr[rows, hh * 2 * LANES:(hh + 1) * 2 * LANES] * scale).astype(bf16)
        qm = jnp.concatenate([jnp.where(qgrp == g, qh, zero_q) for g in range(GROUP)], axis=0)
        s = lax.dot_general(kmat, qm, nt, preferred_element_type=f32) + bias_ref[hh]
        if j == 0:
            s = jnp.where(jnp.logical_and(first, key_row < BLOCK), -jnp.inf, s)
        sink = jnp.concatenate(
            [jnp.full((1, BLOCK), sink_ref[hh * GROUP + g], f32) for g in range(GROUP)], axis=1)
        m = jnp.maximum(jnp.max(s, axis=0, keepdims=True), sink)
        e = jnp.exp(s - m)
        inv = 1.0 / (jnp.sum(e, axis=0, keepdims=True) + jnp.exp(sink - m))
        o_t = jnp.dot(vt_band, e.astype(bf16), preferred_element_type=f32)
        o_t = o_t[hh * HEAD_DIM:(hh + 1) * HEAD_DIM, :] * inv
        for g in range(GROUP):
            head = hh * GROUP + g
            attn_t_scr[head * HEAD_DIM:(head + 1) * HEAD_DIM, rows] = o_t[:, g * BLOCK:(g + 1) * BLOCK]

    def conv_mix(sub):
        r0 = rows_of(sub)
        base = sub * sub_tm
        bg = p_scr[r0, QKV_COLS:QKV_COLS + CONV_WIDTH]
        cg = p_scr[r0, QKV_COLS + CONV_WIDTH:QKV_COLS + 2 * CONV_WIDTH]
        xin = p_scr[r0, QKV_COLS + 2 * CONV_WIDTH:IN_COLS]
        u = cg * xin
        u_scr[SUBLANES + base:SUBLANES + base + sub_tm, :] = u
        u1 = u_scr[SUBLANES - 1 + base:SUBLANES - 1 + base + sub_tm, :]
        u2 = u_scr[SUBLANES - 2 + base:SUBLANES - 2 + base + sub_tm, :]
        y = convw_ref[0:1, :] * u2 + convw_ref[1:2, :] * u1 + convw_ref[2:3, :] * u
        conv = bg * y
        mix_scr[r0, :] = jnp.concatenate(
            [_rms(jnp.transpose(attn_t_scr[:, r0]), gattn_ref[...]), _rms(conv, gconv_ref[...])],
            axis=1).astype(bf16)

    def out_proj(sub, c0, c1):
        r0 = rows_of(sub)
        x1_ref[r0, c0:c1] = x_ref[r0, c0:c1] + jnp.dot(mix_scr[r0, :], wout_ref[:, c0:c1],
                                                      preferred_element_type=f32)

    def route(sub):
        r0 = rows_of(sub)
        h2 = _rms(x1_ref[r0, :], gffn_ref[...])
        h2_hi = h2.astype(bf16)
        h2_ref[r0, :] = h2_hi
        h2_lo = (h2 - h2_hi.astype(f32)).astype(bf16)
        by_hi = lax.dot_general(wr_ref[...], h2_hi, nt, preferred_element_type=f32)
        by_lo = lax.dot_general(wr_ref[0:N_EXPERTS, :], h2_lo, nt, preferred_element_type=f32)
        logits = by_hi[0:N_EXPERTS] + by_hi[N_EXPERTS:] + by_lo + br_ref[...]
        row = lax.broadcasted_iota(jnp.int32, (N_EXPERTS, sub_tm), 0)
        vals, idxs = [], []
        l = logits
        for _ in range(TOP_K):
            m = jnp.max(l, axis=0, keepdims=True)
            idx = jnp.min(jnp.where(l == m, row, N_EXPERTS), axis=0, keepdims=True)
            vals.append(m)
            idxs.append(idx)
            l = jnp.where(row == idx, -jnp.inf, l)
        es = [jnp.exp(v - vals[0]) for v in vals]
        inv = 1.0 / (es[0] + es[1] + es[2] + es[3])
        gate_ref[:, r0] = jnp.concatenate([e * inv for e in es], axis=0)

        onehots = [row == idx for idx in idxs]
        sel = jnp.logical_or(jnp.logical_or(onehots[0], onehots[1]), jnp.logical_or(onehots[2], onehots[3]))
        oh = jnp.where(sel, 1.0, 0.0)
        cnt_lane = lax.broadcasted_iota(jnp.int32, cnt_ref.shape, 1)
        poss = []
        for rt in range(sub_tm // ts):
            sl = slice(rt * ts, (rt + 1) * ts)
            oh_s = oh[:, sl]
            cum = jnp.dot(oh_s.astype(bf16), tri_scr[...], preferred_element_type=f32)
            cnt = jnp.sum(oh_s, axis=1, keepdims=True)
            padded = SUBLANES * jnp.floor((cnt + (SUBLANES - 1.0)) * (1.0 / SUBLANES))
            goff = _excl_cumsum(jnp.broadcast_to(padded, (N_EXPERTS, LANES)), 0)[:, 0:1]
            pos = goff + cum
            poss.append(jnp.concatenate(
                [jnp.sum(jnp.where(o[:, sl], pos, 0.0), axis=0, keepdims=True) for o in onehots], axis=0))
            tile_id = i * (tm // ts) + sub * (sub_tm // ts) + rt
            cnt_ref[...] = jnp.where(cnt_lane == tile_id, cnt.astype(jnp.int32), cnt_ref[...])
        pos_ref[:, r0] = jnp.concatenate(poss, axis=1).astype(jnp.int32)

    def interleave(main, fillers):
        after = [[] for _ in main]
        for n, filler in enumerate(fillers):
            after[n * len(main) // len(fillers)].append(filler)
        for stage, extra in zip(main, after):
            stage()
            for filler in extra:
                filler()

    def attention_stages(sub):
        blocks = range(sub * (sub_tm // BLOCK), (sub + 1) * (sub_tm // BLOCK))
        return [functools.partial(attn_iter, j, hh) for j in blocks for hh in range(N_KV_HEADS)]

    prep(0)
    for c0, c1 in in_chunks:
        in_proj(0, c0, c1)
    kv_stage(0)
    for sub in range(n_sub):
        fillers = []
        if sub > 0:
            fillers.append(functools.partial(conv_mix, sub - 1))
            fillers += [functools.partial(out_proj, sub - 1, c0, c1) for c0, c1 in out_chunks]
        if sub + 1 < n_sub:
            prep(sub + 1)
            fillers += [functools.partial(in_proj, sub + 1, c0, c1) for c0, c1 in in_chunks]
        interleave(attention_stages(sub), fillers)
        if sub + 1 < n_sub:
            kv_stage(sub + 1)
        if 0 < sub < n_sub - 1:
            route(sub - 1)
    last = n_sub - 1
    conv_mix(last)
    out_proj(last, *out_chunks[0])
    if last > 0:
        route(last - 1)
    for c0, c1 in out_chunks[1:]:
        out_proj(last, c0, c1)
    route(last)

    k_scr[0:BLOCK, :] = k_scr[tm:tm + BLOCK, :]
    vt_scr[:, 0:BLOCK] = vt_scr[:, tm:tm + BLOCK]
    u_scr[0:SUBLANES, :] = u_scr[tm:tm + SUBLANES, :]


def _mixer_call(x2d, sinks, g_mix, w_in, conv_w, bias, g_attn, g_conv, w_out, g_ffn, wr_split, br, seq_len):
    n_tok = x2d.shape[0]
    tm = TOKEN_TILE
    n_tiles = n_tok // tm
    assert n_tok // ROUTE_TILE == LANES, "one lane of the count table per routing tile"
    const = lambda shape: pl.BlockSpec(shape, lambda i, *_: (0,) * len(shape))
    grid_spec = pltpu.PrefetchScalarGridSpec(
        num_scalar_prefetch=0,
        grid=(n_tiles,),
        in_specs=[
            pl.BlockSpec(memory_space=pltpu.SMEM),
            pl.BlockSpec((tm, D_MODEL), lambda i: (i, 0)),
            const((1, D_MODEL)), const((D_MODEL, IN_COLS)), const((3, CONV_WIDTH)),
            const((N_KV_HEADS, BAND, GROUP * BLOCK)),
            const((1, ATTN_WIDTH)), const((1, CONV_WIDTH)), const((D_MODEL, D_MODEL)),
            const((1, D_MODEL)), const((2 * N_EXPERTS, D_MODEL)), const((N_EXPERTS, 1)),
        ],
        out_specs=[
            pl.BlockSpec((tm, D_MODEL), lambda i: (i, 0)),
            pl.BlockSpec((tm, D_MODEL), lambda i: (i, 0)),
            pl.BlockSpec((TOP_K, tm), lambda i: (0, i)),
            pl.BlockSpec((TOP_K, tm), lambda i: (0, i)),
            pl.BlockSpec((N_EXPERTS, LANES), lambda i: (0, 0)),
        ],
        scratch_shapes=[
            pltpu.VMEM((tm, IN_COLS), jnp.float32),
            pltpu.VMEM((tm, D_MODEL), jnp.bfloat16),
            pltpu.VMEM((tm, D_MODEL), jnp.bfloat16),
            pltpu.VMEM((BLOCK + tm, 2 * LANES), jnp.bfloat16),
            pltpu.VMEM((KV_WIDTH, BLOCK + tm), jnp.bfloat16),
            pltpu.VMEM((SUBLANES + tm, CONV_WIDTH), jnp.float32),
            pltpu.VMEM((ATTN_WIDTH, tm), jnp.float32),
            pltpu.VMEM((ROUTE_TILE, ROUTE_TILE), jnp.bfloat16),
        ])
    out_shape = [
        jax.ShapeDtypeStruct((n_tok, D_MODEL), jnp.float32),
        jax.ShapeDtypeStruct((n_tok, D_MODEL), jnp.bfloat16),
        jax.ShapeDtypeStruct((TOP_K, n_tok), jnp.int32),
        jax.ShapeDtypeStruct((TOP_K, n_tok), jnp.float32),
        jax.ShapeDtypeStruct((N_EXPERTS, LANES), jnp.int32),
    ]
    return pl.pallas_call(
        functools.partial(_mixer_kernel, seq_len // tm),
        grid_spec=grid_spec, out_shape=out_shape, name="mixer_router",
        compiler_params=pltpu.CompilerParams(dimension_semantics=("arbitrary",),
                                             vmem_limit_bytes=VMEM_LIMIT_BYTES),
    )(sinks, x2d, g_mix, w_in, conv_w, bias, g_attn, g_conv, w_out, g_ffn, wr_split, br)


def _tables_kernel(cnt_ref, size_ref, goff_ref, gdst_ref, blk_ref, used_ref,
                   span_ref, soff_ref, src_ref, delta_ref):
    size = ((cnt_ref[...] + (SUBLANES - 1)) // SUBLANES) * SUBLANES
    size_ref[...] = size
    goff = _excl_cumsum(size, 0)
    goff_ref[...] = goff
    total = jnp.broadcast_to(jnp.sum(size, axis=1, keepdims=True), size.shape)
    region = ((total + (ROW_BLOCK - 1)) // ROW_BLOCK) * ROW_BLOCK
    pstart = _excl_cumsum(region, 0)
    gdst = pstart + _excl_cumsum(size, 1)
    gdst_ref[...] = gdst
    src = (gdst // PACK_ROWS) * PACK_ROWS
    span = jnp.where(size > 0, ((gdst + size + (PACK_ROWS - 1)) // PACK_ROWS) * PACK_ROWS - src, 0)
    soff = _excl_cumsum(span, 0)
    span_ref[...] = span
    soff_ref[...] = soff
    src_ref[...] = src
    delta_ref[...] = soff + (gdst - src) - goff
    pend = (pstart + region)[:, 0:1]
    start = lax.broadcasted_iota(jnp.int32, (N_EXPERTS, blk_ref.shape[1]), 1) * ROW_BLOCK
    owner = jnp.sum(jnp.where(pend <= start, 1, 0), axis=0, keepdims=True)
    blk_ref[...] = jnp.minimum(owner, N_EXPERTS - 1)
    used_ref[...] = jnp.broadcast_to(jnp.max(pend, axis=0, keepdims=True) // ROW_BLOCK, used_ref.shape)


def _tables_call(cnt, n_blocks):
    nb_pad = -(-n_blocks // LANES) * LANES
    tbl = jax.ShapeDtypeStruct((N_EXPERTS, LANES), jnp.int32)
    return pl.pallas_call(
        _tables_kernel,
        out_shape=[tbl, tbl, tbl, jax.ShapeDtypeStruct((1, nb_pad), jnp.int32),
                   jax.ShapeDtypeStruct((1, LANES), jnp.int32), tbl, tbl, tbl, tbl],
        name="group_tables",
    )(cnt)


def _start_group_copies(tile, size_ref, goff_ref, gdst_ref, sem, make_copy, align):
    def body(e, rows):
        n = pl.multiple_of(size_ref[e * LANES + tile], align)
        @pl.when(n > 0)
        def _():
            off = pl.multiple_of(goff_ref[e * LANES + tile], align)
            dst = pl.multiple_of(gdst_ref[e * LANES + tile], align)
            make_copy(off, dst, n, sem).start()
        return rows + n
    return lax.fori_loop(0, N_EXPERTS, body, jnp.int32(0))


def _wait_rows(rows_hbm, n, sem, align):
    @pl.when(n > 0)
    def _():
        m = pl.multiple_of(n, align)
        half = rows_hbm.shape[0] // (2 * align) * align
        pltpu.make_async_copy(rows_hbm.at[pl.ds(0, m)], rows_hbm.at[pl.ds(half, m)], sem).wait()


def _dispatch_kernel(n_blocks,
                     size_ref, goff_ref, gdst_ref, used_ref,
                     h2_ref, pos_ref, rows_ref,
                     stage, zbuf, inflight, zeroed, sems, zsem):
    t = pl.program_id(0)
    n_tiles = pl.num_programs(0)
    slot = t % 2

    @pl.when(t == 0)
    def _():
        zbuf[...] = jnp.zeros_like(zbuf)
        def zfill(e, n):
            first = gdst_ref[e * LANES]
            last = gdst_ref[e * LANES + LANES - 1] + size_ref[e * LANES + LANES - 1]
            pad = pl.multiple_of((last - first + ROW_BLOCK - 1) // ROW_BLOCK * ROW_BLOCK - (last - first), SUBLANES)
            @pl.when(pad > 0)
            def _():
                pltpu.make_async_copy(zbuf.at[pl.ds(0, pad)],
                                      rows_ref.at[pl.ds(pl.multiple_of(last, SUBLANES), pad)], zsem).start()
            return n + pad
        n_zero = lax.fori_loop(0, N_EXPERTS, zfill, jnp.int32(0))
        def ztail(b, _):
            pltpu.make_async_copy(
                zbuf, rows_ref.at[pl.ds(pl.multiple_of(b * ROW_BLOCK, ROW_BLOCK), ROW_BLOCK)], zsem).start()
            return 0
        lax.fori_loop(used_ref[0], n_blocks, ztail, 0)
        zeroed[0] = n_zero + (n_blocks - used_ref[0]) * ROW_BLOCK

    @pl.when(t >= 2)
    def _():
        _wait_rows(rows_ref, inflight[slot], sems.at[slot], SUBLANES)

    pos = pos_ref[...]
    r = lax.broadcasted_iota(jnp.int32, (STAGE_ROWS, pos.shape[1]), 0)
    hit = jnp.logical_or(jnp.logical_or(r == pos[0:1], r == pos[1:2]),
                         jnp.logical_or(r == pos[2:3], r == pos[3:4]))
    perm = jnp.where(hit, 1.0, 0.0).astype(jnp.bfloat16)
    stage[slot] = jnp.dot(perm, h2_ref[...], preferred_element_type=jnp.float32)

    def make_copy(off, dst, n, sem):
        return pltpu.make_async_copy(stage.at[slot, pl.ds(off, n)], rows_ref.at[pl.ds(dst, n)], sem)
    inflight[slot] = _start_group_copies(t, size_ref, goff_ref, gdst_ref, sems.at[slot], make_copy, SUBLANES)

    @pl.when(t == n_tiles - 1)
    def _():
        _wait_rows(rows_ref, inflight[1 - slot], sems.at[1 - slot], SUBLANES)
        _wait_rows(rows_ref, inflight[slot], sems.at[slot], SUBLANES)
        _wait_rows(rows_ref, zeroed[0], zsem, SUBLANES)


def _dispatch_call(size, goff, gdst, used, h2, pos, n_rows):
    n_tok = h2.shape[0]
    ts = ROUTE_TILE
    n_blocks = n_rows // ROW_BLOCK
    grid_spec = pltpu.PrefetchScalarGridSpec(
        num_scalar_prefetch=4,
        grid=(n_tok // ts,),
        in_specs=[
            pl.BlockSpec((ts, D_MODEL), lambda t, *_: (t, 0)),
            pl.BlockSpec((TOP_K, ts), lambda t, *_: (0, t)),
        ],
        out_specs=pl.BlockSpec(memory_space=pl.ANY),
        scratch_shapes=[
            pltpu.VMEM((2, STAGE_ROWS, D_MODEL), jnp.float32),
            pltpu.VMEM((ROW_BLOCK, D_MODEL), jnp.float32),
            pltpu.SMEM((2,), jnp.int32),
            pltpu.SMEM((1,), jnp.int32),
            pltpu.SemaphoreType.DMA((2,)),
            pltpu.SemaphoreType.DMA(()),
        ])
    return pl.pallas_call(
        functools.partial(_dispatch_kernel, n_blocks),
        grid_spec=grid_spec,
        out_shape=jax.ShapeDtypeStruct((n_rows, D_MODEL), jnp.float32),
        name="dispatch_rows",
        compiler_params=pltpu.CompilerParams(dimension_semantics=("arbitrary",),
                                             vmem_limit_bytes=VMEM_LIMIT_BYTES),
    )(size, goff, gdst, used, h2, pos)


def _expert_kernel(n_blocks, blk_ref, used_ref, x_hbm, w1_hbm, b1_ref, w2_hbm, b2_ref, y_hbm,
                   xbuf, ybuf, zbuf, w1_f32, w2_f32, w1_bf, w2_bf, state, wsems, xsems, ysems, zsem):
    bf16 = jnp.bfloat16
    used = used_ref[0]

    def weight_copies(expert, slot):
        return (pltpu.make_async_copy(w1_hbm.at[expert], w1_f32.at[slot], wsems.at[0, slot]),
                pltpu.make_async_copy(w2_hbm.at[expert], w2_f32.at[slot], wsems.at[1, slot]))

    def block_rows(ref, i):
        return ref.at[pl.ds(pl.multiple_of(i * ROW_BLOCK, ROW_BLOCK), ROW_BLOCK)]

    def x_copy(i, slot):
        return pltpu.make_async_copy(block_rows(x_hbm, i), xbuf.at[slot], xsems.at[slot])

    def y_copy(i, slot):
        return pltpu.make_async_copy(ybuf.at[slot], block_rows(y_hbm, i), ysems.at[slot])

    zbuf[...] = jnp.zeros_like(zbuf)
    def ztail(b, _):
        pltpu.make_async_copy(zbuf, block_rows(y_hbm, b), zsem).start()
        return 0
    lax.fori_loop(used, n_blocks, ztail, 0)

    state[0] = jnp.int32(-1)
    state[1] = jnp.int32(0)
    @pl.when(used > 0)
    def _():
        for cp in weight_copies(blk_ref[0], 0):
            cp.start()
        x_copy(0, 0).start()

    def body(i, carry):
        slot = i % 2
        e = blk_ref[i]

        @pl.when(e != state[0])
        def _():
            wslot = state[1]
            for cp in weight_copies(e, wslot):
                cp.wait()
            w1_bf[...] = w1_f32[wslot].astype(bf16)
            w2_bf[...] = w2_f32[wslot].astype(bf16)
            nxt = lax.while_loop(lambda j: jnp.logical_and(j < used, blk_ref[jnp.minimum(j, used - 1)] == e),
                                 lambda j: j + 1, i + 1)
            @pl.when(nxt < used)
            def _():
                for cp in weight_copies(blk_ref[nxt], 1 - wslot):
                    cp.start()
            state[0] = e
            state[1] = 1 - wslot

        x_copy(i, slot).wait()
        @pl.when(i + 1 < used)
        def _():
            x_copy(i + 1, 1 - slot).start()
        @pl.when(i >= 2)
        def _():
            y_copy(i - 2, slot).wait()

        hu = jnp.dot(xbuf[slot].astype(bf16), w1_bf[...], preferred_element_type=jnp.float32) + b1_ref[e]
        glu = jnp.minimum(hu[:, :D_FF], SWIGLU_LIMIT)
        lin = jnp.clip(hu[:, D_FF:], -SWIGLU_LIMIT, SWIGLU_LIMIT)
        act = glu * (1.0 / (1.0 + jnp.exp(-SWIGLU_ALPHA * glu))) * (lin + 1.0)
        y = jnp.dot(act.astype(bf16), w2_bf[...], preferred_element_type=jnp.float32) + b2_ref[e]
        ybuf[slot] = y.astype(bf16)
        y_copy(i, slot).start()
        return carry
    lax.fori_loop(0, used, body, 0)

    @pl.when(used >= 2)
    def _():
        y_copy(used - 2, used % 2).wait()
    @pl.when(used >= 1)
    def _():
        y_copy(used - 1, (used - 1) % 2).wait()
    _wait_rows(y_hbm, (n_blocks - used) * ROW_BLOCK, zsem, PACK_ROWS)


def _expert_call(blk, used, rows, w1, b1, w2, b2):
    n_rows = rows.shape[0]
    n_blocks = n_rows // ROW_BLOCK
    grid_spec = pltpu.PrefetchScalarGridSpec(
        num_scalar_prefetch=2,
        grid=(1,),
        in_specs=[
            pl.BlockSpec(memory_space=pl.ANY),
            pl.BlockSpec(memory_space=pl.ANY),
            pl.BlockSpec((N_EXPERTS, 1, 2 * D_FF), lambda i, b, u: (0, 0, 0)),
            pl.BlockSpec(memory_space=pl.ANY),
            pl.BlockSpec((N_EXPERTS, 1, D_MODEL), lambda i, b, u: (0, 0, 0)),
        ],
        out_specs=pl.BlockSpec(memory_space=pl.ANY),
        scratch_shapes=[
            pltpu.VMEM((2, ROW_BLOCK, D_MODEL), jnp.float32),
            pltpu.VMEM((2, ROW_BLOCK, D_MODEL), jnp.bfloat16),
            pltpu.VMEM((ROW_BLOCK, D_MODEL), jnp.bfloat16),
            pltpu.VMEM((2, D_MODEL, 2 * D_FF), jnp.float32),
            pltpu.VMEM((2, D_FF, D_MODEL), jnp.float32),
            pltpu.VMEM((D_MODEL, 2 * D_FF), jnp.bfloat16),
            pltpu.VMEM((D_FF, D_MODEL), jnp.bfloat16),
            pltpu.SMEM((2,), jnp.int32),
            pltpu.SemaphoreType.DMA((2, 2)),
            pltpu.SemaphoreType.DMA((2,)),
            pltpu.SemaphoreType.DMA((2,)),
            pltpu.SemaphoreType.DMA(()),
        ])
    return pl.pallas_call(
        functools.partial(_expert_kernel, n_blocks),
        grid_spec=grid_spec,
        out_shape=jax.ShapeDtypeStruct((n_rows, D_MODEL), jnp.bfloat16),
        name="expert_ffn",
        compiler_params=pltpu.CompilerParams(dimension_semantics=("arbitrary",),
                                             vmem_limit_bytes=VMEM_LIMIT_BYTES),
    )(blk, used, rows, w1, b1.reshape(N_EXPERTS, 1, 2 * D_FF), w2, b2.reshape(N_EXPERTS, 1, D_MODEL))


def _combine_kernel(span_ref, soff_ref, src_ref, goff_ref, delta_ref,
                    x1_ref, pos_ref, gate_ref, y_ref, gfin_ref, out_ref,
                    stage, inflight, sems):
    t = pl.program_id(0)
    n_tiles = pl.num_programs(0)
    slot = t % 2

    def fetch(tile, slot_):
        def make_copy(off, src, n, sem):
            return pltpu.make_async_copy(y_ref.at[pl.ds(src, n)], stage.at[slot_, pl.ds(off, n)], sem)
        inflight[slot_] = _start_group_copies(tile, span_ref, soff_ref, src_ref, sems.at[slot_], make_copy,
                                              PACK_ROWS)

    @pl.when(t == 0)
    def _():
        stage[...] = jnp.zeros_like(stage)
        fetch(0, 0)

    @pl.when(t + 1 < n_tiles)
    def _():
        fetch(t + 1, 1 - slot)

    _wait_rows(y_ref, inflight[slot], sems.at[slot], PACK_ROWS)

    pos = pos_ref[...]
    shift = jnp.zeros_like(pos)
    prev = jnp.int32(0)
    for e in range(N_EXPERTS):
        d = delta_ref[e * LANES + t]
        shift = shift + jnp.where(pos >= goff_ref[e * LANES + t], d - prev, 0)
        prev = d
    pos = pos + shift
    gate = gate_ref[...]
    r = lax.broadcasted_iota(jnp.int32, (COMBINE_ROWS, pos.shape[1]), 0)
    w = jnp.zeros(r.shape, jnp.float32)
    for k in range(TOP_K):
        w = jnp.where(r == pos[k:k + 1], gate[k:k + 1], w)
    moe = lax.dot_general(w.astype(jnp.bfloat16), stage[slot],
                          (((0,), (0,)), ((), ())), preferred_element_type=jnp.float32)
    out_ref[...] = _rms(x1_ref[...] + moe, gfin_ref[...])


def _combine_call(span, soff, src, goff, delta, x1, pos, gate, y_rows, g_final):
    n_tok = x1.shape[0]
    ts = ROUTE_TILE
    grid_spec = pltpu.PrefetchScalarGridSpec(
        num_scalar_prefetch=5,
        grid=(n_tok // ts,),
        in_specs=[
            pl.BlockSpec((ts, D_MODEL), lambda t, *_: (t, 0)),
            pl.BlockSpec((TOP_K, ts), lambda t, *_: (0, t)),
            pl.BlockSpec((TOP_K, ts), lambda t, *_: (0, t)),
            pl.BlockSpec(memory_space=pl.ANY),
            pl.BlockSpec((1, D_MODEL), lambda t, *_: (0, 0)),
        ],
        out_specs=pl.BlockSpec((ts, D_MODEL), lambda t, *_: (t, 0)),
        scratch_shapes=[
            pltpu.VMEM((2, COMBINE_ROWS, D_MODEL), jnp.bfloat16),
            pltpu.SMEM((2,), jnp.int32),
            pltpu.SemaphoreType.DMA((2,)),
        ])
    return pl.pallas_call(
        _combine_kernel,
        grid_spec=grid_spec,
        out_shape=jax.ShapeDtypeStruct((n_tok, D_MODEL), jnp.float32),
        name="combine_norm",
        compiler_params=pltpu.CompilerParams(dimension_semantics=("arbitrary",),
                                             vmem_limit_bytes=VMEM_LIMIT_BYTES),
    )(span, soff, src, goff, delta, x1, pos, gate, y_rows, g_final)


def _attention_bias():
    slopes = jnp.exp2(-8.0 * jnp.arange(1, N_Q_HEADS + 1, dtype=jnp.float32) / N_Q_HEADS)
    qi = jnp.arange(BLOCK)[:, None]
    kj = jnp.arange(BAND)[None, :]
    dist = qi + BLOCK - kj
    valid = (dist >= 0) & (dist < WINDOW)
    b = -slopes[:, None, None] * dist.astype(jnp.float32)[None]
    b = jnp.where(valid[None], b, -jnp.inf)
    b = b.reshape(N_KV_HEADS, GROUP, BLOCK, BAND).transpose(0, 3, 1, 2)
    return b.reshape(N_KV_HEADS, BAND, GROUP * BLOCK)


def kernel(x, g_mix, w_in, conv_w, sinks, g_attn_out, g_conv_out, w_out, g_ffn, w_router, b_router,
           w1, b1, w2, b2, g_final):
    bsz, s_len, d = x.shape
    n_tok = bsz * s_len
    n_tiles = n_tok // ROUTE_TILE
    n_rows = n_tok * TOP_K + n_tiles * N_EXPERTS * SUBLANES + N_EXPERTS * ROW_BLOCK
    n_blocks = n_rows // ROW_BLOCK
    x2d = x.reshape(n_tok, d)
    bias = _attention_bias()
    l = 0
    wrt = w_router[l].T
    wr_hi = wrt.astype(jnp.bfloat16)
    wr_lo = (wrt - wr_hi.astype(jnp.float32)).astype(jnp.bfloat16)
    wr_split = jnp.concatenate([wr_hi, wr_lo], axis=0)
    x1, h2, pos, gate, cnt = _mixer_call(
        x2d, sinks[l], g_mix[l].reshape(1, d), w_in[l].astype(jnp.bfloat16), conv_w[l], bias,
        g_attn_out[l].reshape(1, ATTN_WIDTH), g_conv_out[l].reshape(1, CONV_WIDTH),
        w_out[l].astype(jnp.bfloat16), g_ffn[l].reshape(1, d), wr_split,
        b_router[l].reshape(N_EXPERTS, 1), s_len)
    size, goff, gdst, blk, used, span, soff, src, delta = _tables_call(cnt, n_blocks)
    size, goff, gdst = size.reshape(-1), goff.reshape(-1), gdst.reshape(-1)
    span, soff, src, delta = span.reshape(-1), soff.reshape(-1), src.reshape(-1), delta.reshape(-1)
    blk, used = blk.reshape(-1), used.reshape(-1)[:1]
    rows = _dispatch_call(size, goff, gdst, used, h2, pos, n_rows)
    y_rows = _expert_call(blk, used, rows, w1[l], b1[l], w2[l], b2[l])
    out = _combine_call(span, soff, src, goff, delta, x1, pos, gate, y_rows, g_final.reshape(1, d))
    return out.reshape(bsz, s_len, d)
```

```python
import functools
import math

import jax
import jax.numpy as jnp
from jax import lax
from jax.experimental import pallas as pl
from jax.experimental.pallas import tpu as pltpu

D_MODEL = 1024
N_Q_HEADS = 8
N_KV_HEADS = 2
HEAD_DIM = 64
GROUP = N_Q_HEADS // N_KV_HEADS
ATTN_WIDTH = N_Q_HEADS * HEAD_DIM
KV_WIDTH = N_KV_HEADS * HEAD_DIM
CONV_WIDTH = D_MODEL - ATTN_WIDTH
QKV_COLS = ATTN_WIDTH + 2 * KV_WIDTH
IN_COLS = QKV_COLS + 3 * CONV_WIDTH
WINDOW = 128
BLOCK = 128
BAND = 2 * BLOCK
N_EXPERTS = 32
TOP_K = 4
D_FF = D_MODEL
SWIGLU_ALPHA = 1.702
SWIGLU_LIMIT = 7.0
EPS = 1e-5

LANES = 128
SUBLANES = 8
VMEM_LIMIT_BYTES = 56 * 1024 * 1024

TOKEN_TILE = 1024
SUB_TILE = 512
ROUTE_TILE = 256
ROW_BLOCK = 512
STAGE_ROWS = -(-(TOP_K * ROUTE_TILE + N_EXPERTS * (SUBLANES - 1)) // LANES) * LANES
PACK_ROWS = 16
COMBINE_ROWS = -(-(TOP_K * ROUTE_TILE + N_EXPERTS * (SUBLANES - 1 + PACK_ROWS)) // LANES) * LANES


def _rms(x, g):
    return x * lax.rsqrt(jnp.mean(x * x, axis=-1, keepdims=True) + EPS) * g


def _excl_cumsum(a, axis):
    idx = lax.broadcasted_iota(jnp.int32, a.shape, axis)
    inc = a
    step = 1
    while step < a.shape[axis]:
        inc = inc + jnp.where(idx >= step, pltpu.roll(inc, step, axis), 0)
        step *= 2
    return inc - a


def _mixer_kernel(tiles_per_seq,
                  sink_ref,
                  x_ref, gmix_ref, win_ref, convw_ref, bias_ref, gattn_ref, gconv_ref, wout_ref,
                  gffn_ref, wr_ref, br_ref,
                  x1_ref, h2_ref, pos_ref, gate_ref, cnt_ref,
                  p_scr, h_scr, mix_scr, k_scr, vt_scr, u_scr, attn_t_scr, tri_scr):
    tm = x_ref.shape[0]
    sub_tm = SUB_TILE
    ts = ROUTE_TILE
    i = pl.program_id(0)
    first = (i % tiles_per_seq) == 0
    f32, bf16 = jnp.float32, jnp.bfloat16

    @pl.when(i == 0)
    def _():
        cnt_ref[...] = jnp.zeros_like(cnt_ref)
        r = lax.broadcasted_iota(jnp.int32, (ts, ts), 0)
        c = lax.broadcasted_iota(jnp.int32, (ts, ts), 1)
        tri_scr[...] = jnp.where(r < c, 1.0, 0.0).astype(bf16)

    @pl.when(first)
    def _():
        k_scr[0:BLOCK, :] = jnp.zeros((BLOCK, k_scr.shape[1]), bf16)
        vt_scr[:, 0:BLOCK] = jnp.zeros((KV_WIDTH, BLOCK), bf16)
        u_scr[0:SUBLANES, :] = jnp.zeros((SUBLANES, CONV_WIDTH), f32)

    n_sub = tm // sub_tm
    in_chunks = [(c, c + 2 * LANES) for c in range(0, IN_COLS, 2 * LANES)]
    out_chunks = [(c, c + 2 * LANES) for c in range(0, D_MODEL, 2 * LANES)]
    lane = lax.broadcasted_iota(jnp.int32, (BAND, LANES), 1)
    lo = lane < HEAD_DIM
    qgrp = lax.broadcasted_iota(jnp.int32, (BLOCK, 2 * LANES), 1) // HEAD_DIM
    key_row = lax.broadcasted_iota(jnp.int32, (BAND, GROUP * BLOCK), 0)
    scale = 1.0 / math.sqrt(HEAD_DIM)
    zero_q = jnp.zeros((BLOCK, 2 * LANES), bf16)
    nt = (((1,), (1,)), ((), ()))

    def rows_of(sub):
        return slice(sub * sub_tm, (sub + 1) * sub_tm)

    def prep(sub):
        r0 = rows_of(sub)
        h_scr[r0, :] = _rms(x_ref[r0, :], gmix_ref[...]).astype(bf16)

    def in_proj(sub, c0, c1):
        r0 = rows_of(sub)
        p_scr[r0, c0:c1] = jnp.dot(h_scr[r0, :], win_ref[:, c0:c1], preferred_element_type=f32)

    def kv_stage(sub):
        r0 = rows_of(sub)
        base = sub * sub_tm
        kk = p_scr[r0, ATTN_WIDTH:ATTN_WIDTH + KV_WIDTH]
        kr = pltpu.roll(kk, HEAD_DIM, axis=1)
        k_scr[BLOCK + base:BLOCK + base + sub_tm, :] = jnp.concatenate([kk, kr], axis=1).astype(bf16)
        vt_scr[:, BLOCK + base:BLOCK + base + sub_tm] = jnp.transpose(
            p_scr[r0, ATTN_WIDTH + KV_WIDTH:QKV_COLS]).astype(bf16)

    def attn_iter(j, hh):
        rows = slice(j * BLOCK, (j + 1) * BLOCK)
        kband = k_scr[j * BLOCK:j * BLOCK + BAND, :]
        bk, bkr = kband[:, 0:LANES], kband[:, LANES:2 * LANES]
        vt_band = vt_scr[:, j * BLOCK:j * BLOCK + BAND]
        k_both = jnp.where(lo, bk, bkr) if hh == 0 else jnp.where(lo, bkr, bk)
        kmat = jnp.concatenate([k_both, k_both], axis=1)
        qh = (p_scr[rows, hh * 2 * LANES:(hh + 1) * 2 * LANES] * scale).astype(bf16)
        qm = jnp.concatenate([jnp.where(qgrp == g, qh, zero_q) for g in range(GROUP)], axis=0)
        s = lax.dot_general(kmat, qm, nt, preferred_element_type=f32) + bias_ref[hh]
        if j == 0:
            s = jnp.where(jnp.logical_and(first, key_row < BLOCK), -jnp.inf, s)
        sink = jnp.concatenate(
            [jnp.full((1, BLOCK), sink_ref[hh * GROUP + g], f32) for g in range(GROUP)], axis=1)
        m = jnp.maximum(jnp.max(s, axis=0, keepdims=True), sink)
        e = jnp.exp(s - m)
        inv = 1.0 / (jnp.sum(e, axis=0, keepdims=True) + jnp.exp(sink - m))
        o_t = jnp.dot(vt_band, e.astype(bf16), preferred_element_type=f32)
        o_t = o_t[hh * HEAD_DIM:(hh + 1) * HEAD_DIM, :] * inv
        for g in range(GROUP):
            head = hh * GROUP + g
            attn_t_scr[head * HEAD_DIM:(head + 1) * HEAD_DIM, rows] = o_t[:, g * BLOCK:(g + 1) * BLOCK]

    def conv_mix(sub):
        r0 = rows_of(sub)
        base = sub * sub_tm
        bg = p_scr[r0, QKV_COLS:QKV_COLS + CONV_WIDTH]
        cg = p_scr[r0, QKV_COLS + CONV_WIDTH:QKV_COLS + 2 * CONV_WIDTH]
        xin = p_scr[r0, QKV_COLS + 2 * CONV_WIDTH:IN_COLS]
        u = cg * xin
        u_scr[SUBLANES + base:SUBLANES + base + sub_tm, :] = u
        u1 = u_scr[SUBLANES - 1 + base:SUBLANES - 1 + base + sub_tm, :]
        u2 = u_scr[SUBLANES - 2 + base:SUBLANES - 2 + base + sub_tm, :]
        y = convw_ref[0:1, :] * u2 + convw_ref[1:2, :] * u1 + convw_ref[2:3, :] * u
        conv = bg * y
        mix_scr[r0, :] = jnp.concatenate(
            [_rms(jnp.transpose(attn_t_scr[:, r0]), gattn_ref[...]), _rms(conv, gconv_ref[...])],
            axis=1).astype(bf16)

    def out_proj(sub, c0, c1):
        r0 = rows_of(sub)
        x1_ref[r0, c0:c1] = x_ref[r0, c0:c1] + jnp.dot(mix_scr[r0, :], wout_ref[:, c0:c1],
                                                      preferred_element_type=f32)

    def route(sub):
        r0 = rows_of(sub)
        h2 = _rms(x1_ref[r0, :], gffn_ref[...])
        h2_hi = h2.astype(bf16)
        h2_ref[r0, :] = h2_hi
        h2_lo = (h2 - h2_hi.astype(f32)).astype(bf16)
        by_hi = lax.dot_general(wr_ref[...], h2_hi, nt, preferred_element_type=f32)
        by_lo = lax.dot_general(wr_ref[0:N_EXPERTS, :], h2_lo, nt, preferred_element_type=f32)
        logits = by_hi[0:N_EXPERTS] + by_hi[N_EXPERTS:] + by_lo + br_ref[...]
        row = lax.broadcasted_iota(jnp.int32, (N_EXPERTS, sub_tm), 0)
        vals, idxs = [], []
        l = logits
        for _ in range(TOP_K):
            m = jnp.max(l, axis=0, keepdims=True)
            idx = jnp.min(jnp.where(l == m, row, N_EXPERTS), axis=0, keepdims=True)
            vals.append(m)
            idxs.append(idx)
            l = jnp.where(row == idx, -jnp.inf, l)
        es = [jnp.exp(v - vals[0]) for v in vals]
        inv = 1.0 / (es[0] + es[1] + es[2] + es[3])
        gate_ref[:, r0] = jnp.concatenate([e * inv for e in es], axis=0)

        onehots = [row == idx for idx in idxs]
        sel = jnp.logical_or(jnp.logical_or(onehots[0], onehots[1]), jnp.logical_or(onehots[2], onehots[3]))
        oh = jnp.where(sel, 1.0, 0.0)
        cnt_lane = lax.broadcasted_iota(jnp.int32, cnt_ref.shape, 1)
        poss = []
        for rt in range(sub_tm // ts):
            sl = slice(rt * ts, (rt + 1) * ts)
            oh_s = oh[:, sl]
            cum = jnp.dot(oh_s.astype(bf16), tri_scr[...], preferred_element_type=f32)
            cnt = jnp.sum(oh_s, axis=1, keepdims=True)
            padded = SUBLANES * jnp.floor((cnt + (SUBLANES - 1.0)) * (1.0 / SUBLANES))
            goff = _excl_cumsum(jnp.broadcast_to(padded, (N_EXPERTS, LANES)), 0)[:, 0:1]
            pos = goff + cum
            poss.append(jnp.concatenate(
                [jnp.sum(jnp.where(o[:, sl], pos, 0.0), axis=0, keepdims=True) for o in onehots], axis=0))
            tile_id = i * (tm // ts) + sub * (sub_tm // ts) + rt
            cnt_ref[...] = jnp.where(cnt_lane == tile_id, cnt.astype(jnp.int32), cnt_ref[...])
        pos_ref[:, r0] = jnp.concatenate(poss, axis=1).astype(jnp.int32)

    def interleave(main, fillers):
        after = [[] for _ in main]
        for n, filler in enumerate(fillers):
            after[n * len(main) // len(fillers)].append(filler)
        for stage, extra in zip(main, after):
            stage()
            for filler in extra:
                filler()

    def attention_stages(sub):
        blocks = range(sub * (sub_tm // BLOCK), (sub + 1) * (sub_tm // BLOCK))
        return [functools.partial(attn_iter, j, hh) for j in blocks for hh in range(N_KV_HEADS)]

    prep(0)
    for c0, c1 in in_chunks:
        in_proj(0, c0, c1)
    kv_stage(0)
    for sub in range(n_sub):
        fillers = []
        if sub > 0:
            fillers.append(functools.partial(conv_mix, sub - 1))
            fillers += [functools.partial(out_proj, sub - 1, c0, c1) for c0, c1 in out_chunks]
        if sub + 1 < n_sub:
            prep(sub + 1)
            fillers += [functools.partial(in_proj, sub + 1, c0, c1) for c0, c1 in in_chunks]
        interleave(attention_stages(sub), fillers)
        if sub + 1 < n_sub:
            kv_stage(sub + 1)
        if 0 < sub < n_sub - 1:
            route(sub - 1)
    last = n_sub - 1
    conv_mix(last)
    out_proj(last, *out_chunks[0])
    if last > 0:
        route(last - 1)
    for c0, c1 in out_chunks[1:]:
        out_proj(last, c0, c1)
    route(last)

    k_scr[0:BLOCK, :] = k_scr[tm:tm + BLOCK, :]
    vt_scr[:, 0:BLOCK] = vt_scr[:, tm:tm + BLOCK]
    u_scr[0:SUBLANES, :] = u_scr[tm:tm + SUBLANES, :]


def _mixer_call(x2d, sinks, g_mix, w_in, conv_w, bias, g_attn, g_conv, w_out, g_ffn, wr_split, br, seq_len):
    n_tok = x2d.shape[0]
    tm = TOKEN_TILE
    n_tiles = n_tok // tm
    assert n_tok // ROUTE_TILE == LANES, "one lane of the count table per routing tile"
    const = lambda shape: pl.BlockSpec(shape, lambda i, *_: (0,) * len(shape))
    grid_spec = pltpu.PrefetchScalarGridSpec(
        num_scalar_prefetch=0,
        grid=(n_tiles,),
        in_specs=[
            pl.BlockSpec(memory_space=pltpu.SMEM),
            pl.BlockSpec((tm, D_MODEL), lambda i: (i, 0)),
            const((1, D_MODEL)), const((D_MODEL, IN_COLS)), const((3, CONV_WIDTH)),
            const((N_KV_HEADS, BAND, GROUP * BLOCK)),
            const((1, ATTN_WIDTH)), const((1, CONV_WIDTH)), const((D_MODEL, D_MODEL)),
            const((1, D_MODEL)), const((2 * N_EXPERTS, D_MODEL)), const((N_EXPERTS, 1)),
        ],
        out_specs=[
            pl.BlockSpec((tm, D_MODEL), lambda i: (i, 0)),
            pl.BlockSpec((tm, D_MODEL), lambda i: (i, 0)),
            pl.BlockSpec((TOP_K, tm), lambda i: (0, i)),
            pl.BlockSpec((TOP_K, tm), lambda i: (0, i)),
            pl.BlockSpec((N_EXPERTS, LANES), lambda i: (0, 0)),
        ],
        scratch_shapes=[
            pltpu.VMEM((tm, IN_COLS), jnp.float32),
            pltpu.VMEM((tm, D_MODEL), jnp.bfloat16),
            pltpu.VMEM((tm, D_MODEL), jnp.bfloat16),
            pltpu.VMEM((BLOCK + tm, 2 * LANES), jnp.bfloat16),
            pltpu.VMEM((KV_WIDTH, BLOCK + tm), jnp.bfloat16),
            pltpu.VMEM((SUBLANES + tm, CONV_WIDTH), jnp.float32),
            pltpu.VMEM((ATTN_WIDTH, tm), jnp.float32),
            pltpu.VMEM((ROUTE_TILE, ROUTE_TILE), jnp.bfloat16),
        ])
    out_shape = [
        jax.ShapeDtypeStruct((n_tok, D_MODEL), jnp.float32),
        jax.ShapeDtypeStruct((n_tok, D_MODEL), jnp.bfloat16),
        jax.ShapeDtypeStruct((TOP_K, n_tok), jnp.int32),
        jax.ShapeDtypeStruct((TOP_K, n_tok), jnp.float32),
        jax.ShapeDtypeStruct((N_EXPERTS, LANES), jnp.int32),
    ]
    return pl.pallas_call(
        functools.partial(_mixer_kernel, seq_len // tm),
        grid_spec=grid_spec, out_shape=out_shape, name="mixer_router",
        compiler_params=pltpu.CompilerParams(dimension_semantics=("arbitrary",),
                                             vmem_limit_bytes=VMEM_LIMIT_BYTES),
    )(sinks, x2d, g_mix, w_in, conv_w, bias, g_attn, g_conv, w_out, g_ffn, wr_split, br)


def _tables_kernel(cnt_ref, size_ref, goff_ref, gdst_ref, blk_ref, used_ref,
                   span_ref, soff_ref, src_ref, delta_ref):
    size = ((cnt_ref[...] + (SUBLANES - 1)) // SUBLANES) * SUBLANES
    size_ref[...] = size
    goff = _excl_cumsum(size, 0)
    goff_ref[...] = goff
    total = jnp.broadcast_to(jnp.sum(size, axis=1, keepdims=True), size.shape)
    region = ((total + (ROW_BLOCK - 1)) // ROW_BLOCK) * ROW_BLOCK
    pstart = _excl_cumsum(region, 0)
    gdst = pstart + _excl_cumsum(size, 1)
    gdst_ref[...] = gdst
    src = (gdst // PACK_ROWS) * PACK_ROWS
    span = jnp.where(size > 0, ((gdst + size + (PACK_ROWS - 1)) // PACK_ROWS) * PACK_ROWS - src, 0)
    soff = _excl_cumsum(span, 0)
    span_ref[...] = span
    soff_ref[...] = soff
    src_ref[...] = src
    delta_ref[...] = soff + (gdst - src) - goff
    pend = (pstart + region)[:, 0:1]
    start = lax.broadcasted_iota(jnp.int32, (N_EXPERTS, blk_ref.shape[1]), 1) * ROW_BLOCK
    owner = jnp.sum(jnp.where(pend <= start, 1, 0), axis=0, keepdims=True)
    blk_ref[...] = jnp.minimum(owner, N_EXPERTS - 1)
    used_ref[...] = jnp.broadcast_to(jnp.max(pend, axis=0, keepdims=True) // ROW_BLOCK, used_ref.shape)


def _tables_call(cnt, n_blocks):
    nb_pad = -(-n_blocks // LANES) * LANES
    tbl = jax.ShapeDtypeStruct((N_EXPERTS, LANES), jnp.int32)
    return pl.pallas_call(
        _tables_kernel,
        out_shape=[tbl, tbl, tbl, jax.ShapeDtypeStruct((1, nb_pad), jnp.int32),
                   jax.ShapeDtypeStruct((1, LANES), jnp.int32), tbl, tbl, tbl, tbl],
        name="group_tables",
    )(cnt)


def _start_group_copies(tile, size_ref, goff_ref, gdst_ref, sem, make_copy, align):
    def body(e, rows):
        n = pl.multiple_of(size_ref[e * LANES + tile], align)
        @pl.when(n > 0)
        def _():
            off = pl.multiple_of(goff_ref[e * LANES + tile], align)
            dst = pl.multiple_of(gdst_ref[e * LANES + tile], align)
            make_copy(off, dst, n, sem).start()
        return rows + n
    return lax.fori_loop(0, N_EXPERTS, body, jnp.int32(0))


def _wait_rows(rows_hbm, n, sem, align):
    @pl.when(n > 0)
    def _():
        m = pl.multiple_of(n, align)
        half = rows_hbm.shape[0] // (2 * align) * align
        pltpu.make_async_copy(rows_hbm.at[pl.ds(0, m)], rows_hbm.at[pl.ds(half, m)], sem).wait()


def _dispatch_kernel(n_blocks,
                     size_ref, goff_ref, gdst_ref, used_ref,
                     h2_ref, pos_ref, rows_ref,
                     stage, zbuf, inflight, zeroed, sems, zsem):
    t = pl.program_id(0)
    n_tiles = pl.num_programs(0)
    slot = t % 2

    @pl.when(t == 0)
    def _():
        zbuf[...] = jnp.zeros_like(zbuf)
        def zfill(e, n):
            first = gdst_ref[e * LANES]
            last = gdst_ref[e * LANES + LANES - 1] + size_ref[e * LANES + LANES - 1]
            pad = pl.multiple_of((last - first + ROW_BLOCK - 1) // ROW_BLOCK * ROW_BLOCK - (last - first), SUBLANES)
            @pl.when(pad > 0)
            def _():
                pltpu.make_async_copy(zbuf.at[pl.ds(0, pad)],
                                      rows_ref.at[pl.ds(pl.multiple_of(last, SUBLANES), pad)], zsem).start()
            return n + pad
        n_zero = lax.fori_loop(0, N_EXPERTS, zfill, jnp.int32(0))
        def ztail(b, _):
            pltpu.make_async_copy(
                zbuf, rows_ref.at[pl.ds(pl.multiple_of(b * ROW_BLOCK, ROW_BLOCK), ROW_BLOCK)], zsem).start()
            return 0
        lax.fori_loop(used_ref[0], n_blocks, ztail, 0)
        zeroed[0] = n_zero + (n_blocks - used_ref[0]) * ROW_BLOCK

    @pl.when(t >= 2)
    def _():
        _wait_rows(rows_ref, inflight[slot], sems.at[slot], SUBLANES)

    pos = pos_ref[...]
    r = lax.broadcasted_iota(jnp.int32, (STAGE_ROWS, pos.shape[1]), 0)
    hit = jnp.logical_or(jnp.logical_or(r == pos[0:1], r == pos[1:2]),
                         jnp.logical_or(r == pos[2:3], r == pos[3:4]))
    perm = jnp.where(hit, 1.0, 0.0).astype(jnp.bfloat16)
    stage[slot] = jnp.dot(perm, h2_ref[...], preferred_element_type=jnp.float32)

    def make_copy(off, dst, n, sem):
        return pltpu.make_async_copy(stage.at[slot, pl.ds(off, n)], rows_ref.at[pl.ds(dst, n)], sem)
    inflight[slot] = _start_group_copies(t, size_ref, goff_ref, gdst_ref, sems.at[slot], make_copy, SUBLANES)

    @pl.when(t == n_tiles - 1)
    def _():
        _wait_rows(rows_ref, inflight[1 - slot], sems.at[1 - slot], SUBLANES)
        _wait_rows(rows_ref, inflight[slot], sems.at[slot], SUBLANES)
        _wait_rows(rows_ref, zeroed[0], zsem, SUBLANES)


def _dispatch_call(size, goff, gdst, used, h2, pos, n_rows):
    n_tok = h2.shape[0]
    ts = ROUTE_TILE
    n_blocks = n_rows // ROW_BLOCK
    grid_spec = pltpu.PrefetchScalarGridSpec(
        num_scalar_prefetch=4,
        grid=(n_tok // ts,),
        in_specs=[
            pl.BlockSpec((ts, D_MODEL), lambda t, *_: (t, 0)),
            pl.BlockSpec((TOP_K, ts), lambda t, *_: (0, t)),
        ],
        out_specs=pl.BlockSpec(memory_space=pl.ANY),
        scratch_shapes=[
            pltpu.VMEM((2, STAGE_ROWS, D_MODEL), jnp.float32),
            pltpu.VMEM((ROW_BLOCK, D_MODEL), jnp.float32),
            pltpu.SMEM((2,), jnp.int32),
            pltpu.SMEM((1,), jnp.int32),
            pltpu.SemaphoreType.DMA((2,)),
            pltpu.SemaphoreType.DMA(()),
        ])
    return pl.pallas_call(
        functools.partial(_dispatch_kernel, n_blocks),
        grid_spec=grid_spec,
        out_shape=jax.ShapeDtypeStruct((n_rows, D_MODEL), jnp.float32),
        name="dispatch_rows",
        compiler_params=pltpu.CompilerParams(dimension_semantics=("arbitrary",),
                                             vmem_limit_bytes=VMEM_LIMIT_BYTES),
    )(size, goff, gdst, used, h2, pos)


def _expert_kernel(blk_ref, used_ref, x_ref, w1_hbm, b1_ref, w2_hbm, b2_ref, y_ref,
                   w1_f32, w2_f32, w1_bf, w2_bf, state, sems):
    i = pl.program_id(0)
    bf16 = jnp.bfloat16
    e = blk_ref[i]
    used = used_ref[0]

    def weight_copies(expert, slot):
        return (pltpu.make_async_copy(w1_hbm.at[expert], w1_f32.at[slot], sems.at[0, slot]),
                pltpu.make_async_copy(w2_hbm.at[expert], w2_f32.at[slot], sems.at[1, slot]))

    @pl.when(i == 0)
    def _():
        state[0] = jnp.int32(-1)
        state[1] = jnp.int32(0)
        @pl.when(used > 0)
        def _():
            for cp in weight_copies(e, 0):
                cp.start()

    @pl.when(jnp.logical_and(i < used, e != state[0]))
    def _():
        slot = state[1]
        for cp in weight_copies(e, slot):
            cp.wait()
        w1_bf[...] = w1_f32[slot].astype(bf16)
        w2_bf[...] = w2_f32[slot].astype(bf16)
        nxt = lax.while_loop(lambda j: jnp.logical_and(j < used, blk_ref[jnp.minimum(j, used - 1)] == e),
                             lambda j: j + 1, i + 1)
        @pl.when(nxt < used)
        def _():
            for cp in weight_copies(blk_ref[nxt], 1 - slot):
                cp.start()
        state[0] = e
        state[1] = 1 - slot

    @pl.when(i >= used)
    def _():
        y_ref[...] = jnp.zeros_like(y_ref)

    @pl.when(i < used)
    def _():
        hu = jnp.dot(x_ref[...].astype(bf16), w1_bf[...], preferred_element_type=jnp.float32) + b1_ref[0]
        glu = jnp.minimum(hu[:, :D_FF], SWIGLU_LIMIT)
        lin = jnp.clip(hu[:, D_FF:], -SWIGLU_LIMIT, SWIGLU_LIMIT)
        act = glu * (1.0 / (1.0 + jnp.exp(-SWIGLU_ALPHA * glu))) * (lin + 1.0)
        y = jnp.dot(act.astype(bf16), w2_bf[...], preferred_element_type=jnp.float32) + b2_ref[0]
        y_ref[...] = y.astype(bf16)


def _expert_call(blk, used, rows, w1, b1, w2, b2):
    n_rows = rows.shape[0]
    n_blocks = n_rows // ROW_BLOCK
    last = lambda i, b, u: jnp.maximum(jnp.minimum(i, u[0] - 1), 0)
    grid_spec = pltpu.PrefetchScalarGridSpec(
        num_scalar_prefetch=2,
        grid=(n_blocks,),
        in_specs=[
            pl.BlockSpec((ROW_BLOCK, D_MODEL), lambda i, b, u: (last(i, b, u), 0)),
            pl.BlockSpec(memory_space=pl.ANY),
            pl.BlockSpec((1, 1, 2 * D_FF), lambda i, b, u: (b[i], 0, 0)),
            pl.BlockSpec(memory_space=pl.ANY),
            pl.BlockSpec((1, 1, D_MODEL), lambda i, b, u: (b[i], 0, 0)),
        ],
        out_specs=pl.BlockSpec((ROW_BLOCK, D_MODEL), lambda i, b, u: (i, 0)),
        scratch_shapes=[
            pltpu.VMEM((2, D_MODEL, 2 * D_FF), jnp.float32),
            pltpu.VMEM((2, D_FF, D_MODEL), jnp.float32),
            pltpu.VMEM((D_MODEL, 2 * D_FF), jnp.bfloat16),
            pltpu.VMEM((D_FF, D_MODEL), jnp.bfloat16),
            pltpu.SMEM((2,), jnp.int32),
            pltpu.SemaphoreType.DMA((2, 2)),
        ])
    return pl.pallas_call(
        _expert_kernel,
        grid_spec=grid_spec,
        out_shape=jax.ShapeDtypeStruct((n_rows, D_MODEL), jnp.bfloat16),
        name="expert_ffn",
        compiler_params=pltpu.CompilerParams(dimension_semantics=("arbitrary",),
                                             vmem_limit_bytes=VMEM_LIMIT_BYTES),
    )(blk, used, rows, w1, b1.reshape(N_EXPERTS, 1, 2 * D_FF), w2, b2.reshape(N_EXPERTS, 1, D_MODEL))


def _combine_kernel(span_ref, soff_ref, src_ref, goff_ref, delta_ref,
                    x1_ref, pos_ref, gate_ref, y_ref, gfin_ref, out_ref,
                    stage, inflight, sems):
    t = pl.program_id(0)
    n_tiles = pl.num_programs(0)
    slot = t % 2

    def fetch(tile, slot_):
        def make_copy(off, src, n, sem):
            return pltpu.make_async_copy(y_ref.at[pl.ds(src, n)], stage.at[slot_, pl.ds(off, n)], sem)
        inflight[slot_] = _start_group_copies(tile, span_ref, soff_ref, src_ref, sems.at[slot_], make_copy,
                                              PACK_ROWS)

    @pl.when(t == 0)
    def _():
        stage[...] = jnp.zeros_like(stage)
        fetch(0, 0)

    @pl.when(t + 1 < n_tiles)
    def _():
        fetch(t + 1, 1 - slot)

    _wait_rows(y_ref, inflight[slot], sems.at[slot], PACK_ROWS)

    pos = pos_ref[...]
    shift = jnp.zeros_like(pos)
    prev = jnp.int32(0)
    for e in range(N_EXPERTS):
        d = delta_ref[e * LANES + t]
        shift = shift + jnp.where(pos >= goff_ref[e * LANES + t], d - prev, 0)
        prev = d
    pos = pos + shift
    gate = gate_ref[...]
    r = lax.broadcasted_iota(jnp.int32, (COMBINE_ROWS, pos.shape[1]), 0)
    w = jnp.zeros(r.shape, jnp.float32)
    for k in range(TOP_K):
        w = jnp.where(r == pos[k:k + 1], gate[k:k + 1], w)
    moe = lax.dot_general(w.astype(jnp.bfloat16), stage[slot],
                          (((0,), (0,)), ((), ())), preferred_element_type=jnp.float32)
    out_ref[...] = _rms(x1_ref[...] + moe, gfin_ref[...])


def _combine_call(span, soff, src, goff, delta, x1, pos, gate, y_rows, g_final):
    n_tok = x1.shape[0]
    ts = ROUTE_TILE
    grid_spec = pltpu.PrefetchScalarGridSpec(
        num_scalar_prefetch=5,
        grid=(n_tok // ts,),
        in_specs=[
            pl.BlockSpec((ts, D_MODEL), lambda t, *_: (t, 0)),
            pl.BlockSpec((TOP_K, ts), lambda t, *_: (0, t)),
            pl.BlockSpec((TOP_K, ts), lambda t, *_: (0, t)),
            pl.BlockSpec(memory_space=pl.ANY),
            pl.BlockSpec((1, D_MODEL), lambda t, *_: (0, 0)),
        ],
        out_specs=pl.BlockSpec((ts, D_MODEL), lambda t, *_: (t, 0)),
        scratch_shapes=[
            pltpu.VMEM((2, COMBINE_ROWS, D_MODEL), jnp.bfloat16),
            pltpu.SMEM((2,), jnp.int32),
            pltpu.SemaphoreType.DMA((2,)),
        ])
    return pl.pallas_call(
        _combine_kernel,
        grid_spec=grid_spec,
        out_shape=jax.ShapeDtypeStruct((n_tok, D_MODEL), jnp.float32),
        name="combine_norm",
        compiler_params=pltpu.CompilerParams(dimension_semantics=("arbitrary",),
                                             vmem_limit_bytes=VMEM_LIMIT_BYTES),
    )(span, soff, src, goff, delta, x1, pos, gate, y_rows, g_final)


def _attention_bias():
    slopes = jnp.exp2(-8.0 * jnp.arange(1, N_Q_HEADS + 1, dtype=jnp.float32) / N_Q_HEADS)
    qi = jnp.arange(BLOCK)[:, None]
    kj = jnp.arange(BAND)[None, :]
    dist = qi + BLOCK - kj
    valid = (dist >= 0) & (dist < WINDOW)
    b = -slopes[:, None, None] * dist.astype(jnp.float32)[None]
    b = jnp.where(valid[None], b, -jnp.inf)
    b = b.reshape(N_KV_HEADS, GROUP, BLOCK, BAND).transpose(0, 3, 1, 2)
    return b.reshape(N_KV_HEADS, BAND, GROUP * BLOCK)


def kernel(x, g_mix, w_in, conv_w, sinks, g_attn_out, g_conv_out, w_out, g_ffn, w_router, b_router,
           w1, b1, w2, b2, g_final):
    bsz, s_len, d = x.shape
    n_tok = bsz * s_len
    n_tiles = n_tok // ROUTE_TILE
    n_rows = n_tok * TOP_K + n_tiles * N_EXPERTS * SUBLANES + N_EXPERTS * ROW_BLOCK
    n_blocks = n_rows // ROW_BLOCK
    x2d = x.reshape(n_tok, d)
    bias = _attention_bias()
    l = 0
    wrt = w_router[l].T
    wr_hi = wrt.astype(jnp.bfloat16)
    wr_lo = (wrt - wr_hi.astype(jnp.float32)).astype(jnp.bfloat16)
    wr_split = jnp.concatenate([wr_hi, wr_lo], axis=0)
    x1, h2, pos, gate, cnt = _mixer_call(
        x2d, sinks[l], g_mix[l].reshape(1, d), w_in[l].astype(jnp.bfloat16), conv_w[l], bias,
        g_attn_out[l].reshape(1, ATTN_WIDTH), g_conv_out[l].reshape(1, CONV_WIDTH),
        w_out[l].astype(jnp.bfloat16), g_ffn[l].reshape(1, d), wr_split,
        b_router[l].reshape(N_EXPERTS, 1), s_len)
    size, goff, gdst, blk, used, span, soff, src, delta = _tables_call(cnt, n_blocks)
    size, goff, gdst = size.reshape(-1), goff.reshape(-1), gdst.reshape(-1)
    span, soff, src, delta = span.reshape(-1), soff.reshape(-1), src.reshape(-1), delta.reshape(-1)
    blk, used = blk.reshape(-1), used.reshape(-1)[:1]
    rows = _dispatch_call(size, goff, gdst, used, h2, pos, n_rows)
    y_rows = _expert_call(blk, used, rows, w1[l], b1[l], w2[l], b2[l])
    out = _combine_call(span, soff, src, goff, delta, x1, pos, gate, y_rows, g_final.reshape(1, d))
    return out.reshape(bsz, s_len, d)
```

```python
import functools
import math

import jax
import jax.numpy as jnp
from jax import lax
from jax.experimental import pallas as pl
from jax.experimental.pallas import tpu as pltpu

D_MODEL = 1024
N_Q_HEADS = 8
N_KV_HEADS = 2
HEAD_DIM = 64
GROUP = N_Q_HEADS // N_KV_HEADS
ATTN_WIDTH = N_Q_HEADS * HEAD_DIM
KV_WIDTH = N_KV_HEADS * HEAD_DIM
CONV_WIDTH = D_MODEL - ATTN_WIDTH
QKV_COLS = ATTN_WIDTH + 2 * KV_WIDTH
IN_COLS = QKV_COLS + 3 * CONV_WIDTH
WINDOW = 128
BLOCK = 128
BAND = 2 * BLOCK
N_EXPERTS = 32
TOP_K = 4
D_FF = D_MODEL
SWIGLU_ALPHA = 1.702
SWIGLU_LIMIT = 7.0
EPS = 1e-5

LANES = 128
SUBLANES = 8
VMEM_LIMIT_BYTES = 56 * 1024 * 1024

TOKEN_TILE = 1024
SUB_TILE = 512
ATTN_AHEAD = 1
ROUTE_TILE = 256
ROW_BLOCK = 512
STAGE_ROWS = -(-(TOP_K * ROUTE_TILE + N_EXPERTS * (SUBLANES - 1)) // LANES) * LANES
PACK_ROWS = 16
COMBINE_ROWS = -(-(TOP_K * ROUTE_TILE + N_EXPERTS * (SUBLANES - 1 + PACK_ROWS)) // LANES) * LANES


def _rms(x, g):
    return x * lax.rsqrt(jnp.mean(x * x, axis=-1, keepdims=True) + EPS) * g


def _excl_cumsum(a, axis):
    idx = lax.broadcasted_iota(jnp.int32, a.shape, axis)
    inc = a
    step = 1
    while step < a.shape[axis]:
        inc = inc + jnp.where(idx >= step, pltpu.roll(inc, step, axis), 0)
        step *= 2
    return inc - a


def _mixer_kernel(tiles_per_seq,
                  sink_ref,
                  x_ref, gmix_ref, win_ref, convw_ref, bias_ref, gattn_ref, gconv_ref, wout_ref,
                  gffn_ref, wr_ref, br_ref,
                  x1_ref, h2_ref, pos_ref, gate_ref, cnt_ref,
                  p_scr, h_scr, mix_scr, k_scr, vt_scr, u_scr, attn_t_scr, tri_scr):
    tm = x_ref.shape[0]
    sub_tm = SUB_TILE
    ts = ROUTE_TILE
    i = pl.program_id(0)
    first = (i % tiles_per_seq) == 0
    f32, bf16 = jnp.float32, jnp.bfloat16

    @pl.when(i == 0)
    def _():
        cnt_ref[...] = jnp.zeros_like(cnt_ref)
        r = lax.broadcasted_iota(jnp.int32, (ts, ts), 0)
        c = lax.broadcasted_iota(jnp.int32, (ts, ts), 1)
        tri_scr[...] = jnp.where(r < c, 1.0, 0.0).astype(bf16)

    @pl.when(first)
    def _():
        k_scr[0:BLOCK, :] = jnp.zeros((BLOCK, k_scr.shape[1]), bf16)
        vt_scr[:, 0:BLOCK] = jnp.zeros((KV_WIDTH, BLOCK), bf16)
        u_scr[0:SUBLANES, :] = jnp.zeros((SUBLANES, CONV_WIDTH), f32)

    n_sub = tm // sub_tm
    in_chunks = [(c, c + 2 * LANES) for c in range(0, IN_COLS, 2 * LANES)]
    out_chunks = [(c, c + 2 * LANES) for c in range(0, D_MODEL, 2 * LANES)]
    lane = lax.broadcasted_iota(jnp.int32, (BAND, LANES), 1)
    lo = lane < HEAD_DIM
    qgrp = lax.broadcasted_iota(jnp.int32, (BLOCK, 2 * LANES), 1) // HEAD_DIM
    key_row = lax.broadcasted_iota(jnp.int32, (BAND, GROUP * BLOCK), 0)
    scale = 1.0 / math.sqrt(HEAD_DIM)
    zero_q = jnp.zeros((BLOCK, 2 * LANES), bf16)
    nt = (((1,), (1,)), ((), ()))

    def rows_of(sub):
        return slice(sub * sub_tm, (sub + 1) * sub_tm)

    def prep(sub):
        r0 = rows_of(sub)
        h_scr[r0, :] = _rms(x_ref[r0, :], gmix_ref[...]).astype(bf16)

    def in_proj(sub, c0, c1):
        r0 = rows_of(sub)
        p_scr[r0, c0:c1] = jnp.dot(h_scr[r0, :], win_ref[:, c0:c1], preferred_element_type=f32)

    def kv_stage(sub):
        r0 = rows_of(sub)
        base = sub * sub_tm
        kk = p_scr[r0, ATTN_WIDTH:ATTN_WIDTH + KV_WIDTH]
        kr = pltpu.roll(kk, HEAD_DIM, axis=1)
        k_scr[BLOCK + base:BLOCK + base + sub_tm, :] = jnp.concatenate([kk, kr], axis=1).astype(bf16)
        vt_scr[:, BLOCK + base:BLOCK + base + sub_tm] = jnp.transpose(
            p_scr[r0, ATTN_WIDTH + KV_WIDTH:QKV_COLS]).astype(bf16)

    def attn_scores(j, hh):
        rows = slice(j * BLOCK, (j + 1) * BLOCK)
        kband = k_scr[j * BLOCK:j * BLOCK + BAND, :]
        bk, bkr = kband[:, 0:LANES], kband[:, LANES:2 * LANES]
        k_both = jnp.where(lo, bk, bkr) if hh == 0 else jnp.where(lo, bkr, bk)
        kmat = jnp.concatenate([k_both, k_both], axis=1)
        qh = (p_scr[rows, hh * 2 * LANES:(hh + 1) * 2 * LANES] * scale).astype(bf16)
        qm = jnp.concatenate([jnp.where(qgrp == g, qh, zero_q) for g in range(GROUP)], axis=0)
        s = lax.dot_general(kmat, qm, nt, preferred_element_type=f32) + bias_ref[hh]
        if j == 0:
            s = jnp.where(jnp.logical_and(first, key_row < BLOCK), -jnp.inf, s)
        return s

    def attn_finish(j, hh, s):
        rows = slice(j * BLOCK, (j + 1) * BLOCK)
        vt_band = vt_scr[:, j * BLOCK:j * BLOCK + BAND]
        sink = jnp.concatenate(
            [jnp.full((1, BLOCK), sink_ref[hh * GROUP + g], f32) for g in range(GROUP)], axis=1)
        m = jnp.maximum(jnp.max(s, axis=0, keepdims=True), sink)
        e = jnp.exp(s - m)
        inv = 1.0 / (jnp.sum(e, axis=0, keepdims=True) + jnp.exp(sink - m))
        o_t = jnp.dot(vt_band, e.astype(bf16), preferred_element_type=f32)
        o_t = o_t[hh * HEAD_DIM:(hh + 1) * HEAD_DIM, :] * inv
        for g in range(GROUP):
            head = hh * GROUP + g
            attn_t_scr[head * HEAD_DIM:(head + 1) * HEAD_DIM, rows] = o_t[:, g * BLOCK:(g + 1) * BLOCK]

    def conv_mix(sub):
        r0 = rows_of(sub)
        base = sub * sub_tm
        bg = p_scr[r0, QKV_COLS:QKV_COLS + CONV_WIDTH]
        cg = p_scr[r0, QKV_COLS + CONV_WIDTH:QKV_COLS + 2 * CONV_WIDTH]
        xin = p_scr[r0, QKV_COLS + 2 * CONV_WIDTH:IN_COLS]
        u = cg * xin
        u_scr[SUBLANES + base:SUBLANES + base + sub_tm, :] = u
        u1 = u_scr[SUBLANES - 1 + base:SUBLANES - 1 + base + sub_tm, :]
        u2 = u_scr[SUBLANES - 2 + base:SUBLANES - 2 + base + sub_tm, :]
        y = convw_ref[0:1, :] * u2 + convw_ref[1:2, :] * u1 + convw_ref[2:3, :] * u
        conv = bg * y
        mix_scr[r0, :] = jnp.concatenate(
            [_rms(jnp.transpose(attn_t_scr[:, r0]), gattn_ref[...]), _rms(conv, gconv_ref[...])],
            axis=1).astype(bf16)

    def out_proj(sub, c0, c1):
        r0 = rows_of(sub)
        x1_ref[r0, c0:c1] = x_ref[r0, c0:c1] + jnp.dot(mix_scr[r0, :], wout_ref[:, c0:c1],
                                                      preferred_element_type=f32)

    def route(sub):
        r0 = rows_of(sub)
        h2 = _rms(x1_ref[r0, :], gffn_ref[...])
        h2_hi = h2.astype(bf16)
        h2_ref[r0, :] = h2_hi
        h2_lo = (h2 - h2_hi.astype(f32)).astype(bf16)
        by_hi = lax.dot_general(wr_ref[...], h2_hi, nt, preferred_element_type=f32)
        by_lo = lax.dot_general(wr_ref[0:N_EXPERTS, :], h2_lo, nt, preferred_element_type=f32)
        logits = by_hi[0:N_EXPERTS] + by_hi[N_EXPERTS:] + by_lo + br_ref[...]
        row = lax.broadcasted_iota(jnp.int32, (N_EXPERTS, sub_tm), 0)
        vals, idxs = [], []
        l = logits
        for _ in range(TOP_K):
            m = jnp.max(l, axis=0, keepdims=True)
            idx = jnp.min(jnp.where(l == m, row, N_EXPERTS), axis=0, keepdims=True)
            vals.append(m)
            idxs.append(idx)
            l = jnp.where(row == idx, -jnp.inf, l)
        es = [jnp.exp(v - vals[0]) for v in vals]
        inv = 1.0 / (es[0] + es[1] + es[2] + es[3])
        gate_ref[:, r0] = jnp.concatenate([e * inv for e in es], axis=0)

        onehots = [row == idx for idx in idxs]
        sel = jnp.logical_or(jnp.logical_or(onehots[0], onehots[1]), jnp.logical_or(onehots[2], onehots[3]))
        oh = jnp.where(sel, 1.0, 0.0)
        cnt_lane = lax.broadcasted_iota(jnp.int32, cnt_ref.shape, 1)
        poss = []
        for rt in range(sub_tm // ts):
            sl = slice(rt * ts, (rt + 1) * ts)
            oh_s = oh[:, sl]
            cum = jnp.dot(oh_s.astype(bf16), tri_scr[...], preferred_element_type=f32)
            cnt = jnp.sum(oh_s, axis=1, keepdims=True)
            padded = SUBLANES * jnp.floor((cnt + (SUBLANES - 1.0)) * (1.0 / SUBLANES))
            goff = _excl_cumsum(jnp.broadcast_to(padded, (N_EXPERTS, LANES)), 0)[:, 0:1]
            pos = goff + cum
            poss.append(jnp.concatenate(
                [jnp.sum(jnp.where(o[:, sl], pos, 0.0), axis=0, keepdims=True) for o in onehots], axis=0))
            tile_id = i * (tm // ts) + sub * (sub_tm // ts) + rt
            cnt_ref[...] = jnp.where(cnt_lane == tile_id, cnt.astype(jnp.int32), cnt_ref[...])
        pos_ref[:, r0] = jnp.concatenate(poss, axis=1).astype(jnp.int32)

    def interleave(main, fillers):
        after = [[] for _ in main]
        for n, filler in enumerate(fillers):
            after[n * len(main) // len(fillers)].append(filler)
        for stage, extra in zip(main, after):
            stage()
            for filler in extra:
                filler()

    def attention_stages(sub):
        blocks = range(sub * (sub_tm // BLOCK), (sub + 1) * (sub_tm // BLOCK))
        its = [(j, hh) for j in blocks for hh in range(N_KV_HEADS)]
        scores = {}

        def score(n):
            scores[n] = attn_scores(*its[n])

        def finish(n):
            attn_finish(*its[n], scores.pop(n))

        stages = [functools.partial(score, n) for n in range(min(ATTN_AHEAD, len(its)))]
        for n in range(len(its)):
            if n + ATTN_AHEAD < len(its):
                stages.append(functools.partial(score, n + ATTN_AHEAD))
            stages.append(functools.partial(finish, n))
        return stages

    prep(0)
    for c0, c1 in in_chunks:
        in_proj(0, c0, c1)
    kv_stage(0)
    for sub in range(n_sub):
        fillers = []
        if sub > 0:
            fillers.append(functools.partial(conv_mix, sub - 1))
            fillers += [functools.partial(out_proj, sub - 1, c0, c1) for c0, c1 in out_chunks]
        if sub + 1 < n_sub:
            prep(sub + 1)
            fillers += [functools.partial(in_proj, sub + 1, c0, c1) for c0, c1 in in_chunks]
        interleave(attention_stages(sub), fillers)
        if sub + 1 < n_sub:
            kv_stage(sub + 1)
        if 0 < sub < n_sub - 1:
            route(sub - 1)
    last = n_sub - 1
    conv_mix(last)
    out_proj(last, *out_chunks[0])
    if last > 0:
        route(last - 1)
    for c0, c1 in out_chunks[1:]:
        out_proj(last, c0, c1)
    route(last)

    k_scr[0:BLOCK, :] = k_scr[tm:tm + BLOCK, :]
    vt_scr[:, 0:BLOCK] = vt_scr[:, tm:tm + BLOCK]
    u_scr[0:SUBLANES, :] = u_scr[tm:tm + SUBLANES, :]


def _mixer_call(x2d, sinks, g_mix, w_in, conv_w, bias, g_attn, g_conv, w_out, g_ffn, wr_split, br, seq_len):
    n_tok = x2d.shape[0]
    tm = TOKEN_TILE
    n_tiles = n_tok // tm
    assert n_tok // ROUTE_TILE == LANES, "one lane of the count table per routing tile"
    const = lambda shape: pl.BlockSpec(shape, lambda i, *_: (0,) * len(shape))
    grid_spec = pltpu.PrefetchScalarGridSpec(
        num_scalar_prefetch=0,
        grid=(n_tiles,),
        in_specs=[
            pl.BlockSpec(memory_space=pltpu.SMEM),
            pl.BlockSpec((tm, D_MODEL), lambda i: (i, 0)),
            const((1, D_MODEL)), const((D_MODEL, IN_COLS)), const((3, CONV_WIDTH)),
            const((N_KV_HEADS, BAND, GROUP * BLOCK)),
            const((1, ATTN_WIDTH)), const((1, CONV_WIDTH)), const((D_MODEL, D_MODEL)),
            const((1, D_MODEL)), const((2 * N_EXPERTS, D_MODEL)), const((N_EXPERTS, 1)),
        ],
        out_specs=[
            pl.BlockSpec((tm, D_MODEL), lambda i: (i, 0)),
            pl.BlockSpec((tm, D_MODEL), lambda i: (i, 0)),
            pl.BlockSpec((TOP_K, tm), lambda i: (0, i)),
            pl.BlockSpec((TOP_K, tm), lambda i: (0, i)),
            pl.BlockSpec((N_EXPERTS, LANES), lambda i: (0, 0)),
        ],
        scratch_shapes=[
            pltpu.VMEM((tm, IN_COLS), jnp.float32),
            pltpu.VMEM((tm, D_MODEL), jnp.bfloat16),
            pltpu.VMEM((tm, D_MODEL), jnp.bfloat16),
            pltpu.VMEM((BLOCK + tm, 2 * LANES), jnp.bfloat16),
            pltpu.VMEM((KV_WIDTH, BLOCK + tm), jnp.bfloat16),
            pltpu.VMEM((SUBLANES + tm, CONV_WIDTH), jnp.float32),
            pltpu.VMEM((ATTN_WIDTH, tm), jnp.float32),
            pltpu.VMEM((ROUTE_TILE, ROUTE_TILE), jnp.bfloat16),
        ])
    out_shape = [
        jax.ShapeDtypeStruct((n_tok, D_MODEL), jnp.float32),
        jax.ShapeDtypeStruct((n_tok, D_MODEL), jnp.bfloat16),
        jax.ShapeDtypeStruct((TOP_K, n_tok), jnp.int32),
        jax.ShapeDtypeStruct((TOP_K, n_tok), jnp.float32),
        jax.ShapeDtypeStruct((N_EXPERTS, LANES), jnp.int32),
    ]
    return pl.pallas_call(
        functools.partial(_mixer_kernel, seq_len // tm),
        grid_spec=grid_spec, out_shape=out_shape, name="mixer_router",
        compiler_params=pltpu.CompilerParams(dimension_semantics=("arbitrary",),
                                             vmem_limit_bytes=VMEM_LIMIT_BYTES),
    )(sinks, x2d, g_mix, w_in, conv_w, bias, g_attn, g_conv, w_out, g_ffn, wr_split, br)


def _tables_kernel(cnt_ref, size_ref, goff_ref, gdst_ref, blk_ref, used_ref,
                   span_ref, soff_ref, src_ref, delta_ref):
    size = ((cnt_ref[...] + (SUBLANES - 1)) // SUBLANES) * SUBLANES
    size_ref[...] = size
    goff = _excl_cumsum(size, 0)
    goff_ref[...] = goff
    total = jnp.broadcast_to(jnp.sum(size, axis=1, keepdims=True), size.shape)
    region = ((total + (ROW_BLOCK - 1)) // ROW_BLOCK) * ROW_BLOCK
    pstart = _excl_cumsum(region, 0)
    gdst = pstart + _excl_cumsum(size, 1)
    gdst_ref[...] = gdst
    src = (gdst // PACK_ROWS) * PACK_ROWS
    span = jnp.where(size > 0, ((gdst + size + (PACK_ROWS - 1)) // PACK_ROWS) * PACK_ROWS - src, 0)
    soff = _excl_cumsum(span, 0)
    span_ref[...] = span
    soff_ref[...] = soff
    src_ref[...] = src
    delta_ref[...] = soff + (gdst - src) - goff
    pend = (pstart + region)[:, 0:1]
    start = lax.broadcasted_iota(jnp.int32, (N_EXPERTS, blk_ref.shape[1]), 1) * ROW_BLOCK
    owner = jnp.sum(jnp.where(pend <= start, 1, 0), axis=0, keepdims=True)
    blk_ref[...] = jnp.minimum(owner, N_EXPERTS - 1)
    used_ref[...] = jnp.broadcast_to(jnp.max(pend, axis=0, keepdims=True) // ROW_BLOCK, used_ref.shape)


def _tables_call(cnt, n_blocks):
    nb_pad = -(-n_blocks // LANES) * LANES
    tbl = jax.ShapeDtypeStruct((N_EXPERTS, LANES), jnp.int32)
    return pl.pallas_call(
        _tables_kernel,
        out_shape=[tbl, tbl, tbl, jax.ShapeDtypeStruct((1, nb_pad), jnp.int32),
                   jax.ShapeDtypeStruct((1, LANES), jnp.int32), tbl, tbl, tbl, tbl],
        name="group_tables",
    )(cnt)


def _start_group_copies(tile, size_ref, goff_ref, gdst_ref, sem, make_copy, align):
    def body(e, rows):
        n = pl.multiple_of(size_ref[e * LANES + tile], align)
        @pl.when(n > 0)
        def _():
            off = pl.multiple_of(goff_ref[e * LANES + tile], align)
            dst = pl.multiple_of(gdst_ref[e * LANES + tile], align)
            make_copy(off, dst, n, sem).start()
        return rows + n
    return lax.fori_loop(0, N_EXPERTS, body, jnp.int32(0))


def _wait_rows(rows_hbm, n, sem, align):
    @pl.when(n > 0)
    def _():
        m = pl.multiple_of(n, align)
        half = rows_hbm.shape[0] // (2 * align) * align
        pltpu.make_async_copy(rows_hbm.at[pl.ds(0, m)], rows_hbm.at[pl.ds(half, m)], sem).wait()


def _dispatch_kernel(n_blocks,
                     size_ref, goff_ref, gdst_ref, used_ref,
                     h2_ref, pos_ref, rows_ref,
                     stage, zbuf, inflight, zeroed, sems, zsem):
    t = pl.program_id(0)
    n_tiles = pl.num_programs(0)
    slot = t % 2

    @pl.when(t == 0)
    def _():
        zbuf[...] = jnp.zeros_like(zbuf)
        def zfill(e, n):
            first = gdst_ref[e * LANES]
            last = gdst_ref[e * LANES + LANES - 1] + size_ref[e * LANES + LANES - 1]
            pad = pl.multiple_of((last - first + ROW_BLOCK - 1) // ROW_BLOCK * ROW_BLOCK - (last - first), SUBLANES)
            @pl.when(pad > 0)
            def _():
                pltpu.make_async_copy(zbuf.at[pl.ds(0, pad)],
                                      rows_ref.at[pl.ds(pl.multiple_of(last, SUBLANES), pad)], zsem).start()
            return n + pad
        n_zero = lax.fori_loop(0, N_EXPERTS, zfill, jnp.int32(0))
        def ztail(b, _):
            pltpu.make_async_copy(
                zbuf, rows_ref.at[pl.ds(pl.multiple_of(b * ROW_BLOCK, ROW_BLOCK), ROW_BLOCK)], zsem).start()
            return 0
        lax.fori_loop(used_ref[0], n_blocks, ztail, 0)
        zeroed[0] = n_zero + (n_blocks - used_ref[0]) * ROW_BLOCK

    @pl.when(t >= 2)
    def _():
        _wait_rows(rows_ref, inflight[slot], sems.at[slot], SUBLANES)

    pos = pos_ref[...]
    r = lax.broadcasted_iota(jnp.int32, (STAGE_ROWS, pos.shape[1]), 0)
    hit = jnp.logical_or(jnp.logical_or(r == pos[0:1], r == pos[1:2]),
                         jnp.logical_or(r == pos[2:3], r == pos[3:4]))
    perm = jnp.where(hit, 1.0, 0.0).astype(jnp.bfloat16)
    stage[slot] = jnp.dot(perm, h2_ref[...], preferred_element_type=jnp.float32)

    def make_copy(off, dst, n, sem):
        return pltpu.make_async_copy(stage.at[slot, pl.ds(off, n)], rows_ref.at[pl.ds(dst, n)], sem)
    inflight[slot] = _start_group_copies(t, size_ref, goff_ref, gdst_ref, sems.at[slot], make_copy, SUBLANES)

    @pl.when(t == n_tiles - 1)
    def _():
        _wait_rows(rows_ref, inflight[1 - slot], sems.at[1 - slot], SUBLANES)
        _wait_rows(rows_ref, inflight[slot], sems.at[slot], SUBLANES)
        _wait_rows(rows_ref, zeroed[0], zsem, SUBLANES)


def _dispatch_call(size, goff, gdst, used, h2, pos, n_rows):
    n_tok = h2.shape[0]
    ts = ROUTE_TILE
    n_blocks = n_rows // ROW_BLOCK
    grid_spec = pltpu.PrefetchScalarGridSpec(
        num_scalar_prefetch=4,
        grid=(n_tok // ts,),
        in_specs=[
            pl.BlockSpec((ts, D_MODEL), lambda t, *_: (t, 0)),
            pl.BlockSpec((TOP_K, ts), lambda t, *_: (0, t)),
        ],
        out_specs=pl.BlockSpec(memory_space=pl.ANY),
        scratch_shapes=[
            pltpu.VMEM((2, STAGE_ROWS, D_MODEL), jnp.float32),
            pltpu.VMEM((ROW_BLOCK, D_MODEL), jnp.float32),
            pltpu.SMEM((2,), jnp.int32),
            pltpu.SMEM((1,), jnp.int32),
            pltpu.SemaphoreType.DMA((2,)),
            pltpu.SemaphoreType.DMA(()),
        ])
    return pl.pallas_call(
        functools.partial(_dispatch_kernel, n_blocks),
        grid_spec=grid_spec,
        out_shape=jax.ShapeDtypeStruct((n_rows, D_MODEL), jnp.float32),
        name="dispatch_rows",
        compiler_params=pltpu.CompilerParams(dimension_semantics=("arbitrary",),
                                             vmem_limit_bytes=VMEM_LIMIT_BYTES),
    )(size, goff, gdst, used, h2, pos)


def _expert_kernel(blk_ref, used_ref, x_ref, w1_hbm, b1_ref, w2_hbm, b2_ref, y_ref,
                   w1_f32, w2_f32, w1_bf, w2_bf, state, sems):
    i = pl.program_id(0)
    bf16 = jnp.bfloat16
    e = blk_ref[i]
    used = used_ref[0]

    def weight_copies(expert, slot):
        return (pltpu.make_async_copy(w1_hbm.at[expert], w1_f32.at[slot], sems.at[0, slot]),
                pltpu.make_async_copy(w2_hbm.at[expert], w2_f32.at[slot], sems.at[1, slot]))

    @pl.when(i == 0)
    def _():
        state[0] = jnp.int32(-1)
        state[1] = jnp.int32(0)
        @pl.when(used > 0)
        def _():
            for cp in weight_copies(e, 0):
                cp.start()

    @pl.when(jnp.logical_and(i < used, e != state[0]))
    def _():
        slot = state[1]
        for cp in weight_copies(e, slot):
            cp.wait()
        w1_bf[...] = w1_f32[slot].astype(bf16)
        w2_bf[...] = w2_f32[slot].astype(bf16)
        nxt = lax.while_loop(lambda j: jnp.logical_and(j < used, blk_ref[jnp.minimum(j, used - 1)] == e),
                             lambda j: j + 1, i + 1)
        @pl.when(nxt < used)
        def _():
            for cp in weight_copies(blk_ref[nxt], 1 - slot):
                cp.start()
        state[0] = e
        state[1] = 1 - slot

    @pl.when(i >= used)
    def _():
        y_ref[...] = jnp.zeros_like(y_ref)

    @pl.when(i < used)
    def _():
        hu = jnp.dot(x_ref[...].astype(bf16), w1_bf[...], preferred_element_type=jnp.float32) + b1_ref[0]
        glu = jnp.minimum(hu[:, :D_FF], SWIGLU_LIMIT)
        lin = jnp.clip(hu[:, D_FF:], -SWIGLU_LIMIT, SWIGLU_LIMIT)
        act = glu * (1.0 / (1.0 + jnp.exp(-SWIGLU_ALPHA * glu))) * (lin + 1.0)
        y = jnp.dot(act.astype(bf16), w2_bf[...], preferred_element_type=jnp.float32) + b2_ref[0]
        y_ref[...] = y.astype(bf16)


def _expert_call(blk, used, rows, w1, b1, w2, b2):
    n_rows = rows.shape[0]
    n_blocks = n_rows // ROW_BLOCK
    last = lambda i, b, u: jnp.maximum(jnp.minimum(i, u[0] - 1), 0)
    grid_spec = pltpu.PrefetchScalarGridSpec(
        num_scalar_prefetch=2,
        grid=(n_blocks,),
        in_specs=[
            pl.BlockSpec((ROW_BLOCK, D_MODEL), lambda i, b, u: (last(i, b, u), 0)),
            pl.BlockSpec(memory_space=pl.ANY),
            pl.BlockSpec((1, 1, 2 * D_FF), lambda i, b, u: (b[i], 0, 0)),
            pl.BlockSpec(memory_space=pl.ANY),
            pl.BlockSpec((1, 1, D_MODEL), lambda i, b, u: (b[i], 0, 0)),
        ],
        out_specs=pl.BlockSpec((ROW_BLOCK, D_MODEL), lambda i, b, u: (i, 0)),
        scratch_shapes=[
            pltpu.VMEM((2, D_MODEL, 2 * D_FF), jnp.float32),
            pltpu.VMEM((2, D_FF, D_MODEL), jnp.float32),
            pltpu.VMEM((D_MODEL, 2 * D_FF), jnp.bfloat16),
            pltpu.VMEM((D_FF, D_MODEL), jnp.bfloat16),
            pltpu.SMEM((2,), jnp.int32),
            pltpu.SemaphoreType.DMA((2, 2)),
        ])
    return pl.pallas_call(
        _expert_kernel,
        grid_spec=grid_spec,
        out_shape=jax.ShapeDtypeStruct((n_rows, D_MODEL), jnp.bfloat16),
        name="expert_ffn",
        compiler_params=pltpu.CompilerParams(dimension_semantics=("arbitrary",),
                                             vmem_limit_bytes=VMEM_LIMIT_BYTES),
    )(blk, used, rows, w1, b1.reshape(N_EXPERTS, 1, 2 * D_FF), w2, b2.reshape(N_EXPERTS, 1, D_MODEL))


def _combine_kernel(span_ref, soff_ref, src_ref, goff_ref, delta_ref,
                    x1_ref, pos_ref, gate_ref, y_ref, gfin_ref, out_ref,
                    stage, inflight, sems):
    t = pl.program_id(0)
    n_tiles = pl.num_programs(0)
    slot = t % 2

    def fetch(tile, slot_):
        def make_copy(off, src, n, sem):
            return pltpu.make_async_copy(y_ref.at[pl.ds(src, n)], stage.at[slot_, pl.ds(off, n)], sem)
        inflight[slot_] = _start_group_copies(tile, span_ref, soff_ref, src_ref, sems.at[slot_], make_copy,
                                              PACK_ROWS)

    @pl.when(t == 0)
    def _():
        stage[...] = jnp.zeros_like(stage)
        fetch(0, 0)

    @pl.when(t + 1 < n_tiles)
    def _():
        fetch(t + 1, 1 - slot)

    _wait_rows(y_ref, inflight[slot], sems.at[slot], PACK_ROWS)

    pos = pos_ref[...]
    shift = jnp.zeros_like(pos)
    prev = jnp.int32(0)
    for e in range(N_EXPERTS):
        d = delta_ref[e * LANES + t]
        shift = shift + jnp.where(pos >= goff_ref[e * LANES + t], d - prev, 0)
        prev = d
    pos = pos + shift
    gate = gate_ref[...]
    r = lax.broadcasted_iota(jnp.int32, (COMBINE_ROWS, pos.shape[1]), 0)
    w = jnp.zeros(r.shape, jnp.float32)
    for k in range(TOP_K):
        w = jnp.where(r == pos[k:k + 1], gate[k:k + 1], w)
    moe = lax.dot_general(w.astype(jnp.bfloat16), stage[slot],
                          (((0,), (0,)), ((), ())), preferred_element_type=jnp.float32)
    out_ref[...] = _rms(x1_ref[...] + moe, gfin_ref[...])


def _combine_call(span, soff, src, goff, delta, x1, pos, gate, y_rows, g_final):
    n_tok = x1.shape[0]
    ts = ROUTE_TILE
    grid_spec = pltpu.PrefetchScalarGridSpec(
        num_scalar_prefetch=5,
        grid=(n_tok // ts,),
        in_specs=[
            pl.BlockSpec((ts, D_MODEL), lambda t, *_: (t, 0)),
            pl.BlockSpec((TOP_K, ts), lambda t, *_: (0, t)),
            pl.BlockSpec((TOP_K, ts), lambda t, *_: (0, t)),
            pl.BlockSpec(memory_space=pl.ANY),
            pl.BlockSpec((1, D_MODEL), lambda t, *_: (0, 0)),
        ],
        out_specs=pl.BlockSpec((ts, D_MODEL), lambda t, *_: (t, 0)),
        scratch_shapes=[
            pltpu.VMEM((2, COMBINE_ROWS, D_MODEL), jnp.bfloat16),
            pltpu.SMEM((2,), jnp.int32),
            pltpu.SemaphoreType.DMA((2,)),
        ])
    return pl.pallas_call(
        _combine_kernel,
        grid_spec=grid_spec,
        out_shape=jax.ShapeDtypeStruct((n_tok, D_MODEL), jnp.float32),
        name="combine_norm",
        compiler_params=pltpu.CompilerParams(dimension_semantics=("arbitrary",),
                                             vmem_limit_bytes=VMEM_LIMIT_BYTES),
    )(span, soff, src, goff, delta, x1, pos, gate, y_rows, g_final)


def _attention_bias():
    slopes = jnp.exp2(-8.0 * jnp.arange(1, N_Q_HEADS + 1, dtype=jnp.float32) / N_Q_HEADS)
    qi = jnp.arange(BLOCK)[:, None]
    kj = jnp.arange(BAND)[None, :]
    dist = qi + BLOCK - kj
    valid = (dist >= 0) & (dist < WINDOW)
    b = -slopes[:, None, None] * dist.astype(jnp.float32)[None]
    b = jnp.where(valid[None], b, -jnp.inf)
    b = b.reshape(N_KV_HEADS, GROUP, BLOCK, BAND).transpose(0, 3, 1, 2)
    return b.reshape(N_KV_HEADS, BAND, GROUP * BLOCK)


def kernel(x, g_mix, w_in, conv_w, sinks, g_attn_out, g_conv_out, w_out, g_ffn, w_router, b_router,
           w1, b1, w2, b2, g_final):
    bsz, s_len, d = x.shape
    n_tok = bsz * s_len
    n_tiles = n_tok // ROUTE_TILE
    n_rows = n_tok * TOP_K + n_tiles * N_EXPERTS * SUBLANES + N_EXPERTS * ROW_BLOCK
    n_blocks = n_rows // ROW_BLOCK
    x2d = x.reshape(n_tok, d)
    bias = _attention_bias()
    l = 0
    wrt = w_router[l].T
    wr_hi = wrt.astype(jnp.bfloat16)
    wr_lo = (wrt - wr_hi.astype(jnp.float32)).astype(jnp.bfloat16)
    wr_split = jnp.concatenate([wr_hi, wr_lo], axis=0)
    x1, h2, pos, gate, cnt = _mixer_call(
        x2d, sinks[l], g_mix[l].reshape(1, d), w_in[l].astype(jnp.bfloat16), conv_w[l], bias,
        g_attn_out[l].reshape(1, ATTN_WIDTH), g_conv_out[l].reshape(1, CONV_WIDTH),
        w_out[l].astype(jnp.bfloat16), g_ffn[l].reshape(1, d), wr_split,
        b_router[l].reshape(N_EXPERTS, 1), s_len)
    size, goff, gdst, blk, used, span, soff, src, delta = _tables_call(cnt, n_blocks)
    size, goff, gdst = size.reshape(-1), goff.reshape(-1), gdst.reshape(-1)
    span, soff, src, delta = span.reshape(-1), soff.reshape(-1), src.reshape(-1), delta.reshape(-1)
    blk, used = blk.reshape(-1), used.reshape(-1)[:1]
    rows = _dispatch_call(size, goff, gdst, used, h2, pos, n_rows)
    y_rows = _expert_call(blk, used, rows, w1[l], b1[l], w2[l], b2[l])
    out = _combine_call(span, soff, src, goff, delta, x1, pos, gate, y_rows, g_final.reshape(1, d))
    return out.reshape(bsz, s_len, d)
```

```python
import functools
import math

import jax
import jax.numpy as jnp
from jax import lax
from jax.experimental import pallas as pl
from jax.experimental.pallas import tpu as pltpu

D_MODEL = 1024
N_Q_HEADS = 8
N_KV_HEADS = 2
HEAD_DIM = 64
GROUP = N_Q_HEADS // N_KV_HEADS
ATTN_WIDTH = N_Q_HEADS * HEAD_DIM
KV_WIDTH = N_KV_HEADS * HEAD_DIM
CONV_WIDTH = D_MODEL - ATTN_WIDTH
QKV_COLS = ATTN_WIDTH + 2 * KV_WIDTH
IN_COLS = QKV_COLS + 3 * CONV_WIDTH
WINDOW = 128
BLOCK = 128
BAND = 2 * BLOCK
N_EXPERTS = 32
TOP_K = 4
D_FF = D_MODEL
SWIGLU_ALPHA = 1.702
SWIGLU_LIMIT = 7.0
EPS = 1e-5

LANES = 128
SUBLANES = 8
VMEM_LIMIT_BYTES = 56 * 1024 * 1024

TOKEN_TILE = 1024
SUB_TILE = 512
ATTN_AHEAD = 1
ROUTE_TILE = 256
ROW_BLOCK = 512
STAGE_ROWS = -(-(TOP_K * ROUTE_TILE + N_EXPERTS * (SUBLANES - 1)) // LANES) * LANES
PACK_ROWS = 16
COMBINE_ROWS = -(-(TOP_K * ROUTE_TILE + N_EXPERTS * (SUBLANES - 1 + PACK_ROWS)) // LANES) * LANES


def _rms(x, g):
    return x * lax.rsqrt(jnp.mean(x * x, axis=-1, keepdims=True) + EPS) * g


def _excl_cumsum(a, axis):
    idx = lax.broadcasted_iota(jnp.int32, a.shape, axis)
    inc = a
    step = 1
    while step < a.shape[axis]:
        inc = inc + jnp.where(idx >= step, pltpu.roll(inc, step, axis), 0)
        step *= 2
    return inc - a


def _mixer_kernel(tiles_per_seq,
                  sink_ref,
                  x_ref, gmix_ref, win_ref, convw_ref, bias_ref, gattn_ref, gconv_ref, wout_ref,
                  gffn_ref, wr_ref, br_ref,
                  x1_ref, h2_ref, pos_ref, gate_ref, cnt_ref,
                  p_scr, h_scr, mix_scr, k_scr, vt_scr, u_scr, attn_t_scr, tri_scr):
    tm = x_ref.shape[0]
    sub_tm = SUB_TILE
    ts = ROUTE_TILE
    i = pl.program_id(0)
    first = (i % tiles_per_seq) == 0
    f32, bf16 = jnp.float32, jnp.bfloat16

    @pl.when(i == 0)
    def _():
        cnt_ref[...] = jnp.zeros_like(cnt_ref)
        r = lax.broadcasted_iota(jnp.int32, (ts, ts), 0)
        c = lax.broadcasted_iota(jnp.int32, (ts, ts), 1)
        tri_scr[...] = jnp.where(r < c, 1.0, 0.0).astype(bf16)

    @pl.when(first)
    def _():
        k_scr[0:BLOCK, :] = jnp.zeros((BLOCK, k_scr.shape[1]), bf16)
        vt_scr[:, 0:BLOCK] = jnp.zeros((KV_WIDTH, BLOCK), bf16)
        u_scr[0:SUBLANES, :] = jnp.zeros((SUBLANES, CONV_WIDTH), f32)

    n_sub = tm // sub_tm
    in_chunks = [(c, c + 2 * LANES) for c in range(0, IN_COLS, 2 * LANES)]
    out_chunks = [(c, c + 2 * LANES) for c in range(0, D_MODEL, 2 * LANES)]
    lane = lax.broadcasted_iota(jnp.int32, (BAND, LANES), 1)
    lo = lane < HEAD_DIM
    qgrp = lax.broadcasted_iota(jnp.int32, (BLOCK, 2 * LANES), 1) // HEAD_DIM
    key_row = lax.broadcasted_iota(jnp.int32, (BAND, GROUP * BLOCK), 0)
    scale = 1.0 / math.sqrt(HEAD_DIM)
    zero_q = jnp.zeros((BLOCK, 2 * LANES), bf16)
    nt = (((1,), (1,)), ((), ()))

    def rows_of(sub):
        return slice(sub * sub_tm, (sub + 1) * sub_tm)

    def prep(sub):
        r0 = rows_of(sub)
        h_scr[r0, :] = _rms(x_ref[r0, :], gmix_ref[...]).astype(bf16)

    def in_proj(sub, c0, c1):
        r0 = rows_of(sub)
        p_scr[r0, c0:c1] = jnp.dot(h_scr[r0, :], win_ref[:, c0:c1], preferred_element_type=f32)

    def kv_stage(sub):
        r0 = rows_of(sub)
        base = sub * sub_tm
        kk = p_scr[r0, ATTN_WIDTH:ATTN_WIDTH + KV_WIDTH]
        kr = pltpu.roll(kk, HEAD_DIM, axis=1)
        k_scr[BLOCK + base:BLOCK + base + sub_tm, :] = jnp.concatenate([kk, kr], axis=1).astype(bf16)
        vt_scr[:, BLOCK + base:BLOCK + base + sub_tm] = jnp.transpose(
            p_scr[r0, ATTN_WIDTH + KV_WIDTH:QKV_COLS]).astype(bf16)

    def attn_scores(j, hh):
        rows = slice(j * BLOCK, (j + 1) * BLOCK)
        kband = k_scr[j * BLOCK:j * BLOCK + BAND, :]
        bk, bkr = kband[:, 0:LANES], kband[:, LANES:2 * LANES]
        k_both = jnp.where(lo, bk, bkr) if hh == 0 else jnp.where(lo, bkr, bk)
        kmat = jnp.concatenate([k_both, k_both], axis=1)
        qh = (p_scr[rows, hh * 2 * LANES:(hh + 1) * 2 * LANES] * scale).astype(bf16)
        qm = jnp.concatenate([jnp.where(qgrp == g, qh, zero_q) for g in range(GROUP)], axis=0)
        s = lax.dot_general(kmat, qm, nt, preferred_element_type=f32) + bias_ref[hh]
        if j == 0:
            s = jnp.where(jnp.logical_and(first, key_row < BLOCK), -jnp.inf, s)
        return s

    def attn_finish(j, hh, s):
        rows = slice(j * BLOCK, (j + 1) * BLOCK)
        vt_band = vt_scr[:, j * BLOCK:j * BLOCK + BAND]
        sink = jnp.concatenate(
            [jnp.full((1, BLOCK), sink_ref[hh * GROUP + g], f32) for g in range(GROUP)], axis=1)
        m = jnp.maximum(jnp.max(s, axis=0, keepdims=True), sink)
        e = jnp.exp(s - m)
        inv = 1.0 / (jnp.sum(e, axis=0, keepdims=True) + jnp.exp(sink - m))
        o_t = jnp.dot(vt_band, e.astype(bf16), preferred_element_type=f32)
        o_t = o_t[hh * HEAD_DIM:(hh + 1) * HEAD_DIM, :] * inv
        for g in range(GROUP):
            head = hh * GROUP + g
            attn_t_scr[head * HEAD_DIM:(head + 1) * HEAD_DIM, rows] = o_t[:, g * BLOCK:(g + 1) * BLOCK]

    def conv_mix(sub):
        r0 = rows_of(sub)
        base = sub * sub_tm
        bg = p_scr[r0, QKV_COLS:QKV_COLS + CONV_WIDTH]
        cg = p_scr[r0, QKV_COLS + CONV_WIDTH:QKV_COLS + 2 * CONV_WIDTH]
        xin = p_scr[r0, QKV_COLS + 2 * CONV_WIDTH:IN_COLS]
        u = cg * xin
        u_scr[SUBLANES + base:SUBLANES + base + sub_tm, :] = u
        u1 = u_scr[SUBLANES - 1 + base:SUBLANES - 1 + base + sub_tm, :]
        u2 = u_scr[SUBLANES - 2 + base:SUBLANES - 2 + base + sub_tm, :]
        y = convw_ref[0:1, :] * u2 + convw_ref[1:2, :] * u1 + convw_ref[2:3, :] * u
        conv = bg * y
        mix_scr[r0, :] = jnp.concatenate(
            [_rms(jnp.transpose(attn_t_scr[:, r0]), gattn_ref[...]), _rms(conv, gconv_ref[...])],
            axis=1).astype(bf16)

    def out_proj(sub, c0, c1):
        r0 = rows_of(sub)
        x1_ref[r0, c0:c1] = x_ref[r0, c0:c1] + jnp.dot(mix_scr[r0, :], wout_ref[:, c0:c1],
                                                      preferred_element_type=f32)

    def route(sub):
        r0 = rows_of(sub)
        h2 = _rms(x1_ref[r0, :], gffn_ref[...])
        h2_hi = h2.astype(bf16)
        h2_ref[r0, :] = h2_hi
        h2_lo = (h2 - h2_hi.astype(f32)).astype(bf16)
        by_hi = lax.dot_general(wr_ref[...], h2_hi, nt, preferred_element_type=f32)
        by_lo = lax.dot_general(wr_ref[0:N_EXPERTS, :], h2_lo, nt, preferred_element_type=f32)
        logits = by_hi[0:N_EXPERTS] + by_hi[N_EXPERTS:] + by_lo + br_ref[...]
        row = lax.broadcasted_iota(jnp.int32, (N_EXPERTS, sub_tm), 0)
        vals, idxs = [], []
        l = logits
        for _ in range(TOP_K):
            m = jnp.max(l, axis=0, keepdims=True)
            idx = jnp.min(jnp.where(l == m, row, N_EXPERTS), axis=0, keepdims=True)
            vals.append(m)
            idxs.append(idx)
            l = jnp.where(row == idx, -jnp.inf, l)
        es = [jnp.exp(v - vals[0]) for v in vals]
        inv = 1.0 / (es[0] + es[1] + es[2] + es[3])
        gate_ref[:, r0] = jnp.concatenate([e * inv for e in es], axis=0)

        onehots = [row == idx for idx in idxs]
        sel = jnp.logical_or(jnp.logical_or(onehots[0], onehots[1]), jnp.logical_or(onehots[2], onehots[3]))
        oh = jnp.where(sel, 1.0, 0.0)
        cnt_lane = lax.broadcasted_iota(jnp.int32, cnt_ref.shape, 1)
        poss = []
        for rt in range(sub_tm // ts):
            sl = slice(rt * ts, (rt + 1) * ts)
            oh_s = oh[:, sl]
            cum = jnp.dot(oh_s.astype(bf16), tri_scr[...], preferred_element_type=f32)
            cnt = jnp.sum(oh_s, axis=1, keepdims=True)
            padded = SUBLANES * jnp.floor((cnt + (SUBLANES - 1.0)) * (1.0 / SUBLANES))
            goff = _excl_cumsum(jnp.broadcast_to(padded, (N_EXPERTS, LANES)), 0)[:, 0:1]
            pos = goff + cum
            poss.append(jnp.concatenate(
                [jnp.sum(jnp.where(o[:, sl], pos, 0.0), axis=0, keepdims=True) for o in onehots], axis=0))
            tile_id = i * (tm // ts) + sub * (sub_tm // ts) + rt
            cnt_ref[...] = jnp.where(cnt_lane == tile_id, cnt.astype(jnp.int32), cnt_ref[...])
        pos_ref[:, r0] = jnp.concatenate(poss, axis=1).astype(jnp.int32)

    def interleave(main, fillers):
        after = [[] for _ in main]
        for n, filler in enumerate(fillers):
            after[n * len(main) // len(fillers)].append(filler)
        for stage, extra in zip(main, after):
            stage()
            for filler in extra:
                filler()

    def attention_stages(sub):
        blocks = range(sub * (sub_tm // BLOCK), (sub + 1) * (sub_tm // BLOCK))
        its = [(j, hh) for j in blocks for hh in range(N_KV_HEADS)]
        scores = {}

        def score(n):
            scores[n] = attn_scores(*its[n])

        def finish(n):
            attn_finish(*its[n], scores.pop(n))

        stages = [functools.partial(score, n) for n in range(min(ATTN_AHEAD, len(its)))]
        for n in range(len(its)):
            if n + ATTN_AHEAD < len(its):
                stages.append(functools.partial(score, n + ATTN_AHEAD))
            stages.append(functools.partial(finish, n))
        return stages

    prep(0)
    for c0, c1 in in_chunks:
        in_proj(0, c0, c1)
    kv_stage(0)
    for sub in range(n_sub):
        fillers = []
        if sub > 0:
            fillers.append(functools.partial(conv_mix, sub - 1))
            fillers += [functools.partial(out_proj, sub - 1, c0, c1) for c0, c1 in out_chunks]
        if sub + 1 < n_sub:
            prep(sub + 1)
            fillers += [functools.partial(in_proj, sub + 1, c0, c1) for c0, c1 in in_chunks]
        interleave(attention_stages(sub), fillers)
        if sub + 1 < n_sub:
            kv_stage(sub + 1)
        if 0 < sub < n_sub - 1:
            route(sub - 1)
    last = n_sub - 1
    conv_mix(last)
    out_proj(last, *out_chunks[0])
    if last > 0:
        route(last - 1)
    for c0, c1 in out_chunks[1:]:
        out_proj(last, c0, c1)
    route(last)

    k_scr[0:BLOCK, :] = k_scr[tm:tm + BLOCK, :]
    vt_scr[:, 0:BLOCK] = vt_scr[:, tm:tm + BLOCK]
    u_scr[0:SUBLANES, :] = u_scr[tm:tm + SUBLANES, :]


def _mixer_call(x2d, sinks, g_mix, w_in, conv_w, bias, g_attn, g_conv, w_out, g_ffn, wr_split, br, seq_len):
    n_tok = x2d.shape[0]
    tm = TOKEN_TILE
    n_tiles = n_tok // tm
    assert n_tok // ROUTE_TILE == LANES, "one lane of the count table per routing tile"
    const = lambda shape: pl.BlockSpec(shape, lambda i, *_: (0,) * len(shape))
    grid_spec = pltpu.PrefetchScalarGridSpec(
        num_scalar_prefetch=0,
        grid=(n_tiles,),
        in_specs=[
            pl.BlockSpec(memory_space=pltpu.SMEM),
            pl.BlockSpec((tm, D_MODEL), lambda i: (i, 0)),
            const((1, D_MODEL)), const((D_MODEL, IN_COLS)), const((3, CONV_WIDTH)),
            const((N_KV_HEADS, BAND, GROUP * BLOCK)),
            const((1, ATTN_WIDTH)), const((1, CONV_WIDTH)), const((D_MODEL, D_MODEL)),
            const((1, D_MODEL)), const((2 * N_EXPERTS, D_MODEL)), const((N_EXPERTS, 1)),
        ],
        out_specs=[
            pl.BlockSpec((tm, D_MODEL), lambda i: (i, 0)),
            pl.BlockSpec((tm, D_MODEL), lambda i: (i, 0)),
            pl.BlockSpec((TOP_K, tm), lambda i: (0, i)),
            pl.BlockSpec((TOP_K, tm), lambda i: (0, i)),
            pl.BlockSpec((N_EXPERTS, LANES), lambda i: (0, 0)),
        ],
        scratch_shapes=[
            pltpu.VMEM((tm, IN_COLS), jnp.float32),
            pltpu.VMEM((tm, D_MODEL), jnp.bfloat16),
            pltpu.VMEM((tm, D_MODEL), jnp.bfloat16),
            pltpu.VMEM((BLOCK + tm, 2 * LANES), jnp.bfloat16),
            pltpu.VMEM((KV_WIDTH, BLOCK + tm), jnp.bfloat16),
            pltpu.VMEM((SUBLANES + tm, CONV_WIDTH), jnp.float32),
            pltpu.VMEM((ATTN_WIDTH, tm), jnp.float32),
            pltpu.VMEM((ROUTE_TILE, ROUTE_TILE), jnp.bfloat16),
        ])
    out_shape = [
        jax.ShapeDtypeStruct((n_tok, D_MODEL), jnp.float32),
        jax.ShapeDtypeStruct((n_tok, D_MODEL), jnp.bfloat16),
        jax.ShapeDtypeStruct((TOP_K, n_tok), jnp.int32),
        jax.ShapeDtypeStruct((TOP_K, n_tok), jnp.float32),
        jax.ShapeDtypeStruct((N_EXPERTS, LANES), jnp.int32),
    ]
    return pl.pallas_call(
        functools.partial(_mixer_kernel, seq_len // tm),
        grid_spec=grid_spec, out_shape=out_shape, name="mixer_router",
        compiler_params=pltpu.CompilerParams(dimension_semantics=("arbitrary",),
                                             vmem_limit_bytes=VMEM_LIMIT_BYTES),
    )(sinks, x2d, g_mix, w_in, conv_w, bias, g_attn, g_conv, w_out, g_ffn, wr_split, br)


def _tables_kernel(cnt_ref, size_ref, goff_ref, gdst_ref, blk_ref, used_ref,
                   span_ref, soff_ref, src_ref, delta_ref):
    size = ((cnt_ref[...] + (SUBLANES - 1)) // SUBLANES) * SUBLANES
    size_ref[...] = size
    goff = _excl_cumsum(size, 0)
    goff_ref[...] = goff
    total = jnp.broadcast_to(jnp.sum(size, axis=1, keepdims=True), size.shape)
    region = ((total + (ROW_BLOCK - 1)) // ROW_BLOCK) * ROW_BLOCK
    pstart = _excl_cumsum(region, 0)
    gdst = pstart + _excl_cumsum(size, 1)
    gdst_ref[...] = gdst
    src = (gdst // PACK_ROWS) * PACK_ROWS
    span = jnp.where(size > 0, ((gdst + size + (PACK_ROWS - 1)) // PACK_ROWS) * PACK_ROWS - src, 0)
    soff = _excl_cumsum(span, 0)
    span_ref[...] = span
    soff_ref[...] = soff
    src_ref[...] = src
    delta_ref[...] = soff + (gdst - src) - goff
    pend = (pstart + region)[:, 0:1]
    start = lax.broadcasted_iota(jnp.int32, (N_EXPERTS, blk_ref.shape[1]), 1) * ROW_BLOCK
    owner = jnp.sum(jnp.where(pend <= start, 1, 0), axis=0, keepdims=True)
    blk_ref[0:1, :] = jnp.minimum(owner, N_EXPERTS - 1)
    first_row, last_row = pstart[:, 0:1], (pstart + total)[:, 0:1]
    inside = jnp.logical_and(first_row <= start, start < pend)
    blk_ref[1:2, :] = jnp.sum(jnp.where(inside, jnp.minimum(last_row - start, ROW_BLOCK), 0),
                              axis=0, keepdims=True)
    used_ref[...] = jnp.broadcast_to(jnp.max(pend, axis=0, keepdims=True) // ROW_BLOCK, used_ref.shape)


def _tables_call(cnt, n_blocks):
    nb_pad = -(-n_blocks // LANES) * LANES
    tbl = jax.ShapeDtypeStruct((N_EXPERTS, LANES), jnp.int32)
    return pl.pallas_call(
        _tables_kernel,
        out_shape=[tbl, tbl, tbl, jax.ShapeDtypeStruct((2, nb_pad), jnp.int32),
                   jax.ShapeDtypeStruct((1, LANES), jnp.int32), tbl, tbl, tbl, tbl],
        name="group_tables",
    )(cnt)


def _start_group_copies(tile, size_ref, goff_ref, gdst_ref, sem, make_copy, align):
    def body(e, rows):
        n = pl.multiple_of(size_ref[e * LANES + tile], align)
        @pl.when(n > 0)
        def _():
            off = pl.multiple_of(goff_ref[e * LANES + tile], align)
            dst = pl.multiple_of(gdst_ref[e * LANES + tile], align)
            make_copy(off, dst, n, sem).start()
        return rows + n
    return lax.fori_loop(0, N_EXPERTS, body, jnp.int32(0))


def _wait_rows(rows_hbm, n, sem, align):
    @pl.when(n > 0)
    def _():
        m = pl.multiple_of(n, align)
        half = rows_hbm.shape[0] // (2 * align) * align
        pltpu.make_async_copy(rows_hbm.at[pl.ds(0, m)], rows_hbm.at[pl.ds(half, m)], sem).wait()


def _dispatch_kernel(n_blocks,
                     size_ref, goff_ref, gdst_ref, used_ref,
                     h2_ref, pos_ref, rows_ref,
                     stage, zbuf, inflight, zeroed, sems, zsem):
    t = pl.program_id(0)
    n_tiles = pl.num_programs(0)
    slot = t % 2

    @pl.when(t == 0)
    def _():
        zbuf[...] = jnp.zeros_like(zbuf)
        def zfill(e, n):
            first = gdst_ref[e * LANES]
            last = gdst_ref[e * LANES + LANES - 1] + size_ref[e * LANES + LANES - 1]
            pad = pl.multiple_of((last - first + ROW_BLOCK - 1) // ROW_BLOCK * ROW_BLOCK - (last - first), SUBLANES)
            @pl.when(pad > 0)
            def _():
                pltpu.make_async_copy(zbuf.at[pl.ds(0, pad)],
                                      rows_ref.at[pl.ds(pl.multiple_of(last, SUBLANES), pad)], zsem).start()
            return n + pad
        n_zero = lax.fori_loop(0, N_EXPERTS, zfill, jnp.int32(0))
        def ztail(b, _):
            pltpu.make_async_copy(
                zbuf, rows_ref.at[pl.ds(pl.multiple_of(b * ROW_BLOCK, ROW_BLOCK), ROW_BLOCK)], zsem).start()
            return 0
        lax.fori_loop(used_ref[0], n_blocks, ztail, 0)
        zeroed[0] = n_zero + (n_blocks - used_ref[0]) * ROW_BLOCK

    @pl.when(t >= 2)
    def _():
        _wait_rows(rows_ref, inflight[slot], sems.at[slot], SUBLANES)

    pos = pos_ref[...]
    r = lax.broadcasted_iota(jnp.int32, (STAGE_ROWS, pos.shape[1]), 0)
    hit = jnp.logical_or(jnp.logical_or(r == pos[0:1], r == pos[1:2]),
                         jnp.logical_or(r == pos[2:3], r == pos[3:4]))
    perm = jnp.where(hit, 1.0, 0.0).astype(jnp.bfloat16)
    stage[slot] = jnp.dot(perm, h2_ref[...], preferred_element_type=jnp.float32)

    def make_copy(off, dst, n, sem):
        return pltpu.make_async_copy(stage.at[slot, pl.ds(off, n)], rows_ref.at[pl.ds(dst, n)], sem)
    inflight[slot] = _start_group_copies(t, size_ref, goff_ref, gdst_ref, sems.at[slot], make_copy, SUBLANES)

    @pl.when(t == n_tiles - 1)
    def _():
        _wait_rows(rows_ref, inflight[1 - slot], sems.at[1 - slot], SUBLANES)
        _wait_rows(rows_ref, inflight[slot], sems.at[slot], SUBLANES)
        _wait_rows(rows_ref, zeroed[0], zsem, SUBLANES)


def _dispatch_call(size, goff, gdst, used, h2, pos, n_rows):
    n_tok = h2.shape[0]
    ts = ROUTE_TILE
    n_blocks = n_rows // ROW_BLOCK
    grid_spec = pltpu.PrefetchScalarGridSpec(
        num_scalar_prefetch=4,
        grid=(n_tok // ts,),
        in_specs=[
            pl.BlockSpec((ts, D_MODEL), lambda t, *_: (t, 0)),
            pl.BlockSpec((TOP_K, ts), lambda t, *_: (0, t)),
        ],
        out_specs=pl.BlockSpec(memory_space=pl.ANY),
        scratch_shapes=[
            pltpu.VMEM((2, STAGE_ROWS, D_MODEL), jnp.float32),
            pltpu.VMEM((ROW_BLOCK, D_MODEL), jnp.float32),
            pltpu.SMEM((2,), jnp.int32),
            pltpu.SMEM((1,), jnp.int32),
            pltpu.SemaphoreType.DMA((2,)),
            pltpu.SemaphoreType.DMA(()),
        ])
    return pl.pallas_call(
        functools.partial(_dispatch_kernel, n_blocks),
        grid_spec=grid_spec,
        out_shape=jax.ShapeDtypeStruct((n_rows, D_MODEL), jnp.float32),
        name="dispatch_rows",
        compiler_params=pltpu.CompilerParams(dimension_semantics=("arbitrary",),
                                             vmem_limit_bytes=VMEM_LIMIT_BYTES),
    )(size, goff, gdst, used, h2, pos)


def _expert_kernel(blk_ref, used_ref, valid_ref, x_ref, w1_hbm, b1_ref, w2_hbm, b2_ref, y_ref,
                   w1_f32, w2_f32, w1_bf, w2_bf, state, sems):
    i = pl.program_id(0)
    bf16 = jnp.bfloat16
    e = blk_ref[i]
    used = used_ref[0]

    def weight_copies(expert, slot):
        return (pltpu.make_async_copy(w1_hbm.at[expert], w1_f32.at[slot], sems.at[0, slot]),
                pltpu.make_async_copy(w2_hbm.at[expert], w2_f32.at[slot], sems.at[1, slot]))

    @pl.when(i == 0)
    def _():
        state[0] = jnp.int32(-1)
        state[1] = jnp.int32(0)
        @pl.when(used > 0)
        def _():
            for cp in weight_copies(e, 0):
                cp.start()

    @pl.when(jnp.logical_and(i < used, e != state[0]))
    def _():
        slot = state[1]
        for cp in weight_copies(e, slot):
            cp.wait()
        w1_bf[...] = w1_f32[slot].astype(bf16)
        w2_bf[...] = w2_f32[slot].astype(bf16)
        nxt = lax.while_loop(lambda j: jnp.logical_and(j < used, blk_ref[jnp.minimum(j, used - 1)] == e),
                             lambda j: j + 1, i + 1)
        @pl.when(nxt < used)
        def _():
            for cp in weight_copies(blk_ref[nxt], 1 - slot):
                cp.start()
        state[0] = e
        state[1] = 1 - slot

    @pl.when(i >= used)
    def _():
        y_ref[...] = jnp.zeros_like(y_ref)

    def ffn(rows):
        hu = jnp.dot(x_ref[rows, :].astype(bf16), w1_bf[...], preferred_element_type=jnp.float32) + b1_ref[0]
        glu = jnp.minimum(hu[:, :D_FF], SWIGLU_LIMIT)
        lin = jnp.clip(hu[:, D_FF:], -SWIGLU_LIMIT, SWIGLU_LIMIT)
        act = glu * (1.0 / (1.0 + jnp.exp(-SWIGLU_ALPHA * glu))) * (lin + 1.0)
        y = jnp.dot(act.astype(bf16), w2_bf[...], preferred_element_type=jnp.float32) + b2_ref[0]
        y_ref[rows, :] = y.astype(bf16)

    half = ROW_BLOCK // 2
    valid = valid_ref[i]

    @pl.when(jnp.logical_and(i < used, valid > half))
    def _():
        ffn(slice(0, ROW_BLOCK))

    @pl.when(jnp.logical_and(i < used, valid <= half))
    def _():
        ffn(slice(0, half))
        y_ref[half:, :] = jnp.zeros((ROW_BLOCK - half, D_MODEL), bf16)


def _expert_call(blk, used, valid, rows, w1, b1, w2, b2):
    n_rows = rows.shape[0]
    n_blocks = n_rows // ROW_BLOCK
    last = lambda i, u: jnp.maximum(jnp.minimum(i, u[0] - 1), 0)
    grid_spec = pltpu.PrefetchScalarGridSpec(
        num_scalar_prefetch=3,
        grid=(n_blocks,),
        in_specs=[
            pl.BlockSpec((ROW_BLOCK, D_MODEL), lambda i, b, u, v: (last(i, u), 0)),
            pl.BlockSpec(memory_space=pl.ANY),
            pl.BlockSpec((1, 1, 2 * D_FF), lambda i, b, u, v: (b[i], 0, 0)),
            pl.BlockSpec(memory_space=pl.ANY),
            pl.BlockSpec((1, 1, D_MODEL), lambda i, b, u, v: (b[i], 0, 0)),
        ],
        out_specs=pl.BlockSpec((ROW_BLOCK, D_MODEL), lambda i, b, u, v: (i, 0)),
        scratch_shapes=[
            pltpu.VMEM((2, D_MODEL, 2 * D_FF), jnp.float32),
            pltpu.VMEM((2, D_FF, D_MODEL), jnp.float32),
            pltpu.VMEM((D_MODEL, 2 * D_FF), jnp.bfloat16),
            pltpu.VMEM((D_FF, D_MODEL), jnp.bfloat16),
            pltpu.SMEM((2,), jnp.int32),
            pltpu.SemaphoreType.DMA((2, 2)),
        ])
    return pl.pallas_call(
        _expert_kernel,
        grid_spec=grid_spec,
        out_shape=jax.ShapeDtypeStruct((n_rows, D_MODEL), jnp.bfloat16),
        name="expert_ffn",
        compiler_params=pltpu.CompilerParams(dimension_semantics=("arbitrary",),
                                             vmem_limit_bytes=VMEM_LIMIT_BYTES),
    )(blk, used, valid, rows, w1, b1.reshape(N_EXPERTS, 1, 2 * D_FF), w2, b2.reshape(N_EXPERTS, 1, D_MODEL))


def _combine_kernel(span_ref, soff_ref, src_ref, goff_ref, delta_ref,
                    x1_ref, pos_ref, gate_ref, y_ref, gfin_ref, out_ref,
                    stage, inflight, sems):
    t = pl.program_id(0)
    n_tiles = pl.num_programs(0)
    slot = t % 2

    def fetch(tile, slot_):
        def make_copy(off, src, n, sem):
            return pltpu.make_async_copy(y_ref.at[pl.ds(src, n)], stage.at[slot_, pl.ds(off, n)], sem)
        inflight[slot_] = _start_group_copies(tile, span_ref, soff_ref, src_ref, sems.at[slot_], make_copy,
                                              PACK_ROWS)

    @pl.when(t == 0)
    def _():
        stage[...] = jnp.zeros_like(stage)
        fetch(0, 0)

    @pl.when(t + 1 < n_tiles)
    def _():
        fetch(t + 1, 1 - slot)

    _wait_rows(y_ref, inflight[slot], sems.at[slot], PACK_ROWS)

    pos = pos_ref[...]
    shift = jnp.zeros_like(pos)
    prev = jnp.int32(0)
    for e in range(N_EXPERTS):
        d = delta_ref[e * LANES + t]
        shift = shift + jnp.where(pos >= goff_ref[e * LANES + t], d - prev, 0)
        prev = d
    pos = pos + shift
    gate = gate_ref[...]
    r = lax.broadcasted_iota(jnp.int32, (COMBINE_ROWS, pos.shape[1]), 0)
    w = jnp.zeros(r.shape, jnp.float32)
    for k in range(TOP_K):
        w = jnp.where(r == pos[k:k + 1], gate[k:k + 1], w)
    moe = lax.dot_general(w.astype(jnp.bfloat16), stage[slot],
                          (((0,), (0,)), ((), ())), preferred_element_type=jnp.float32)
    out_ref[...] = _rms(x1_ref[...] + moe, gfin_ref[...])


def _combine_call(span, soff, src, goff, delta, x1, pos, gate, y_rows, g_final):
    n_tok = x1.shape[0]
    ts = ROUTE_TILE
    grid_spec = pltpu.PrefetchScalarGridSpec(
        num_scalar_prefetch=5,
        grid=(n_tok // ts,),
        in_specs=[
            pl.BlockSpec((ts, D_MODEL), lambda t, *_: (t, 0)),
            pl.BlockSpec((TOP_K, ts), lambda t, *_: (0, t)),
            pl.BlockSpec((TOP_K, ts), lambda t, *_: (0, t)),
            pl.BlockSpec(memory_space=pl.ANY),
            pl.BlockSpec((1, D_MODEL), lambda t, *_: (0, 0)),
        ],
        out_specs=pl.BlockSpec((ts, D_MODEL), lambda t, *_: (t, 0)),
        scratch_shapes=[
            pltpu.VMEM((2, COMBINE_ROWS, D_MODEL), jnp.bfloat16),
            pltpu.SMEM((2,), jnp.int32),
            pltpu.SemaphoreType.DMA((2,)),
        ])
    return pl.pallas_call(
        _combine_kernel,
        grid_spec=grid_spec,
        out_shape=jax.ShapeDtypeStruct((n_tok, D_MODEL), jnp.float32),
        name="combine_norm",
        compiler_params=pltpu.CompilerParams(dimension_semantics=("arbitrary",),
                                             vmem_limit_bytes=VMEM_LIMIT_BYTES),
    )(span, soff, src, goff, delta, x1, pos, gate, y_rows, g_final)


def _attention_bias():
    slopes = jnp.exp2(-8.0 * jnp.arange(1, N_Q_HEADS + 1, dtype=jnp.float32) / N_Q_HEADS)
    qi = jnp.arange(BLOCK)[:, None]
    kj = jnp.arange(BAND)[None, :]
    dist = qi + BLOCK - kj
    valid = (dist >= 0) & (dist < WINDOW)
    b = -slopes[:, None, None] * dist.astype(jnp.float32)[None]
    b = jnp.where(valid[None], b, -jnp.inf)
    b = b.reshape(N_KV_HEADS, GROUP, BLOCK, BAND).transpose(0, 3, 1, 2)
    return b.reshape(N_KV_HEADS, BAND, GROUP * BLOCK)


def kernel(x, g_mix, w_in, conv_w, sinks, g_attn_out, g_conv_out, w_out, g_ffn, w_router, b_router,
           w1, b1, w2, b2, g_final):
    bsz, s_len, d = x.shape
    n_tok = bsz * s_len
    n_tiles = n_tok // ROUTE_TILE
    n_rows = n_tok * TOP_K + n_tiles * N_EXPERTS * SUBLANES + N_EXPERTS * ROW_BLOCK
    n_blocks = n_rows // ROW_BLOCK
    x2d = x.reshape(n_tok, d)
    bias = _attention_bias()
    l = 0
    wrt = w_router[l].T
    wr_hi = wrt.astype(jnp.bfloat16)
    wr_lo = (wrt - wr_hi.astype(jnp.float32)).astype(jnp.bfloat16)
    wr_split = jnp.concatenate([wr_hi, wr_lo], axis=0)
    x1, h2, pos, gate, cnt = _mixer_call(
        x2d, sinks[l], g_mix[l].reshape(1, d), w_in[l].astype(jnp.bfloat16), conv_w[l], bias,
        g_attn_out[l].reshape(1, ATTN_WIDTH), g_conv_out[l].reshape(1, CONV_WIDTH),
        w_out[l].astype(jnp.bfloat16), g_ffn[l].reshape(1, d), wr_split,
        b_router[l].reshape(N_EXPERTS, 1), s_len)
    size, goff, gdst, blk, used, span, soff, src, delta = _tables_call(cnt, n_blocks)
    size, goff, gdst = size.reshape(-1), goff.reshape(-1), gdst.reshape(-1)
    span, soff, src, delta = span.reshape(-1), soff.reshape(-1), src.reshape(-1), delta.reshape(-1)
    blk, valid, used = blk[0], blk[1], used.reshape(-1)[:1]
    rows = _dispatch_call(size, goff, gdst, used, h2, pos, n_rows)
    y_rows = _expert_call(blk, used, valid, rows, w1[l], b1[l], w2[l], b2[l])
    out = _combine_call(span, soff, src, goff, delta, x1, pos, gate, y_rows, g_final.reshape(1, d))
    return out.reshape(bsz, s_len, d)
```

```python
import functools
import math

import jax
import jax.numpy as jnp
from jax import lax
from jax.experimental import pallas as pl
from jax.experimental.pallas import tpu as pltpu

D_MODEL = 1024
N_Q_HEADS = 8
N_KV_HEADS = 2
HEAD_DIM = 64
GROUP = N_Q_HEADS // N_KV_HEADS
ATTN_WIDTH = N_Q_HEADS * HEAD_DIM
KV_WIDTH = N_KV_HEADS * HEAD_DIM
CONV_WIDTH = D_MODEL - ATTN_WIDTH
QKV_COLS = ATTN_WIDTH + 2 * KV_WIDTH
IN_COLS = QKV_COLS + 3 * CONV_WIDTH
WINDOW = 128
BLOCK = 128
BAND = 2 * BLOCK
N_EXPERTS = 32
TOP_K = 4
D_FF = D_MODEL
SWIGLU_ALPHA = 1.702
SWIGLU_LIMIT = 7.0
EPS = 1e-5

LANES = 128
SUBLANES = 8
VMEM_LIMIT_BYTES = 56 * 1024 * 1024

TOKEN_TILE = 1024
SUB_TILE = 512
ATTN_AHEAD = 1
ROUTE_TILE = 256
ROW_BLOCK = 512
STAGE_ROWS = -(-(TOP_K * ROUTE_TILE + N_EXPERTS * (SUBLANES - 1)) // LANES) * LANES
PACK_ROWS = 16
COMBINE_ROWS = -(-(TOP_K * ROUTE_TILE + N_EXPERTS * (SUBLANES - 1 + PACK_ROWS)) // LANES) * LANES


def _rms(x, g):
    return x * lax.rsqrt(jnp.mean(x * x, axis=-1, keepdims=True) + EPS) * g


def _excl_cumsum(a, axis):
    idx = lax.broadcasted_iota(jnp.int32, a.shape, axis)
    inc = a
    step = 1
    while step < a.shape[axis]:
        inc = inc + jnp.where(idx >= step, pltpu.roll(inc, step, axis), 0)
        step *= 2
    return inc - a


def _mixer_kernel(tiles_per_seq,
                  sink_ref,
                  x_ref, gmix_ref, win_ref, convw_ref, bias_ref, gattn_ref, gconv_ref, wout_ref,
                  gffn_ref, wr_ref, br_ref,
                  x1_ref, h2_ref, pos_ref, gate_ref, cnt_ref,
                  p_scr, h_scr, mix_scr, k_scr, vt_scr, u_scr, attn_t_scr, tri_scr):
    tm = x_ref.shape[0]
    sub_tm = SUB_TILE
    ts = ROUTE_TILE
    i = pl.program_id(0)
    first = (i % tiles_per_seq) == 0
    f32, bf16 = jnp.float32, jnp.bfloat16

    @pl.when(i == 0)
    def _():
        cnt_ref[...] = jnp.zeros_like(cnt_ref)
        r = lax.broadcasted_iota(jnp.int32, (ts, ts), 0)
        c = lax.broadcasted_iota(jnp.int32, (ts, ts), 1)
        tri_scr[...] = jnp.where(r < c, 1.0, 0.0).astype(bf16)

    @pl.when(first)
    def _():
        k_scr[0:BLOCK, :] = jnp.zeros((BLOCK, k_scr.shape[1]), bf16)
        vt_scr[:, 0:BLOCK] = jnp.zeros((KV_WIDTH, BLOCK), bf16)
        u_scr[0:SUBLANES, :] = jnp.zeros((SUBLANES, CONV_WIDTH), f32)

    n_sub = tm // sub_tm
    in_chunks = [(c, c + 2 * LANES) for c in range(0, IN_COLS, 2 * LANES)]
    out_chunks = [(c, c + 2 * LANES) for c in range(0, D_MODEL, 2 * LANES)]
    lane = lax.broadcasted_iota(jnp.int32, (BAND, LANES), 1)
    lo = lane < HEAD_DIM
    qgrp = lax.broadcasted_iota(jnp.int32, (BLOCK, 2 * LANES), 1) // HEAD_DIM
    key_row = lax.broadcasted_iota(jnp.int32, (BAND, GROUP * BLOCK), 0)
    scale = 1.0 / math.sqrt(HEAD_DIM)
    zero_q = jnp.zeros((BLOCK, 2 * LANES), bf16)
    nt = (((1,), (1,)), ((), ()))

    def rows_of(sub):
        return slice(sub * sub_tm, (sub + 1) * sub_tm)

    def prep(sub):
        r0 = rows_of(sub)
        h_scr[r0, :] = _rms(x_ref[r0, :], gmix_ref[...]).astype(bf16)

    def in_proj(sub, c0, c1):
        r0 = rows_of(sub)
        p_scr[r0, c0:c1] = jnp.dot(h_scr[r0, :], win_ref[:, c0:c1], preferred_element_type=f32)

    def kv_stage(sub):
        r0 = rows_of(sub)
        base = sub * sub_tm
        kk = p_scr[r0, ATTN_WIDTH:ATTN_WIDTH + KV_WIDTH]
        kr = pltpu.roll(kk, HEAD_DIM, axis=1)
        k_scr[BLOCK + base:BLOCK + base + sub_tm, :] = jnp.concatenate([kk, kr], axis=1).astype(bf16)
        vt_scr[:, BLOCK + base:BLOCK + base + sub_tm] = jnp.transpose(
            p_scr[r0, ATTN_WIDTH + KV_WIDTH:QKV_COLS]).astype(bf16)

    def attn_scores(j, hh):
        rows = slice(j * BLOCK, (j + 1) * BLOCK)
        kband = k_scr[j * BLOCK:j * BLOCK + BAND, :]
        bk, bkr = kband[:, 0:LANES], kband[:, LANES:2 * LANES]
        k_both = jnp.where(lo, bk, bkr) if hh == 0 else jnp.where(lo, bkr, bk)
        kmat = jnp.concatenate([k_both, k_both], axis=1)
        qh = (p_scr[rows, hh * 2 * LANES:(hh + 1) * 2 * LANES] * scale).astype(bf16)
        qm = jnp.concatenate([jnp.where(qgrp == g, qh, zero_q) for g in range(GROUP)], axis=0)
        s = lax.dot_general(kmat, qm, nt, preferred_element_type=f32) + bias_ref[hh]
        if j == 0:
            s = jnp.where(jnp.logical_and(first, key_row < BLOCK), -jnp.inf, s)
        return s

    def attn_finish(j, hh, s):
        rows = slice(j * BLOCK, (j + 1) * BLOCK)
        vt_band = vt_scr[:, j * BLOCK:j * BLOCK + BAND]
        sink = jnp.concatenate(
            [jnp.full((1, BLOCK), sink_ref[hh * GROUP + g], f32) for g in range(GROUP)], axis=1)
        m = jnp.maximum(jnp.max(s, axis=0, keepdims=True), sink)
        e = jnp.exp(s - m)
        inv = 1.0 / (jnp.sum(e, axis=0, keepdims=True) + jnp.exp(sink - m))
        o_t = jnp.dot(vt_band, e.astype(bf16), preferred_element_type=f32)
        o_t = o_t[hh * HEAD_DIM:(hh + 1) * HEAD_DIM, :] * inv
        for g in range(GROUP):
            head = hh * GROUP + g
            attn_t_scr[head * HEAD_DIM:(head + 1) * HEAD_DIM, rows] = o_t[:, g * BLOCK:(g + 1) * BLOCK]

    def conv_mix(sub):
        r0 = rows_of(sub)
        base = sub * sub_tm
        bg = p_scr[r0, QKV_COLS:QKV_COLS + CONV_WIDTH]
        cg = p_scr[r0, QKV_COLS + CONV_WIDTH:QKV_COLS + 2 * CONV_WIDTH]
        xin = p_scr[r0, QKV_COLS + 2 * CONV_WIDTH:IN_COLS]
        u = cg * xin
        u_scr[SUBLANES + base:SUBLANES + base + sub_tm, :] = u
        u1 = u_scr[SUBLANES - 1 + base:SUBLANES - 1 + base + sub_tm, :]
        u2 = u_scr[SUBLANES - 2 + base:SUBLANES - 2 + base + sub_tm, :]
        y = convw_ref[0:1, :] * u2 + convw_ref[1:2, :] * u1 + convw_ref[2:3, :] * u
        conv = bg * y
        mix_scr[r0, :] = jnp.concatenate(
            [_rms(jnp.transpose(attn_t_scr[:, r0]), gattn_ref[...]), _rms(conv, gconv_ref[...])],
            axis=1).astype(bf16)

    def out_proj(sub, c0, c1):
        r0 = rows_of(sub)
        x1_ref[r0, c0:c1] = x_ref[r0, c0:c1] + jnp.dot(mix_scr[r0, :], wout_ref[:, c0:c1],
                                                      preferred_element_type=f32)

    def route(sub):
        r0 = rows_of(sub)
        h2 = _rms(x1_ref[r0, :], gffn_ref[...])
        h2_hi = h2.astype(bf16)
        h2_ref[r0, :] = h2_hi
        h2_lo = (h2 - h2_hi.astype(f32)).astype(bf16)
        by_hi = lax.dot_general(wr_ref[...], h2_hi, nt, preferred_element_type=f32)
        by_lo = lax.dot_general(wr_ref[0:N_EXPERTS, :], h2_lo, nt, preferred_element_type=f32)
        logits = by_hi[0:N_EXPERTS] + by_hi[N_EXPERTS:] + by_lo + br_ref[...]
        row = lax.broadcasted_iota(jnp.int32, (N_EXPERTS, sub_tm), 0)
        vals, idxs = [], []
        l = logits
        for _ in range(TOP_K):
            m = jnp.max(l, axis=0, keepdims=True)
            idx = jnp.min(jnp.where(l == m, row, N_EXPERTS), axis=0, keepdims=True)
            vals.append(m)
            idxs.append(idx)
            l = jnp.where(row == idx, -jnp.inf, l)
        es = [jnp.exp(v - vals[0]) for v in vals]
        inv = 1.0 / (es[0] + es[1] + es[2] + es[3])
        gate_ref[:, r0] = jnp.concatenate([e * inv for e in es], axis=0)

        onehots = [row == idx for idx in idxs]
        sel = jnp.logical_or(jnp.logical_or(onehots[0], onehots[1]), jnp.logical_or(onehots[2], onehots[3]))
        oh = jnp.where(sel, 1.0, 0.0)
        cnt_lane = lax.broadcasted_iota(jnp.int32, cnt_ref.shape, 1)
        poss = []
        for rt in range(sub_tm // ts):
            sl = slice(rt * ts, (rt + 1) * ts)
            oh_s = oh[:, sl]
            cum = jnp.dot(oh_s.astype(bf16), tri_scr[...], preferred_element_type=f32)
            cnt = jnp.sum(oh_s, axis=1, keepdims=True)
            padded = SUBLANES * jnp.floor((cnt + (SUBLANES - 1.0)) * (1.0 / SUBLANES))
            goff = _excl_cumsum(jnp.broadcast_to(padded, (N_EXPERTS, LANES)), 0)[:, 0:1]
            pos = goff + cum
            poss.append(jnp.concatenate(
                [jnp.sum(jnp.where(o[:, sl], pos, 0.0), axis=0, keepdims=True) for o in onehots], axis=0))
            tile_id = i * (tm // ts) + sub * (sub_tm // ts) + rt
            cnt_ref[...] = jnp.where(cnt_lane == tile_id, cnt.astype(jnp.int32), cnt_ref[...])
        pos_ref[:, r0] = jnp.concatenate(poss, axis=1).astype(jnp.int32)

    def interleave(main, fillers):
        after = [[] for _ in main]
        for n, filler in enumerate(fillers):
            after[n * len(main) // len(fillers)].append(filler)
        for stage, extra in zip(main, after):
            stage()
            for filler in extra:
                filler()

    def attention_stages(sub):
        blocks = range(sub * (sub_tm // BLOCK), (sub + 1) * (sub_tm // BLOCK))
        its = [(j, hh) for j in blocks for hh in range(N_KV_HEADS)]
        scores = {}

        def score(n):
            scores[n] = attn_scores(*its[n])

        def finish(n):
            attn_finish(*its[n], scores.pop(n))

        stages = [functools.partial(score, n) for n in range(min(ATTN_AHEAD, len(its)))]
        for n in range(len(its)):
            if n + ATTN_AHEAD < len(its):
                stages.append(functools.partial(score, n + ATTN_AHEAD))
            stages.append(functools.partial(finish, n))
        return stages

    prep(0)
    for c0, c1 in in_chunks:
        in_proj(0, c0, c1)
    kv_stage(0)
    for sub in range(n_sub):
        fillers = []
        if sub > 0:
            fillers.append(functools.partial(conv_mix, sub - 1))
            fillers += [functools.partial(out_proj, sub - 1, c0, c1) for c0, c1 in out_chunks]
        if sub + 1 < n_sub:
            prep(sub + 1)
            fillers += [functools.partial(in_proj, sub + 1, c0, c1) for c0, c1 in in_chunks]
        interleave(attention_stages(sub), fillers)
        if sub + 1 < n_sub:
            kv_stage(sub + 1)
        if 0 < sub < n_sub - 1:
            route(sub - 1)
    last = n_sub - 1
    conv_mix(last)
    out_proj(last, *out_chunks[0])
    if last > 0:
        route(last - 1)
    for c0, c1 in out_chunks[1:]:
        out_proj(last, c0, c1)
    route(last)

    k_scr[0:BLOCK, :] = k_scr[tm:tm + BLOCK, :]
    vt_scr[:, 0:BLOCK] = vt_scr[:, tm:tm + BLOCK]
    u_scr[0:SUBLANES, :] = u_scr[tm:tm + SUBLANES, :]


def _mixer_call(x2d, sinks, g_mix, w_in, conv_w, bias, g_attn, g_conv, w_out, g_ffn, wr_split, br, seq_len):
    n_tok = x2d.shape[0]
    tm = TOKEN_TILE
    n_tiles = n_tok // tm
    assert n_tok // ROUTE_TILE == LANES, "one lane of the count table per routing tile"
    const = lambda shape: pl.BlockSpec(shape, lambda i, *_: (0,) * len(shape))
    grid_spec = pltpu.PrefetchScalarGridSpec(
        num_scalar_prefetch=0,
        grid=(n_tiles,),
        in_specs=[
            pl.BlockSpec(memory_space=pltpu.SMEM),
            pl.BlockSpec((tm, D_MODEL), lambda i: (i, 0)),
            const((1, D_MODEL)), const((D_MODEL, IN_COLS)), const((3, CONV_WIDTH)),
            const((N_KV_HEADS, BAND, GROUP * BLOCK)),
            const((1, ATTN_WIDTH)), const((1, CONV_WIDTH)), const((D_MODEL, D_MODEL)),
            const((1, D_MODEL)), const((2 * N_EXPERTS, D_MODEL)), const((N_EXPERTS, 1)),
        ],
        out_specs=[
            pl.BlockSpec((tm, D_MODEL), lambda i: (i, 0)),
            pl.BlockSpec((tm, D_MODEL), lambda i: (i, 0)),
            pl.BlockSpec((TOP_K, tm), lambda i: (0, i)),
            pl.BlockSpec((TOP_K, tm), lambda i: (0, i)),
            pl.BlockSpec((N_EXPERTS, LANES), lambda i: (0, 0)),
        ],
        scratch_shapes=[
            pltpu.VMEM((tm, IN_COLS), jnp.float32),
            pltpu.VMEM((tm, D_MODEL), jnp.bfloat16),
            pltpu.VMEM((tm, D_MODEL), jnp.bfloat16),
            pltpu.VMEM((BLOCK + tm, 2 * LANES), jnp.bfloat16),
            pltpu.VMEM((KV_WIDTH, BLOCK + tm), jnp.bfloat16),
            pltpu.VMEM((SUBLANES + tm, CONV_WIDTH), jnp.float32),
            pltpu.VMEM((ATTN_WIDTH, tm), jnp.float32),
            pltpu.VMEM((ROUTE_TILE, ROUTE_TILE), jnp.bfloat16),
        ])
    out_shape = [
        jax.ShapeDtypeStruct((n_tok, D_MODEL), jnp.float32),
        jax.ShapeDtypeStruct((n_tok, D_MODEL), jnp.bfloat16),
        jax.ShapeDtypeStruct((TOP_K, n_tok), jnp.int32),
        jax.ShapeDtypeStruct((TOP_K, n_tok), jnp.float32),
        jax.ShapeDtypeStruct((N_EXPERTS, LANES), jnp.int32),
    ]
    return pl.pallas_call(
        functools.partial(_mixer_kernel, seq_len // tm),
        grid_spec=grid_spec, out_shape=out_shape, name="mixer_router",
        compiler_params=pltpu.CompilerParams(dimension_semantics=("arbitrary",),
                                             vmem_limit_bytes=VMEM_LIMIT_BYTES),
    )(sinks, x2d, g_mix, w_in, conv_w, bias, g_attn, g_conv, w_out, g_ffn, wr_split, br)


def _tables_kernel(cnt_ref, size_ref, goff_ref, gdst_ref, blk_ref, used_ref,
                   span_ref, soff_ref, src_ref, delta_ref):
    size = ((cnt_ref[...] + (SUBLANES - 1)) // SUBLANES) * SUBLANES
    size_ref[...] = size
    goff = _excl_cumsum(size, 0)
    goff_ref[...] = goff
    total = jnp.broadcast_to(jnp.sum(size, axis=1, keepdims=True), size.shape)
    region = ((total + (ROW_BLOCK - 1)) // ROW_BLOCK) * ROW_BLOCK
    pstart = _excl_cumsum(region, 0)
    gdst = pstart + _excl_cumsum(size, 1)
    gdst_ref[...] = gdst
    src = (gdst // PACK_ROWS) * PACK_ROWS
    span = jnp.where(size > 0, ((gdst + size + (PACK_ROWS - 1)) // PACK_ROWS) * PACK_ROWS - src, 0)
    soff = _excl_cumsum(span, 0)
    span_ref[...] = span
    soff_ref[...] = soff
    src_ref[...] = src
    delta_ref[...] = soff + (gdst - src) - goff
    pend = (pstart + region)[:, 0:1]
    start = lax.broadcasted_iota(jnp.int32, (N_EXPERTS, blk_ref.shape[1]), 1) * ROW_BLOCK
    owner = jnp.sum(jnp.where(pend <= start, 1, 0), axis=0, keepdims=True)
    blk_ref[0:1, :] = jnp.minimum(owner, N_EXPERTS - 1)
    first_row, last_row = pstart[:, 0:1], (pstart + total)[:, 0:1]
    inside = jnp.logical_and(first_row <= start, start < pend)
    blk_ref[1:2, :] = jnp.sum(jnp.where(inside, jnp.minimum(last_row - start, ROW_BLOCK), 0),
                              axis=0, keepdims=True)
    used_ref[...] = jnp.broadcast_to(jnp.max(pend, axis=0, keepdims=True) // ROW_BLOCK, used_ref.shape)


def _tables_call(cnt, n_blocks):
    nb_pad = -(-n_blocks // LANES) * LANES
    tbl = jax.ShapeDtypeStruct((N_EXPERTS, LANES), jnp.int32)
    return pl.pallas_call(
        _tables_kernel,
        out_shape=[tbl, tbl, tbl, jax.ShapeDtypeStruct((2, nb_pad), jnp.int32),
                   jax.ShapeDtypeStruct((1, LANES), jnp.int32), tbl, tbl, tbl, tbl],
        name="group_tables",
    )(cnt)


def _start_group_copies(tile, size_ref, goff_ref, gdst_ref, sem, make_copy, align):
    def body(e, rows):
        n = pl.multiple_of(size_ref[e * LANES + tile], align)
        @pl.when(n > 0)
        def _():
            off = pl.multiple_of(goff_ref[e * LANES + tile], align)
            dst = pl.multiple_of(gdst_ref[e * LANES + tile], align)
            make_copy(off, dst, n, sem).start()
        return rows + n
    return lax.fori_loop(0, N_EXPERTS, body, jnp.int32(0))


def _wait_rows(rows_hbm, n, sem, align):
    @pl.when(n > 0)
    def _():
        m = pl.multiple_of(n, align)
        half = rows_hbm.shape[0] // (2 * align) * align
        pltpu.make_async_copy(rows_hbm.at[pl.ds(0, m)], rows_hbm.at[pl.ds(half, m)], sem).wait()


def _dispatch_kernel(n_blocks,
                     size_ref, goff_ref, gdst_ref, used_ref,
                     h2_ref, pos_ref, rows_ref,
                     stage, zbuf, inflight, zeroed, sems, zsem):
    t = pl.program_id(0)
    n_tiles = pl.num_programs(0)
    slot = t % 2

    @pl.when(t == 0)
    def _():
        zbuf[...] = jnp.zeros_like(zbuf)
        def zfill(e, n):
            first = gdst_ref[e * LANES]
            last = gdst_ref[e * LANES + LANES - 1] + size_ref[e * LANES + LANES - 1]
            pad = pl.multiple_of((last - first + ROW_BLOCK - 1) // ROW_BLOCK * ROW_BLOCK - (last - first), SUBLANES)
            @pl.when(pad > 0)
            def _():
                pltpu.make_async_copy(zbuf.at[pl.ds(0, pad)],
                                      rows_ref.at[pl.ds(pl.multiple_of(last, SUBLANES), pad)], zsem).start()
            return n + pad
        n_zero = lax.fori_loop(0, N_EXPERTS, zfill, jnp.int32(0))
        def ztail(b, _):
            pltpu.make_async_copy(
                zbuf, rows_ref.at[pl.ds(pl.multiple_of(b * ROW_BLOCK, ROW_BLOCK), ROW_BLOCK)], zsem).start()
            return 0
        lax.fori_loop(used_ref[0], n_blocks, ztail, 0)
        zeroed[0] = n_zero + (n_blocks - used_ref[0]) * ROW_BLOCK

    @pl.when(t >= 2)
    def _():
        _wait_rows(rows_ref, inflight[slot], sems.at[slot], SUBLANES)

    pos = pos_ref[...]
    r = lax.broadcasted_iota(jnp.int32, (STAGE_ROWS, pos.shape[1]), 0)
    hit = jnp.logical_or(jnp.logical_or(r == pos[0:1], r == pos[1:2]),
                         jnp.logical_or(r == pos[2:3], r == pos[3:4]))
    perm = jnp.where(hit, 1.0, 0.0).astype(jnp.bfloat16)
    stage[slot] = jnp.dot(perm, h2_ref[...], preferred_element_type=jnp.float32)

    def make_copy(off, dst, n, sem):
        return pltpu.make_async_copy(stage.at[slot, pl.ds(off, n)], rows_ref.at[pl.ds(dst, n)], sem)
    inflight[slot] = _start_group_copies(t, size_ref, goff_ref, gdst_ref, sems.at[slot], make_copy, SUBLANES)

    @pl.when(t == n_tiles - 1)
    def _():
        _wait_rows(rows_ref, inflight[1 - slot], sems.at[1 - slot], SUBLANES)
        _wait_rows(rows_ref, inflight[slot], sems.at[slot], SUBLANES)
        _wait_rows(rows_ref, zeroed[0], zsem, SUBLANES)


def _dispatch_call(size, goff, gdst, used, h2, pos, n_rows):
    n_tok = h2.shape[0]
    ts = ROUTE_TILE
    n_blocks = n_rows // ROW_BLOCK
    grid_spec = pltpu.PrefetchScalarGridSpec(
        num_scalar_prefetch=4,
        grid=(n_tok // ts,),
        in_specs=[
            pl.BlockSpec((ts, D_MODEL), lambda t, *_: (t, 0)),
            pl.BlockSpec((TOP_K, ts), lambda t, *_: (0, t)),
        ],
        out_specs=pl.BlockSpec(memory_space=pl.ANY),
        scratch_shapes=[
            pltpu.VMEM((2, STAGE_ROWS, D_MODEL), jnp.float32),
            pltpu.VMEM((ROW_BLOCK, D_MODEL), jnp.float32),
            pltpu.SMEM((2,), jnp.int32),
            pltpu.SMEM((1,), jnp.int32),
            pltpu.SemaphoreType.DMA((2,)),
            pltpu.SemaphoreType.DMA(()),
        ])
    return pl.pallas_call(
        functools.partial(_dispatch_kernel, n_blocks),
        grid_spec=grid_spec,
        out_shape=jax.ShapeDtypeStruct((n_rows, D_MODEL), jnp.float32),
        name="dispatch_rows",
        compiler_params=pltpu.CompilerParams(dimension_semantics=("arbitrary",),
                                             vmem_limit_bytes=VMEM_LIMIT_BYTES),
    )(size, goff, gdst, used, h2, pos)


def _expert_kernel(blk_ref, used_ref, valid_ref, x_ref, w1_hbm, b1_ref, w2_hbm, b2_ref, y_ref,
                   w1_f32, w2_f32, w1_bf, w2_bf, state, sems):
    i = pl.program_id(0)
    bf16 = jnp.bfloat16
    e = blk_ref[i]
    used = used_ref[0]

    def weight_copies(expert, slot):
        return (pltpu.make_async_copy(w1_hbm.at[expert], w1_f32.at[slot], sems.at[0, slot]),
                pltpu.make_async_copy(w2_hbm.at[expert], w2_f32.at[slot], sems.at[1, slot]))

    @pl.when(i == 0)
    def _():
        state[0] = jnp.int32(-1)
        state[1] = jnp.int32(0)
        @pl.when(used > 0)
        def _():
            for cp in weight_copies(e, 0):
                cp.start()

    @pl.when(jnp.logical_and(i < used, e != state[0]))
    def _():
        slot = state[1]
        for cp in weight_copies(e, slot):
            cp.wait()
        w1_bf[...] = w1_f32[slot].astype(bf16)
        w2_bf[...] = w2_f32[slot].astype(bf16)
        nxt = lax.while_loop(lambda j: jnp.logical_and(j < used, blk_ref[jnp.minimum(j, used - 1)] == e),
                             lambda j: j + 1, i + 1)
        @pl.when(nxt < used)
        def _():
            for cp in weight_copies(blk_ref[nxt], 1 - slot):
                cp.start()
        state[0] = e
        state[1] = 1 - slot

    @pl.when(i >= used)
    def _():
        y_ref[...] = jnp.zeros_like(y_ref)

    def ffn(rows):
        hu = jnp.dot(x_ref[rows, :].astype(bf16), w1_bf[...], preferred_element_type=jnp.float32) + b1_ref[0]
        glu = jnp.minimum(hu[:, :D_FF], SWIGLU_LIMIT)
        lin = jnp.clip(hu[:, D_FF:], -SWIGLU_LIMIT, SWIGLU_LIMIT)
        act = glu * (1.0 / (1.0 + jnp.exp(-SWIGLU_ALPHA * glu))) * (lin + 1.0)
        y = jnp.dot(act.astype(bf16), w2_bf[...], preferred_element_type=jnp.float32) + b2_ref[0]
        y_ref[rows, :] = y.astype(bf16)

    valid = valid_ref[i]
    quarter = ROW_BLOCK // 4
    for q in range(1, 5):
        n = q * quarter
        lo_ok = valid > n - quarter if q > 1 else True
        hi_ok = valid <= n if q < 4 else True
        @pl.when(jnp.logical_and(i < used, jnp.logical_and(lo_ok, hi_ok)))
        def _():
            ffn(slice(0, n))
            if n < ROW_BLOCK:
                y_ref[n:, :] = jnp.zeros((ROW_BLOCK - n, D_MODEL), bf16)


def _expert_call(blk, used, valid, rows, w1, b1, w2, b2):
    n_rows = rows.shape[0]
    n_blocks = n_rows // ROW_BLOCK
    last = lambda i, u: jnp.maximum(jnp.minimum(i, u[0] - 1), 0)
    grid_spec = pltpu.PrefetchScalarGridSpec(
        num_scalar_prefetch=3,
        grid=(n_blocks,),
        in_specs=[
            pl.BlockSpec((ROW_BLOCK, D_MODEL), lambda i, b, u, v: (last(i, u), 0)),
            pl.BlockSpec(memory_space=pl.ANY),
            pl.BlockSpec((1, 1, 2 * D_FF), lambda i, b, u, v: (b[i], 0, 0)),
            pl.BlockSpec(memory_space=pl.ANY),
            pl.BlockSpec((1, 1, D_MODEL), lambda i, b, u, v: (b[i], 0, 0)),
        ],
        out_specs=pl.BlockSpec((ROW_BLOCK, D_MODEL), lambda i, b, u, v: (i, 0)),
        scratch_shapes=[
            pltpu.VMEM((2, D_MODEL, 2 * D_FF), jnp.float32),
            pltpu.VMEM((2, D_FF, D_MODEL), jnp.float32),
            pltpu.VMEM((D_MODEL, 2 * D_FF), jnp.bfloat16),
            pltpu.VMEM((D_FF, D_MODEL), jnp.bfloat16),
            pltpu.SMEM((2,), jnp.int32),
            pltpu.SemaphoreType.DMA((2, 2)),
        ])
    return pl.pallas_call(
        _expert_kernel,
        grid_spec=grid_spec,
        out_shape=jax.ShapeDtypeStruct((n_rows, D_MODEL), jnp.bfloat16),
        name="expert_ffn",
        compiler_params=pltpu.CompilerParams(dimension_semantics=("arbitrary",),
                                             vmem_limit_bytes=VMEM_LIMIT_BYTES),
    )(blk, used, valid, rows, w1, b1.reshape(N_EXPERTS, 1, 2 * D_FF), w2, b2.reshape(N_EXPERTS, 1, D_MODEL))


def _combine_kernel(span_ref, soff_ref, src_ref, goff_ref, delta_ref,
                    x1_ref, pos_ref, gate_ref, y_ref, gfin_ref, out_ref,
                    stage, inflight, sems):
    t = pl.program_id(0)
    n_tiles = pl.num_programs(0)
    slot = t % 2

    def fetch(tile, slot_):
        def make_copy(off, src, n, sem):
            return pltpu.make_async_copy(y_ref.at[pl.ds(src, n)], stage.at[slot_, pl.ds(off, n)], sem)
        inflight[slot_] = _start_group_copies(tile, span_ref, soff_ref, src_ref, sems.at[slot_], make_copy,
                                              PACK_ROWS)

    @pl.when(t == 0)
    def _():
        stage[...] = jnp.zeros_like(stage)
        fetch(0, 0)

    @pl.when(t + 1 < n_tiles)
    def _():
        fetch(t + 1, 1 - slot)

    _wait_rows(y_ref, inflight[slot], sems.at[slot], PACK_ROWS)

    pos = pos_ref[...]
    shift = jnp.zeros_like(pos)
    prev = jnp.int32(0)
    for e in range(N_EXPERTS):
        d = delta_ref[e * LANES + t]
        shift = shift + jnp.where(pos >= goff_ref[e * LANES + t], d - prev, 0)
        prev = d
    pos = pos + shift
    gate = gate_ref[...]
    r = lax.broadcasted_iota(jnp.int32, (COMBINE_ROWS, pos.shape[1]), 0)
    w = jnp.zeros(r.shape, jnp.float32)
    for k in range(TOP_K):
        w = jnp.where(r == pos[k:k + 1], gate[k:k + 1], w)
    moe = lax.dot_general(w.astype(jnp.bfloat16), stage[slot],
                          (((0,), (0,)), ((), ())), preferred_element_type=jnp.float32)
    out_ref[...] = _rms(x1_ref[...] + moe, gfin_ref[...])


def _combine_call(span, soff, src, goff, delta, x1, pos, gate, y_rows, g_final):
    n_tok = x1.shape[0]
    ts = ROUTE_TILE
    grid_spec = pltpu.PrefetchScalarGridSpec(
        num_scalar_prefetch=5,
        grid=(n_tok // ts,),
        in_specs=[
            pl.BlockSpec((ts, D_MODEL), lambda t, *_: (t, 0)),
            pl.BlockSpec((TOP_K, ts), lambda t, *_: (0, t)),
            pl.BlockSpec((TOP_K, ts), lambda t, *_: (0, t)),
            pl.BlockSpec(memory_space=pl.ANY),
            pl.BlockSpec((1, D_MODEL), lambda t, *_: (0, 0)),
        ],
        out_specs=pl.BlockSpec((ts, D_MODEL), lambda t, *_: (t, 0)),
        scratch_shapes=[
            pltpu.VMEM((2, COMBINE_ROWS, D_MODEL), jnp.bfloat16),
            pltpu.SMEM((2,), jnp.int32),
            pltpu.SemaphoreType.DMA((2,)),
        ])
    return pl.pallas_call(
        _combine_kernel,
        grid_spec=grid_spec,
        out_shape=jax.ShapeDtypeStruct((n_tok, D_MODEL), jnp.float32),
        name="combine_norm",
        compiler_params=pltpu.CompilerParams(dimension_semantics=("arbitrary",),
                                             vmem_limit_bytes=VMEM_LIMIT_BYTES),
    )(span, soff, src, goff, delta, x1, pos, gate, y_rows, g_final)


def _attention_bias():
    slopes = jnp.exp2(-8.0 * jnp.arange(1, N_Q_HEADS + 1, dtype=jnp.float32) / N_Q_HEADS)
    qi = jnp.arange(BLOCK)[:, None]
    kj = jnp.arange(BAND)[None, :]
    dist = qi + BLOCK - kj
    valid = (dist >= 0) & (dist < WINDOW)
    b = -slopes[:, None, None] * dist.astype(jnp.float32)[None]
    b = jnp.where(valid[None], b, -jnp.inf)
    b = b.reshape(N_KV_HEADS, GROUP, BLOCK, BAND).transpose(0, 3, 1, 2)
    return b.reshape(N_KV_HEADS, BAND, GROUP * BLOCK)


def kernel(x, g_mix, w_in, conv_w, sinks, g_attn_out, g_conv_out, w_out, g_ffn, w_router, b_router,
           w1, b1, w2, b2, g_final):
    bsz, s_len, d = x.shape
    n_tok = bsz * s_len
    n_tiles = n_tok // ROUTE_TILE
    n_rows = n_tok * TOP_K + n_tiles * N_EXPERTS * SUBLANES + N_EXPERTS * ROW_BLOCK
    n_blocks = n_rows // ROW_BLOCK
    x2d = x.reshape(n_tok, d)
    bias = _attention_bias()
    l = 0
    wrt = w_router[l].T
    wr_hi = wrt.astype(jnp.bfloat16)
    wr_lo = (wrt - wr_hi.astype(jnp.float32)).astype(jnp.bfloat16)
    wr_split = jnp.concatenate([wr_hi, wr_lo], axis=0)
    x1, h2, pos, gate, cnt = _mixer_call(
        x2d, sinks[l], g_mix[l].reshape(1, d), w_in[l].astype(jnp.bfloat16), conv_w[l], bias,
        g_attn_out[l].reshape(1, ATTN_WIDTH), g_conv_out[l].reshape(1, CONV_WIDTH),
        w_out[l].astype(jnp.bfloat16), g_ffn[l].reshape(1, d), wr_split,
        b_router[l].reshape(N_EXPERTS, 1), s_len)
    size, goff, gdst, blk, used, span, soff, src, delta = _tables_call(cnt, n_blocks)
    size, goff, gdst = size.reshape(-1), goff.reshape(-1), gdst.reshape(-1)
    span, soff, src, delta = span.reshape(-1), soff.reshape(-1), src.reshape(-1), delta.reshape(-1)
    blk, valid, used = blk[0], blk[1], used.reshape(-1)[:1]
    rows = _dispatch_call(size, goff, gdst, used, h2, pos, n_rows)
    y_rows = _expert_call(blk, used, valid, rows, w1[l], b1[l], w2[l], b2[l])
    out = _combine_call(span, soff, src, goff, delta, x1, pos, gate, y_rows, g_final.reshape(1, d))
    return out.reshape(bsz, s_len, d)
```

```python
import functools
import math

import jax
import jax.numpy as jnp
from jax import lax
from jax.experimental import pallas as pl
from jax.experimental.pallas import tpu as pltpu

D_MODEL = 1024
N_Q_HEADS = 8
N_KV_HEADS = 2
HEAD_DIM = 64
GROUP = N_Q_HEADS // N_KV_HEADS
ATTN_WIDTH = N_Q_HEADS * HEAD_DIM
KV_WIDTH = N_KV_HEADS * HEAD_DIM
CONV_WIDTH = D_MODEL - ATTN_WIDTH
QKV_COLS = ATTN_WIDTH + 2 * KV_WIDTH
IN_COLS = QKV_COLS + 3 * CONV_WIDTH
WINDOW = 128
BLOCK = 128
BAND = 2 * BLOCK
N_EXPERTS = 32
TOP_K = 4
D_FF = D_MODEL
SWIGLU_ALPHA = 1.702
SWIGLU_LIMIT = 7.0
EPS = 1e-5

LANES = 128
SUBLANES = 8
VMEM_LIMIT_BYTES = 56 * 1024 * 1024

TOKEN_TILE = 1024
SUB_TILE = 512
ATTN_AHEAD = 1
ROUTE_TILE = 256
ROW_BLOCK = 512
STAGE_ROWS = -(-(TOP_K * ROUTE_TILE + N_EXPERTS * (SUBLANES - 1)) // LANES) * LANES
PACK_ROWS = 16
COMBINE_ROWS = -(-(TOP_K * ROUTE_TILE + N_EXPERTS * (SUBLANES - 1 + PACK_ROWS)) // LANES) * LANES


def _rms(x, g):
    return x * lax.rsqrt(jnp.mean(x * x, axis=-1, keepdims=True) + EPS) * g


def _excl_cumsum(a, axis):
    idx = lax.broadcasted_iota(jnp.int32, a.shape, axis)
    inc = a
    step = 1
    while step < a.shape[axis]:
        inc = inc + jnp.where(idx >= step, pltpu.roll(inc, step, axis), 0)
        step *= 2
    return inc - a


def _mixer_kernel(tiles_per_seq,
                  sink_ref,
                  x_ref, gmix_ref, win_ref, convw_ref, bias_ref, gattn_ref, gconv_ref, wout_ref,
                  gffn_ref, wr_ref, br_ref,
                  x1_ref, h2_ref, pos_ref, gate_ref, cnt_ref,
                  p_scr, h_scr, mix_scr, k_scr, vt_scr, u_scr, attn_t_scr, tri_scr):
    tm = x_ref.shape[0]
    sub_tm = SUB_TILE
    ts = ROUTE_TILE
    i = pl.program_id(0)
    first = (i % tiles_per_seq) == 0
    f32, bf16 = jnp.float32, jnp.bfloat16

    @pl.when(i == 0)
    def _():
        cnt_ref[...] = jnp.zeros_like(cnt_ref)
        r = lax.broadcasted_iota(jnp.int32, (ts, ts), 0)
        c = lax.broadcasted_iota(jnp.int32, (ts, ts), 1)
        tri_scr[...] = jnp.where(r < c, 1.0, 0.0).astype(bf16)

    @pl.when(first)
    def _():
        k_scr[0:BLOCK, :] = jnp.zeros((BLOCK, k_scr.shape[1]), bf16)
        vt_scr[:, 0:BLOCK] = jnp.zeros((KV_WIDTH, BLOCK), bf16)
        u_scr[0:SUBLANES, :] = jnp.zeros((SUBLANES, CONV_WIDTH), f32)

    n_sub = tm // sub_tm
    in_chunks = [(c, c + 2 * LANES) for c in range(0, IN_COLS, 2 * LANES)]
    out_chunks = [(c, c + 2 * LANES) for c in range(0, D_MODEL, 2 * LANES)]
    lane = lax.broadcasted_iota(jnp.int32, (BAND, LANES), 1)
    lo = lane < HEAD_DIM
    qgrp = lax.broadcasted_iota(jnp.int32, (BLOCK, 2 * LANES), 1) // HEAD_DIM
    key_row = lax.broadcasted_iota(jnp.int32, (BAND, GROUP * BLOCK), 0)
    scale = 1.0 / math.sqrt(HEAD_DIM)
    zero_q = jnp.zeros((BLOCK, 2 * LANES), bf16)
    nt = (((1,), (1,)), ((), ()))

    def rows_of(sub):
        return slice(sub * sub_tm, (sub + 1) * sub_tm)

    def prep(sub):
        r0 = rows_of(sub)
        h_scr[r0, :] = _rms(x_ref[r0, :], gmix_ref[...]).astype(bf16)

    def in_proj(sub, c0, c1):
        r0 = rows_of(sub)
        p_scr[r0, c0:c1] = jnp.dot(h_scr[r0, :], win_ref[:, c0:c1], preferred_element_type=f32)

    def kv_stage(sub):
        r0 = rows_of(sub)
        base = sub * sub_tm
        kk = p_scr[r0, ATTN_WIDTH:ATTN_WIDTH + KV_WIDTH]
        kr = pltpu.roll(kk, HEAD_DIM, axis=1)
        k_scr[BLOCK + base:BLOCK + base + sub_tm, :] = jnp.concatenate([kk, kr], axis=1).astype(bf16)
        vt_scr[:, BLOCK + base:BLOCK + base + sub_tm] = jnp.transpose(
            p_scr[r0, ATTN_WIDTH + KV_WIDTH:QKV_COLS]).astype(bf16)

    def attn_scores(j, hh):
        rows = slice(j * BLOCK, (j + 1) * BLOCK)
        kband = k_scr[j * BLOCK:j * BLOCK + BAND, :]
        bk, bkr = kband[:, 0:LANES], kband[:, LANES:2 * LANES]
        k_both = jnp.where(lo, bk, bkr) if hh == 0 else jnp.where(lo, bkr, bk)
        kmat = jnp.concatenate([k_both, k_both], axis=1)
        qh = (p_scr[rows, hh * 2 * LANES:(hh + 1) * 2 * LANES] * scale).astype(bf16)
        qm = jnp.concatenate([jnp.where(qgrp == g, qh, zero_q) for g in range(GROUP)], axis=0)
        s = lax.dot_general(kmat, qm, nt, preferred_element_type=f32) + bias_ref[hh]
        if j == 0:
            s = jnp.where(jnp.logical_and(first, key_row < BLOCK), -jnp.inf, s)
        return s

    def attn_finish(j, hh, s):
        rows = slice(j * BLOCK, (j + 1) * BLOCK)
        vt_band = vt_scr[:, j * BLOCK:j * BLOCK + BAND]
        sink = jnp.concatenate(
            [jnp.full((1, BLOCK), sink_ref[hh * GROUP + g], f32) for g in range(GROUP)], axis=1)
        m = jnp.maximum(jnp.max(s, axis=0, keepdims=True), sink)
        e = jnp.exp(s - m)
        inv = 1.0 / (jnp.sum(e, axis=0, keepdims=True) + jnp.exp(sink - m))
        o_t = jnp.dot(vt_band, e.astype(bf16), preferred_element_type=f32)
        o_t = o_t[hh * HEAD_DIM:(hh + 1) * HEAD_DIM, :] * inv
        for g in range(GROUP):
            head = hh * GROUP + g
            attn_t_scr[head * HEAD_DIM:(head + 1) * HEAD_DIM, rows] = o_t[:, g * BLOCK:(g + 1) * BLOCK]

    def conv_mix(sub):
        r0 = rows_of(sub)
        base = sub * sub_tm
        bg = p_scr[r0, QKV_COLS:QKV_COLS + CONV_WIDTH]
        cg = p_scr[r0, QKV_COLS + CONV_WIDTH:QKV_COLS + 2 * CONV_WIDTH]
        xin = p_scr[r0, QKV_COLS + 2 * CONV_WIDTH:IN_COLS]
        u = cg * xin
        u_scr[SUBLANES + base:SUBLANES + base + sub_tm, :] = u
        u1 = u_scr[SUBLANES - 1 + base:SUBLANES - 1 + base + sub_tm, :]
        u2 = u_scr[SUBLANES - 2 + base:SUBLANES - 2 + base + sub_tm, :]
        y = convw_ref[0:1, :] * u2 + convw_ref[1:2, :] * u1 + convw_ref[2:3, :] * u
        conv = bg * y
        mix_scr[r0, :] = jnp.concatenate(
            [_rms(jnp.transpose(attn_t_scr[:, r0]), gattn_ref[...]), _rms(conv, gconv_ref[...])],
            axis=1).astype(bf16)

    def out_proj(sub, c0, c1):
        r0 = rows_of(sub)
        x1_ref[r0, c0:c1] = x_ref[r0, c0:c1] + jnp.dot(mix_scr[r0, :], wout_ref[:, c0:c1],
                                                      preferred_element_type=f32)

    def route(sub):
        r0 = rows_of(sub)
        h2 = _rms(x1_ref[r0, :], gffn_ref[...])
        h2_hi = h2.astype(bf16)
        h2_ref[r0, :] = h2_hi
        h2_lo = (h2 - h2_hi.astype(f32)).astype(bf16)
        by_hi = lax.dot_general(wr_ref[...], h2_hi, nt, preferred_element_type=f32)
        by_lo = lax.dot_general(wr_ref[0:N_EXPERTS, :], h2_lo, nt, preferred_element_type=f32)
        logits = by_hi[0:N_EXPERTS] + by_hi[N_EXPERTS:] + by_lo + br_ref[...]
        row = lax.broadcasted_iota(jnp.int32, (N_EXPERTS, sub_tm), 0)
        vals, idxs = [], []
        l = logits
        for _ in range(TOP_K):
            m = jnp.max(l, axis=0, keepdims=True)
            idx = jnp.min(jnp.where(l == m, row, N_EXPERTS), axis=0, keepdims=True)
            vals.append(m)
            idxs.append(idx)
            l = jnp.where(row == idx, -jnp.inf, l)
        es = [jnp.exp(v - vals[0]) for v in vals]
        inv = 1.0 / (es[0] + es[1] + es[2] + es[3])
        gate_ref[:, r0] = jnp.concatenate([e * inv for e in es], axis=0)

        onehots = [row == idx for idx in idxs]
        sel = jnp.logical_or(jnp.logical_or(onehots[0], onehots[1]), jnp.logical_or(onehots[2], onehots[3]))
        oh = jnp.where(sel, 1.0, 0.0)
        cnt_lane = lax.broadcasted_iota(jnp.int32, cnt_ref.shape, 1)
        poss = []
        for rt in range(sub_tm // ts):
            sl = slice(rt * ts, (rt + 1) * ts)
            oh_s = oh[:, sl]
            cum = jnp.dot(oh_s.astype(bf16), tri_scr[...], preferred_element_type=f32)
            cnt = jnp.sum(oh_s, axis=1, keepdims=True)
            padded = SUBLANES * jnp.floor((cnt + (SUBLANES - 1.0)) * (1.0 / SUBLANES))
            goff = _excl_cumsum(jnp.broadcast_to(padded, (N_EXPERTS, LANES)), 0)[:, 0:1]
            pos = goff + cum
            poss.append(jnp.concatenate(
                [jnp.sum(jnp.where(o[:, sl], pos, 0.0), axis=0, keepdims=True) for o in onehots], axis=0))
            tile_id = i * (tm // ts) + sub * (sub_tm // ts) + rt
            cnt_ref[...] = jnp.where(cnt_lane == tile_id, cnt.astype(jnp.int32), cnt_ref[...])
        pos_ref[:, r0] = jnp.concatenate(poss, axis=1).astype(jnp.int32)

    def interleave(main, fillers):
        after = [[] for _ in main]
        for n, filler in enumerate(fillers):
            after[n * len(main) // len(fillers)].append(filler)
        for stage, extra in zip(main, after):
            stage()
            for filler in extra:
                filler()

    def attention_stages(sub):
        blocks = range(sub * (sub_tm // BLOCK), (sub + 1) * (sub_tm // BLOCK))
        its = [(j, hh) for j in blocks for hh in range(N_KV_HEADS)]
        scores = {}

        def score(n):
            scores[n] = attn_scores(*its[n])

        def finish(n):
            attn_finish(*its[n], scores.pop(n))

        stages = [functools.partial(score, n) for n in range(min(ATTN_AHEAD, len(its)))]
        for n in range(len(its)):
            if n + ATTN_AHEAD < len(its):
                stages.append(functools.partial(score, n + ATTN_AHEAD))
            stages.append(functools.partial(finish, n))
        return stages

    prep(0)
    for c0, c1 in in_chunks:
        in_proj(0, c0, c1)
    kv_stage(0)
    for sub in range(n_sub):
        fillers = []
        if sub > 0:
            fillers.append(functools.partial(conv_mix, sub - 1))
            fillers += [functools.partial(out_proj, sub - 1, c0, c1) for c0, c1 in out_chunks]
        if sub + 1 < n_sub:
            prep(sub + 1)
            fillers += [functools.partial(in_proj, sub + 1, c0, c1) for c0, c1 in in_chunks]
        interleave(attention_stages(sub), fillers)
        if sub + 1 < n_sub:
            kv_stage(sub + 1)
        if 0 < sub < n_sub - 1:
            route(sub - 1)
    last = n_sub - 1
    conv_mix(last)
    out_proj(last, *out_chunks[0])
    if last > 0:
        route(last - 1)
    for c0, c1 in out_chunks[1:]:
        out_proj(last, c0, c1)
    route(last)

    k_scr[0:BLOCK, :] = k_scr[tm:tm + BLOCK, :]
    vt_scr[:, 0:BLOCK] = vt_scr[:, tm:tm + BLOCK]
    u_scr[0:SUBLANES, :] = u_scr[tm:tm + SUBLANES, :]


def _mixer_call(x2d, sinks, g_mix, w_in, conv_w, bias, g_attn, g_conv, w_out, g_ffn, wr_split, br, seq_len):
    n_tok = x2d.shape[0]
    tm = TOKEN_TILE
    n_tiles = n_tok // tm
    assert n_tok // ROUTE_TILE == LANES, "one lane of the count table per routing tile"
    const = lambda shape: pl.BlockSpec(shape, lambda i, *_: (0,) * len(shape))
    grid_spec = pltpu.PrefetchScalarGridSpec(
        num_scalar_prefetch=0,
        grid=(n_tiles,),
        in_specs=[
            pl.BlockSpec(memory_space=pltpu.SMEM),
            pl.BlockSpec((tm, D_MODEL), lambda i: (i, 0)),
            const((1, D_MODEL)), const((D_MODEL, IN_COLS)), const((3, CONV_WIDTH)),
            const((N_KV_HEADS, BAND, GROUP * BLOCK)),
            const((1, ATTN_WIDTH)), const((1, CONV_WIDTH)), const((D_MODEL, D_MODEL)),
            const((1, D_MODEL)), const((2 * N_EXPERTS, D_MODEL)), const((N_EXPERTS, 1)),
        ],
        out_specs=[
            pl.BlockSpec((tm, D_MODEL), lambda i: (i, 0)),
            pl.BlockSpec((tm, D_MODEL), lambda i: (i, 0)),
            pl.BlockSpec((TOP_K, tm), lambda i: (0, i)),
            pl.BlockSpec((TOP_K, tm), lambda i: (0, i)),
            pl.BlockSpec((N_EXPERTS, LANES), lambda i: (0, 0)),
        ],
        scratch_shapes=[
            pltpu.VMEM((tm, IN_COLS), jnp.float32),
            pltpu.VMEM((tm, D_MODEL), jnp.bfloat16),
            pltpu.VMEM((tm, D_MODEL), jnp.bfloat16),
            pltpu.VMEM((BLOCK + tm, 2 * LANES), jnp.bfloat16),
            pltpu.VMEM((KV_WIDTH, BLOCK + tm), jnp.bfloat16),
            pltpu.VMEM((SUBLANES + tm, CONV_WIDTH), jnp.float32),
            pltpu.VMEM((ATTN_WIDTH, tm), jnp.float32),
            pltpu.VMEM((ROUTE_TILE, ROUTE_TILE), jnp.bfloat16),
        ])
    out_shape = [
        jax.ShapeDtypeStruct((n_tok, D_MODEL), jnp.float32),
        jax.ShapeDtypeStruct((n_tok, D_MODEL), jnp.bfloat16),
        jax.ShapeDtypeStruct((TOP_K, n_tok), jnp.int32),
        jax.ShapeDtypeStruct((TOP_K, n_tok), jnp.float32),
        jax.ShapeDtypeStruct((N_EXPERTS, LANES), jnp.int32),
    ]
    return pl.pallas_call(
        functools.partial(_mixer_kernel, seq_len // tm),
        grid_spec=grid_spec, out_shape=out_shape, name="mixer_router",
        compiler_params=pltpu.CompilerParams(dimension_semantics=("arbitrary",),
                                             vmem_limit_bytes=VMEM_LIMIT_BYTES),
    )(sinks, x2d, g_mix, w_in, conv_w, bias, g_attn, g_conv, w_out, g_ffn, wr_split, br)


def _tables_kernel(cnt_ref, size_ref, goff_ref, gdst_ref, blk_ref, used_ref,
                   span_ref, soff_ref, src_ref, delta_ref):
    size = ((cnt_ref[...] + (SUBLANES - 1)) // SUBLANES) * SUBLANES
    size_ref[...] = size
    goff = _excl_cumsum(size, 0)
    goff_ref[...] = goff
    total = jnp.broadcast_to(jnp.sum(size, axis=1, keepdims=True), size.shape)
    region = ((total + (ROW_BLOCK - 1)) // ROW_BLOCK) * ROW_BLOCK
    pstart = _excl_cumsum(region, 0)
    gdst = pstart + _excl_cumsum(size, 1)
    gdst_ref[...] = gdst
    src = (gdst // PACK_ROWS) * PACK_ROWS
    span = jnp.where(size > 0, ((gdst + size + (PACK_ROWS - 1)) // PACK_ROWS) * PACK_ROWS - src, 0)
    soff = _excl_cumsum(span, 0)
    span_ref[...] = span
    soff_ref[...] = soff
    src_ref[...] = src
    delta_ref[...] = soff + (gdst - src) - goff
    pend = (pstart + region)[:, 0:1]
    start = lax.broadcasted_iota(jnp.int32, (N_EXPERTS, blk_ref.shape[1]), 1) * ROW_BLOCK
    owner = jnp.sum(jnp.where(pend <= start, 1, 0), axis=0, keepdims=True)
    blk_ref[0:1, :] = jnp.minimum(owner, N_EXPERTS - 1)
    first_row, last_row = pstart[:, 0:1], (pstart + total)[:, 0:1]
    inside = jnp.logical_and(first_row <= start, start < pend)
    blk_ref[1:2, :] = jnp.sum(jnp.where(inside, jnp.minimum(last_row - start, ROW_BLOCK), 0),
                              axis=0, keepdims=True)
    used_ref[...] = jnp.broadcast_to(jnp.max(pend, axis=0, keepdims=True) // ROW_BLOCK, used_ref.shape)


def _tables_call(cnt, n_blocks):
    nb_pad = -(-n_blocks // LANES) * LANES
    tbl = jax.ShapeDtypeStruct((N_EXPERTS, LANES), jnp.int32)
    return pl.pallas_call(
        _tables_kernel,
        out_shape=[tbl, tbl, tbl, jax.ShapeDtypeStruct((2, nb_pad), jnp.int32),
                   jax.ShapeDtypeStruct((1, LANES), jnp.int32), tbl, tbl, tbl, tbl],
        name="group_tables",
    )(cnt)


def _start_group_copies(tile, size_ref, goff_ref, gdst_ref, sem, make_copy, align):
    def body(e, rows):
        n = pl.multiple_of(size_ref[e * LANES + tile], align)
        @pl.when(n > 0)
        def _():
            off = pl.multiple_of(goff_ref[e * LANES + tile], align)
            dst = pl.multiple_of(gdst_ref[e * LANES + tile], align)
            make_copy(off, dst, n, sem).start()
        return rows + n
    return lax.fori_loop(0, N_EXPERTS, body, jnp.int32(0))


def _wait_rows(rows_hbm, n, sem, align):
    @pl.when(n > 0)
    def _():
        m = pl.multiple_of(n, align)
        half = rows_hbm.shape[0] // (2 * align) * align
        pltpu.make_async_copy(rows_hbm.at[pl.ds(0, m)], rows_hbm.at[pl.ds(half, m)], sem).wait()


def _dispatch_kernel(n_blocks,
                     size_ref, goff_ref, gdst_ref, used_ref,
                     h2_ref, pos_ref, rows_ref,
                     stage, zbuf, inflight, zeroed, sems, zsem):
    t = pl.program_id(0)
    n_tiles = pl.num_programs(0)
    slot = t % 2

    @pl.when(t == 0)
    def _():
        zbuf[...] = jnp.zeros_like(zbuf)
        def zfill(e, n):
            first = gdst_ref[e * LANES]
            last = gdst_ref[e * LANES + LANES - 1] + size_ref[e * LANES + LANES - 1]
            pad = pl.multiple_of((last - first + ROW_BLOCK - 1) // ROW_BLOCK * ROW_BLOCK - (last - first), SUBLANES)
            @pl.when(pad > 0)
            def _():
                pltpu.make_async_copy(zbuf.at[pl.ds(0, pad)],
                                      rows_ref.at[pl.ds(pl.multiple_of(last, SUBLANES), pad)], zsem).start()
            return n + pad
        n_zero = lax.fori_loop(0, N_EXPERTS, zfill, jnp.int32(0))
        def ztail(b, _):
            pltpu.make_async_copy(
                zbuf, rows_ref.at[pl.ds(pl.multiple_of(b * ROW_BLOCK, ROW_BLOCK), ROW_BLOCK)], zsem).start()
            return 0
        lax.fori_loop(used_ref[0], n_blocks, ztail, 0)
        zeroed[0] = n_zero + (n_blocks - used_ref[0]) * ROW_BLOCK

    @pl.when(t >= 2)
    def _():
        _wait_rows(rows_ref, inflight[slot], sems.at[slot], SUBLANES)

    pos = pos_ref[...]
    r = lax.broadcasted_iota(jnp.int32, (STAGE_ROWS, pos.shape[1]), 0)
    hit = jnp.logical_or(jnp.logical_or(r == pos[0:1], r == pos[1:2]),
                         jnp.logical_or(r == pos[2:3], r == pos[3:4]))
    perm = jnp.where(hit, 1.0, 0.0).astype(jnp.bfloat16)
    stage[slot] = jnp.dot(perm, h2_ref[...], preferred_element_type=jnp.float32)

    def make_copy(off, dst, n, sem):
        return pltpu.make_async_copy(stage.at[slot, pl.ds(off, n)], rows_ref.at[pl.ds(dst, n)], sem)
    inflight[slot] = _start_group_copies(t, size_ref, goff_ref, gdst_ref, sems.at[slot], make_copy, SUBLANES)

    @pl.when(t == n_tiles - 1)
    def _():
        _wait_rows(rows_ref, inflight[1 - slot], sems.at[1 - slot], SUBLANES)
        _wait_rows(rows_ref, inflight[slot], sems.at[slot], SUBLANES)
        _wait_rows(rows_ref, zeroed[0], zsem, SUBLANES)


def _dispatch_call(size, goff, gdst, used, h2, pos, n_rows):
    n_tok = h2.shape[0]
    ts = ROUTE_TILE
    n_blocks = n_rows // ROW_BLOCK
    grid_spec = pltpu.PrefetchScalarGridSpec(
        num_scalar_prefetch=4,
        grid=(n_tok // ts,),
        in_specs=[
            pl.BlockSpec((ts, D_MODEL), lambda t, *_: (t, 0)),
            pl.BlockSpec((TOP_K, ts), lambda t, *_: (0, t)),
        ],
        out_specs=pl.BlockSpec(memory_space=pl.ANY),
        scratch_shapes=[
            pltpu.VMEM((2, STAGE_ROWS, D_MODEL), jnp.float32),
            pltpu.VMEM((ROW_BLOCK, D_MODEL), jnp.float32),
            pltpu.SMEM((2,), jnp.int32),
            pltpu.SMEM((1,), jnp.int32),
            pltpu.SemaphoreType.DMA((2,)),
            pltpu.SemaphoreType.DMA(()),
        ])
    return pl.pallas_call(
        functools.partial(_dispatch_kernel, n_blocks),
        grid_spec=grid_spec,
        out_shape=jax.ShapeDtypeStruct((n_rows, D_MODEL), jnp.float32),
        name="dispatch_rows",
        compiler_params=pltpu.CompilerParams(dimension_semantics=("arbitrary",),
                                             vmem_limit_bytes=VMEM_LIMIT_BYTES),
    )(size, goff, gdst, used, h2, pos)


def _expert_kernel(blk_ref, used_ref, valid_ref, x_ref, w1_hbm, b1_ref, w2_hbm, b2_ref, y_ref,
                   w1_f32, w2_f32, w1_bf, w2_bf, state, sems):
    i = pl.program_id(0)
    bf16 = jnp.bfloat16
    e = blk_ref[i]
    used = used_ref[0]

    def weight_copies(expert, slot):
        return (pltpu.make_async_copy(w1_hbm.at[expert], w1_f32.at[slot], sems.at[0, slot]),
                pltpu.make_async_copy(w2_hbm.at[expert], w2_f32.at[slot], sems.at[1, slot]))

    @pl.when(i == 0)
    def _():
        state[0] = jnp.int32(-1)
        state[1] = jnp.int32(0)
        @pl.when(used > 0)
        def _():
            for cp in weight_copies(e, 0):
                cp.start()

    @pl.when(jnp.logical_and(i < used, e != state[0]))
    def _():
        slot = state[1]
        for cp in weight_copies(e, slot):
            cp.wait()
        w1_bf[...] = w1_f32[slot].astype(bf16)
        w2_bf[...] = w2_f32[slot].astype(bf16)
        nxt = lax.while_loop(lambda j: jnp.logical_and(j < used, blk_ref[jnp.minimum(j, used - 1)] == e),
                             lambda j: j + 1, i + 1)
        @pl.when(nxt < used)
        def _():
            for cp in weight_copies(blk_ref[nxt], 1 - slot):
                cp.start()
        state[0] = e
        state[1] = 1 - slot

    @pl.when(i >= used)
    def _():
        y_ref[...] = jnp.zeros_like(y_ref)

    def ffn(rows):
        hu = jnp.dot(x_ref[rows, :].astype(bf16), w1_bf[...], preferred_element_type=jnp.float32) + b1_ref[0]
        glu = jnp.minimum(hu[:, :D_FF], SWIGLU_LIMIT)
        lin = jnp.clip(hu[:, D_FF:], -SWIGLU_LIMIT, SWIGLU_LIMIT)
        act = glu * (1.0 / (1.0 + jnp.exp(-SWIGLU_ALPHA * glu))) * (lin + 1.0)
        y = jnp.dot(act.astype(bf16), w2_bf[...], preferred_element_type=jnp.float32) + b2_ref[0]
        y_ref[rows, :] = y.astype(bf16)

    half = ROW_BLOCK // 2
    valid = valid_ref[i]

    @pl.when(jnp.logical_and(i < used, valid > half))
    def _():
        ffn(slice(0, ROW_BLOCK))

    @pl.when(jnp.logical_and(i < used, valid <= half))
    def _():
        ffn(slice(0, half))
        y_ref[half:, :] = jnp.zeros((ROW_BLOCK - half, D_MODEL), bf16)


def _expert_call(blk, used, valid, rows, w1, b1, w2, b2):
    n_rows = rows.shape[0]
    n_blocks = n_rows // ROW_BLOCK
    last = lambda i, u: jnp.maximum(jnp.minimum(i, u[0] - 1), 0)
    grid_spec = pltpu.PrefetchScalarGridSpec(
        num_scalar_prefetch=3,
        grid=(n_blocks,),
        in_specs=[
            pl.BlockSpec((ROW_BLOCK, D_MODEL), lambda i, b, u, v: (last(i, u), 0)),
            pl.BlockSpec(memory_space=pl.ANY),
            pl.BlockSpec((1, 1, 2 * D_FF), lambda i, b, u, v: (b[i], 0, 0)),
            pl.BlockSpec(memory_space=pl.ANY),
            pl.BlockSpec((1, 1, D_MODEL), lambda i, b, u, v: (b[i], 0, 0)),
        ],
        out_specs=pl.BlockSpec((ROW_BLOCK, D_MODEL), lambda i, b, u, v: (i, 0)),
        scratch_shapes=[
            pltpu.VMEM((2, D_MODEL, 2 * D_FF), jnp.float32),
            pltpu.VMEM((2, D_FF, D_MODEL), jnp.float32),
            pltpu.VMEM((D_MODEL, 2 * D_FF), jnp.bfloat16),
            pltpu.VMEM((D_FF, D_MODEL), jnp.bfloat16),
            pltpu.SMEM((2,), jnp.int32),
            pltpu.SemaphoreType.DMA((2, 2)),
        ])
    return pl.pallas_call(
        _expert_kernel,
        grid_spec=grid_spec,
        out_shape=jax.ShapeDtypeStruct((n_rows, D_MODEL), jnp.bfloat16),
        name="expert_ffn",
        compiler_params=pltpu.CompilerParams(dimension_semantics=("arbitrary",),
                                             vmem_limit_bytes=VMEM_LIMIT_BYTES),
    )(blk, used, valid, rows, w1, b1.reshape(N_EXPERTS, 1, 2 * D_FF), w2, b2.reshape(N_EXPERTS, 1, D_MODEL))


def _combine_kernel(span_ref, soff_ref, src_ref, goff_ref, delta_ref,
                    x1_ref, pos_ref, gate_ref, y_ref, gfin_ref, out_ref,
                    stage, inflight, sems):
    t = pl.program_id(0)
    n_tiles = pl.num_programs(0)
    slot = t % 2

    def fetch(tile, slot_):
        def make_copy(off, src, n, sem):
            return pltpu.make_async_copy(y_ref.at[pl.ds(src, n)], stage.at[slot_, pl.ds(off, n)], sem)
        inflight[slot_] = _start_group_copies(tile, span_ref, soff_ref, src_ref, sems.at[slot_], make_copy,
                                              PACK_ROWS)

    @pl.when(t == 0)
    def _():
        stage[...] = jnp.zeros_like(stage)
        fetch(0, 0)

    @pl.when(t + 1 < n_tiles)
    def _():
        fetch(t + 1, 1 - slot)

    _wait_rows(y_ref, inflight[slot], sems.at[slot], PACK_ROWS)

    pos = pos_ref[...]
    shift = jnp.zeros_like(pos)
    prev = jnp.int32(0)
    for e in range(N_EXPERTS):
        d = delta_ref[e * LANES + t]
        shift = shift + jnp.where(pos >= goff_ref[e * LANES + t], d - prev, 0)
        prev = d
    pos = pos + shift
    gate = gate_ref[...]
    r = lax.broadcasted_iota(jnp.int32, (COMBINE_ROWS, pos.shape[1]), 0)
    w = jnp.zeros(r.shape, jnp.float32)
    for k in range(TOP_K):
        w = jnp.where(r == pos[k:k + 1], gate[k:k + 1], w)
    moe = lax.dot_general(w.astype(jnp.bfloat16), stage[slot],
                          (((0,), (0,)), ((), ())), preferred_element_type=jnp.float32)
    out_ref[...] = _rms(x1_ref[...] + moe, gfin_ref[...])


def _combine_call(span, soff, src, goff, delta, x1, pos, gate, y_rows, g_final):
    n_tok = x1.shape[0]
    ts = ROUTE_TILE
    grid_spec = pltpu.PrefetchScalarGridSpec(
        num_scalar_prefetch=5,
        grid=(n_tok // ts,),
        in_specs=[
            pl.BlockSpec((ts, D_MODEL), lambda t, *_: (t, 0)),
            pl.BlockSpec((TOP_K, ts), lambda t, *_: (0, t)),
            pl.BlockSpec((TOP_K, ts), lambda t, *_: (0, t)),
            pl.BlockSpec(memory_space=pl.ANY),
            pl.BlockSpec((1, D_MODEL), lambda t, *_: (0, 0)),
        ],
        out_specs=pl.BlockSpec((ts, D_MODEL), lambda t, *_: (t, 0)),
        scratch_shapes=[
            pltpu.VMEM((2, COMBINE_ROWS, D_MODEL), jnp.bfloat16),
            pltpu.SMEM((2,), jnp.int32),
            pltpu.SemaphoreType.DMA((2,)),
        ])
    return pl.pallas_call(
        _combine_kernel,
        grid_spec=grid_spec,
        out_shape=jax.ShapeDtypeStruct((n_tok, D_MODEL), jnp.float32),
        name="combine_norm",
        compiler_params=pltpu.CompilerParams(dimension_semantics=("arbitrary",),
                                             vmem_limit_bytes=VMEM_LIMIT_BYTES),
    )(span, soff, src, goff, delta, x1, pos, gate, y_rows, g_final)


def _attention_bias():
    slopes = jnp.exp2(-8.0 * jnp.arange(1, N_Q_HEADS + 1, dtype=jnp.float32) / N_Q_HEADS)
    qi = jnp.arange(BLOCK)[:, None]
    kj = jnp.arange(BAND)[None, :]
    dist = qi + BLOCK - kj
    valid = (dist >= 0) & (dist < WINDOW)
    b = -slopes[:, None, None] * dist.astype(jnp.float32)[None]
    b = jnp.where(valid[None], b, -jnp.inf)
    b = b.reshape(N_KV_HEADS, GROUP, BLOCK, BAND).transpose(0, 3, 1, 2)
    return b.reshape(N_KV_HEADS, BAND, GROUP * BLOCK)


def kernel(x, g_mix, w_in, conv_w, sinks, g_attn_out, g_conv_out, w_out, g_ffn, w_router, b_router,
           w1, b1, w2, b2, g_final):
    bsz, s_len, d = x.shape
    n_tok = bsz * s_len
    n_tiles = n_tok // ROUTE_TILE
    worst = n_tok * TOP_K + n_tiles * N_EXPERTS * (SUBLANES - 1) + N_EXPERTS * (ROW_BLOCK - SUBLANES)
    n_blocks = -(-worst // ROW_BLOCK)
    n_rows = n_blocks * ROW_BLOCK
    x2d = x.reshape(n_tok, d)
    bias = _attention_bias()
    l = 0
    wrt = w_router[l].T
    wr_hi = wrt.astype(jnp.bfloat16)
    wr_lo = (wrt - wr_hi.astype(jnp.float32)).astype(jnp.bfloat16)
    wr_split = jnp.concatenate([wr_hi, wr_lo], axis=0)
    x1, h2, pos, gate, cnt = _mixer_call(
        x2d, sinks[l], g_mix[l].reshape(1, d), w_in[l].astype(jnp.bfloat16), conv_w[l], bias,
        g_attn_out[l].reshape(1, ATTN_WIDTH), g_conv_out[l].reshape(1, CONV_WIDTH),
        w_out[l].astype(jnp.bfloat16), g_ffn[l].reshape(1, d), wr_split,
        b_router[l].reshape(N_EXPERTS, 1), s_len)
    size, goff, gdst, blk, used, span, soff, src, delta = _tables_call(cnt, n_blocks)
    size, goff, gdst = size.reshape(-1), goff.reshape(-1), gdst.reshape(-1)
    span, soff, src, delta = span.reshape(-1), soff.reshape(-1), src.reshape(-1), delta.reshape(-1)
    blk, valid, used = blk[0], blk[1], used.reshape(-1)[:1]
    rows = _dispatch_call(size, goff, gdst, used, h2, pos, n_rows)
    y_rows = _expert_call(blk, used, valid, rows, w1[l], b1[l], w2[l], b2[l])
    out = _combine_call(span, soff, src, goff, delta, x1, pos, gate, y_rows, g_final.reshape(1, d))
    return out.reshape(bsz, s_len, d)
```

```python
import functools
import math

import jax
import jax.numpy as jnp
from jax import lax
from jax.experimental import pallas as pl
from jax.experimental.pallas import tpu as pltpu

D_MODEL = 1024
N_Q_HEADS = 8
N_KV_HEADS = 2
HEAD_DIM = 64
GROUP = N_Q_HEADS // N_KV_HEADS
ATTN_WIDTH = N_Q_HEADS * HEAD_DIM
KV_WIDTH = N_KV_HEADS * HEAD_DIM
CONV_WIDTH = D_MODEL - ATTN_WIDTH
QKV_COLS = ATTN_WIDTH + 2 * KV_WIDTH
IN_COLS = QKV_COLS + 3 * CONV_WIDTH
WINDOW = 128
BLOCK = 128
BAND = 2 * BLOCK
N_EXPERTS = 32
TOP_K = 4
D_FF = D_MODEL
SWIGLU_ALPHA = 1.702
SWIGLU_LIMIT = 7.0
EPS = 1e-5

LANES = 128
SUBLANES = 8
VMEM_LIMIT_BYTES = 56 * 1024 * 1024

TOKEN_TILE = 1024
SUB_TILE = 512
ATTN_AHEAD = 1
ROUTE_TILE = 256
ROW_BLOCK = 512
STAGE_ROWS = -(-(TOP_K * ROUTE_TILE + N_EXPERTS * (SUBLANES - 1)) // LANES) * LANES
PACK_ROWS = 16
COMBINE_ROWS = -(-(TOP_K * ROUTE_TILE + N_EXPERTS * (SUBLANES - 1 + PACK_ROWS)) // LANES) * LANES


def _rms(x, g):
    return x * lax.rsqrt(jnp.mean(x * x, axis=-1, keepdims=True) + EPS) * g


def _excl_cumsum(a, axis):
    idx = lax.broadcasted_iota(jnp.int32, a.shape, axis)
    inc = a
    step = 1
    while step < a.shape[axis]:
        inc = inc + jnp.where(idx >= step, pltpu.roll(inc, step, axis), 0)
        step *= 2
    return inc - a


def _mixer_kernel(tiles_per_seq,
                  sink_ref,
                  x_ref, gmix_ref, win_ref, convw_ref, bias_ref, gattn_ref, gconv_ref, wout_ref,
                  gffn_ref, wr_ref, br_ref,
                  x1_ref, h2_ref, pos_ref, gate_ref, cnt_ref,
                  p_scr, h_scr, mix_scr, k_scr, vt_scr, u_scr, attn_t_scr, tri_scr):
    tm = x_ref.shape[0]
    sub_tm = SUB_TILE
    ts = ROUTE_TILE
    i = pl.program_id(0)
    first = (i % tiles_per_seq) == 0
    f32, bf16 = jnp.float32, jnp.bfloat16

    @pl.when(i == 0)
    def _():
        cnt_ref[...] = jnp.zeros_like(cnt_ref)
        r = lax.broadcasted_iota(jnp.int32, (ts, ts), 0)
        c = lax.broadcasted_iota(jnp.int32, (ts, ts), 1)
        tri_scr[...] = jnp.where(r < c, 1.0, 0.0).astype(bf16)

    @pl.when(first)
    def _():
        k_scr[0:BLOCK, :] = jnp.zeros((BLOCK, k_scr.shape[1]), bf16)
        vt_scr[:, 0:BLOCK] = jnp.zeros((KV_WIDTH, BLOCK), bf16)
        u_scr[0:SUBLANES, :] = jnp.zeros((SUBLANES, CONV_WIDTH), f32)

    n_sub = tm // sub_tm
    in_chunks = [(c, c + 2 * LANES) for c in range(0, IN_COLS, 2 * LANES)]
    out_chunks = [(c, c + 2 * LANES) for c in range(0, D_MODEL, 2 * LANES)]
    lane = lax.broadcasted_iota(jnp.int32, (BAND, LANES), 1)
    lo = lane < HEAD_DIM
    qgrp = lax.broadcasted_iota(jnp.int32, (BLOCK, 2 * LANES), 1) // HEAD_DIM
    key_row = lax.broadcasted_iota(jnp.int32, (BAND, GROUP * BLOCK), 0)
    scale = 1.0 / math.sqrt(HEAD_DIM)
    zero_q = jnp.zeros((BLOCK, 2 * LANES), bf16)
    nt = (((1,), (1,)), ((), ()))

    def rows_of(sub):
        return slice(sub * sub_tm, (sub + 1) * sub_tm)

    def prep(sub):
        r0 = rows_of(sub)
        h_scr[r0, :] = _rms(x_ref[r0, :], gmix_ref[...]).astype(bf16)

    def in_proj(sub, c0, c1):
        r0 = rows_of(sub)
        p_scr[r0, c0:c1] = jnp.dot(h_scr[r0, :], win_ref[:, c0:c1], preferred_element_type=f32)

    def kv_stage(sub):
        r0 = rows_of(sub)
        base = sub * sub_tm
        kk = p_scr[r0, ATTN_WIDTH:ATTN_WIDTH + KV_WIDTH]
        kr = pltpu.roll(kk, HEAD_DIM, axis=1)
        k_scr[BLOCK + base:BLOCK + base + sub_tm, :] = jnp.concatenate([kk, kr], axis=1).astype(bf16)
        vt_scr[:, BLOCK + base:BLOCK + base + sub_tm] = jnp.transpose(
            p_scr[r0, ATTN_WIDTH + KV_WIDTH:QKV_COLS]).astype(bf16)

    def attn_scores(j, hh):
        rows = slice(j * BLOCK, (j + 1) * BLOCK)
        kband = k_scr[j * BLOCK:j * BLOCK + BAND, :]
        bk, bkr = kband[:, 0:LANES], kband[:, LANES:2 * LANES]
        k_both = jnp.where(lo, bk, bkr) if hh == 0 else jnp.where(lo, bkr, bk)
        kmat = jnp.concatenate([k_both, k_both], axis=1)
        qh = (p_scr[rows, hh * 2 * LANES:(hh + 1) * 2 * LANES] * scale).astype(bf16)
        qm = jnp.concatenate([jnp.where(qgrp == g, qh, zero_q) for g in range(GROUP)], axis=0)
        s = lax.dot_general(kmat, qm, nt, preferred_element_type=f32) + bias_ref[hh]
        if j == 0:
            s = jnp.where(jnp.logical_and(first, key_row < BLOCK), -jnp.inf, s)
        return s

    def attn_finish(j, hh, s):
        rows = slice(j * BLOCK, (j + 1) * BLOCK)
        vt_band = vt_scr[:, j * BLOCK:j * BLOCK + BAND]
        sink = jnp.concatenate(
            [jnp.full((1, BLOCK), sink_ref[hh * GROUP + g], f32) for g in range(GROUP)], axis=1)
        m = jnp.maximum(jnp.max(s, axis=0, keepdims=True), sink)
        e = jnp.exp(s - m)
        inv = 1.0 / (jnp.sum(e, axis=0, keepdims=True) + jnp.exp(sink - m))
        o_t = jnp.dot(vt_band, e.astype(bf16), preferred_element_type=f32)
        o_t = o_t[hh * HEAD_DIM:(hh + 1) * HEAD_DIM, :] * inv
        for g in range(GROUP):
            head = hh * GROUP + g
            attn_t_scr[head * HEAD_DIM:(head + 1) * HEAD_DIM, rows] = o_t[:, g * BLOCK:(g + 1) * BLOCK]

    def conv_mix(sub):
        r0 = rows_of(sub)
        base = sub * sub_tm
        bg = p_scr[r0, QKV_COLS:QKV_COLS + CONV_WIDTH]
        cg = p_scr[r0, QKV_COLS + CONV_WIDTH:QKV_COLS + 2 * CONV_WIDTH]
        xin = p_scr[r0, QKV_COLS + 2 * CONV_WIDTH:IN_COLS]
        u = cg * xin
        u_scr[SUBLANES + base:SUBLANES + base + sub_tm, :] = u
        u1 = u_scr[SUBLANES - 1 + base:SUBLANES - 1 + base + sub_tm, :]
        u2 = u_scr[SUBLANES - 2 + base:SUBLANES - 2 + base + sub_tm, :]
        y = convw_ref[0:1, :] * u2 + convw_ref[1:2, :] * u1 + convw_ref[2:3, :] * u
        conv = bg * y
        mix_scr[r0, :] = jnp.concatenate(
            [_rms(jnp.transpose(attn_t_scr[:, r0]), gattn_ref[...]), _rms(conv, gconv_ref[...])],
            axis=1).astype(bf16)

    def out_proj(sub, c0, c1):
        r0 = rows_of(sub)
        x1_ref[r0, c0:c1] = x_ref[r0, c0:c1] + jnp.dot(mix_scr[r0, :], wout_ref[:, c0:c1],
                                                      preferred_element_type=f32)

    def route(sub):
        r0 = rows_of(sub)
        h2 = _rms(x1_ref[r0, :], gffn_ref[...])
        h2_hi = h2.astype(bf16)
        h2_ref[r0, :] = h2_hi
        h2_lo = (h2 - h2_hi.astype(f32)).astype(bf16)
        by_hi = lax.dot_general(wr_ref[...], h2_hi, nt, preferred_element_type=f32)
        by_lo = lax.dot_general(wr_ref[0:N_EXPERTS, :], h2_lo, nt, preferred_element_type=f32)
        logits = by_hi[0:N_EXPERTS] + by_hi[N_EXPERTS:] + by_lo + br_ref[...]
        row = lax.broadcasted_iota(jnp.int32, (N_EXPERTS, sub_tm), 0)
        vals, idxs = [], []
        l = logits
        for _ in range(TOP_K):
            m = jnp.max(l, axis=0, keepdims=True)
            idx = jnp.min(jnp.where(l == m, row, N_EXPERTS), axis=0, keepdims=True)
            vals.append(m)
            idxs.append(idx)
            l = jnp.where(row == idx, -jnp.inf, l)
        es = [jnp.exp(v - vals[0]) for v in vals]
        inv = 1.0 / (es[0] + es[1] + es[2] + es[3])
        gate_ref[:, r0] = jnp.concatenate([e * inv for e in es], axis=0)

        onehots = [row == idx for idx in idxs]
        sel = jnp.logical_or(jnp.logical_or(onehots[0], onehots[1]), jnp.logical_or(onehots[2], onehots[3]))
        oh = jnp.where(sel, 1.0, 0.0)
        cnt_lane = lax.broadcasted_iota(jnp.int32, cnt_ref.shape, 1)
        poss = []
        for rt in range(sub_tm // ts):
            sl = slice(rt * ts, (rt + 1) * ts)
            oh_s = oh[:, sl]
            cum = jnp.dot(oh_s.astype(bf16), tri_scr[...], preferred_element_type=f32)
            cnt = jnp.sum(oh_s, axis=1, keepdims=True)
            padded = SUBLANES * jnp.floor((cnt + (SUBLANES - 1.0)) * (1.0 / SUBLANES))
            goff = _excl_cumsum(jnp.broadcast_to(padded, (N_EXPERTS, LANES)), 0)[:, 0:1]
            pos = goff + cum
            poss.append(jnp.concatenate(
                [jnp.sum(jnp.where(o[:, sl], pos, 0.0), axis=0, keepdims=True) for o in onehots], axis=0))
            tile_id = i * (tm // ts) + sub * (sub_tm // ts) + rt
            cnt_ref[...] = jnp.where(cnt_lane == tile_id, cnt.astype(jnp.int32), cnt_ref[...])
        pos_ref[:, r0] = jnp.concatenate(poss, axis=1).astype(jnp.int32)

    def interleave(main, fillers):
        after = [[] for _ in main]
        for n, filler in enumerate(fillers):
            after[n * len(main) // len(fillers)].append(filler)
        for stage, extra in zip(main, after):
            stage()
            for filler in extra:
                filler()

    def attention_stages(sub):
        blocks = range(sub * (sub_tm // BLOCK), (sub + 1) * (sub_tm // BLOCK))
        its = [(j, hh) for j in blocks for hh in range(N_KV_HEADS)]
        scores = {}

        def score(n):
            scores[n] = attn_scores(*its[n])

        def finish(n):
            attn_finish(*its[n], scores.pop(n))

        stages = [functools.partial(score, n) for n in range(min(ATTN_AHEAD, len(its)))]
        for n in range(len(its)):
            if n + ATTN_AHEAD < len(its):
                stages.append(functools.partial(score, n + ATTN_AHEAD))
            stages.append(functools.partial(finish, n))
        return stages

    prep(0)
    for c0, c1 in in_chunks:
        in_proj(0, c0, c1)
    kv_stage(0)
    for sub in range(n_sub):
        fillers = []
        if sub > 0:
            fillers.append(functools.partial(conv_mix, sub - 1))
            fillers += [functools.partial(out_proj, sub - 1, c0, c1) for c0, c1 in out_chunks]
        if sub + 1 < n_sub:
            prep(sub + 1)
            fillers += [functools.partial(in_proj, sub + 1, c0, c1) for c0, c1 in in_chunks]
        interleave(attention_stages(sub), fillers)
        if sub + 1 < n_sub:
            kv_stage(sub + 1)
        if 0 < sub < n_sub - 1:
            route(sub - 1)
    last = n_sub - 1
    conv_mix(last)
    out_proj(last, *out_chunks[0])
    if last > 0:
        route(last - 1)
    for c0, c1 in out_chunks[1:]:
        out_proj(last, c0, c1)
    route(last)

    k_scr[0:BLOCK, :] = k_scr[tm:tm + BLOCK, :]
    vt_scr[:, 0:BLOCK] = vt_scr[:, tm:tm + BLOCK]
    u_scr[0:SUBLANES, :] = u_scr[tm:tm + SUBLANES, :]


def _mixer_call(x2d, sinks, g_mix, w_in, conv_w, bias, g_attn, g_conv, w_out, g_ffn, wr_split, br, seq_len):
    n_tok = x2d.shape[0]
    tm = TOKEN_TILE
    n_tiles = n_tok // tm
    assert n_tok // ROUTE_TILE == LANES, "one lane of the count table per routing tile"
    const = lambda shape: pl.BlockSpec(shape, lambda i, *_: (0,) * len(shape))
    grid_spec = pltpu.PrefetchScalarGridSpec(
        num_scalar_prefetch=0,
        grid=(n_tiles,),
        in_specs=[
            pl.BlockSpec(memory_space=pltpu.SMEM),
            pl.BlockSpec((tm, D_MODEL), lambda i: (i, 0)),
            const((1, D_MODEL)), const((D_MODEL, IN_COLS)), const((3, CONV_WIDTH)),
            const((N_KV_HEADS, BAND, GROUP * BLOCK)),
            const((1, ATTN_WIDTH)), const((1, CONV_WIDTH)), const((D_MODEL, D_MODEL)),
            const((1, D_MODEL)), const((2 * N_EXPERTS, D_MODEL)), const((N_EXPERTS, 1)),
        ],
        out_specs=[
            pl.BlockSpec((tm, D_MODEL), lambda i: (i, 0)),
            pl.BlockSpec((tm, D_MODEL), lambda i: (i, 0)),
            pl.BlockSpec((TOP_K, tm), lambda i: (0, i)),
            pl.BlockSpec((TOP_K, tm), lambda i: (0, i)),
            pl.BlockSpec((N_EXPERTS, LANES), lambda i: (0, 0)),
        ],
        scratch_shapes=[
            pltpu.VMEM((tm, IN_COLS), jnp.float32),
            pltpu.VMEM((tm, D_MODEL), jnp.bfloat16),
            pltpu.VMEM((tm, D_MODEL), jnp.bfloat16),
            pltpu.VMEM((BLOCK + tm, 2 * LANES), jnp.bfloat16),
            pltpu.VMEM((KV_WIDTH, BLOCK + tm), jnp.bfloat16),
            pltpu.VMEM((SUBLANES + tm, CONV_WIDTH), jnp.float32),
            pltpu.VMEM((ATTN_WIDTH, tm), jnp.float32),
            pltpu.VMEM((ROUTE_TILE, ROUTE_TILE), jnp.bfloat16),
        ])
    out_shape = [
        jax.ShapeDtypeStruct((n_tok, D_MODEL), jnp.float32),
        jax.ShapeDtypeStruct((n_tok, D_MODEL), jnp.bfloat16),
        jax.ShapeDtypeStruct((TOP_K, n_tok), jnp.int32),
        jax.ShapeDtypeStruct((TOP_K, n_tok), jnp.float32),
        jax.ShapeDtypeStruct((N_EXPERTS, LANES), jnp.int32),
    ]
    return pl.pallas_call(
        functools.partial(_mixer_kernel, seq_len // tm),
        grid_spec=grid_spec, out_shape=out_shape, name="mixer_router",
        compiler_params=pltpu.CompilerParams(dimension_semantics=("arbitrary",),
                                             vmem_limit_bytes=VMEM_LIMIT_BYTES),
    )(sinks, x2d, g_mix, w_in, conv_w, bias, g_attn, g_conv, w_out, g_ffn, wr_split, br)


def _tables_kernel(cnt_ref, size_ref, goff_ref, gdst_ref, blk_ref, used_ref,
                   span_ref, soff_ref, src_ref, delta_ref):
    size = ((cnt_ref[...] + (SUBLANES - 1)) // SUBLANES) * SUBLANES
    size_ref[...] = size
    goff = _excl_cumsum(size, 0)
    goff_ref[...] = goff
    total = jnp.broadcast_to(jnp.sum(size, axis=1, keepdims=True), size.shape)
    region = ((total + (ROW_BLOCK - 1)) // ROW_BLOCK) * ROW_BLOCK
    pstart = _excl_cumsum(region, 0)
    gdst = pstart + _excl_cumsum(size, 1)
    gdst_ref[...] = gdst
    src = (gdst // PACK_ROWS) * PACK_ROWS
    span = jnp.where(size > 0, ((gdst + size + (PACK_ROWS - 1)) // PACK_ROWS) * PACK_ROWS - src, 0)
    soff = _excl_cumsum(span, 0)
    span_ref[...] = span
    soff_ref[...] = soff
    src_ref[...] = src
    delta_ref[...] = soff + (gdst - src) - goff
    pend = (pstart + region)[:, 0:1]
    start = lax.broadcasted_iota(jnp.int32, (N_EXPERTS, blk_ref.shape[1]), 1) * ROW_BLOCK
    owner = jnp.sum(jnp.where(pend <= start, 1, 0), axis=0, keepdims=True)
    blk_ref[0:1, :] = jnp.minimum(owner, N_EXPERTS - 1)
    first_row, last_row = pstart[:, 0:1], (pstart + total)[:, 0:1]
    inside = jnp.logical_and(first_row <= start, start < pend)
    blk_ref[1:2, :] = jnp.sum(jnp.where(inside, jnp.minimum(last_row - start, ROW_BLOCK), 0),
                              axis=0, keepdims=True)
    used_ref[...] = jnp.broadcast_to(jnp.max(pend, axis=0, keepdims=True) // ROW_BLOCK, used_ref.shape)


def _tables_call(cnt, n_blocks):
    nb_pad = -(-n_blocks // LANES) * LANES
    tbl = jax.ShapeDtypeStruct((N_EXPERTS, LANES), jnp.int32)
    return pl.pallas_call(
        _tables_kernel,
        out_shape=[tbl, tbl, tbl, jax.ShapeDtypeStruct((2, nb_pad), jnp.int32),
                   jax.ShapeDtypeStruct((1, LANES), jnp.int32), tbl, tbl, tbl, tbl],
        name="group_tables",
    )(cnt)


def _start_group_copies(tile, size_ref, goff_ref, gdst_ref, sem, make_copy, align):
    rows = jnp.int32(0)
    for e in range(N_EXPERTS):
        n = pl.multiple_of(size_ref[e * LANES + tile], align)
        @pl.when(n > 0)
        def _():
            off = pl.multiple_of(goff_ref[e * LANES + tile], align)
            dst = pl.multiple_of(gdst_ref[e * LANES + tile], align)
            make_copy(off, dst, n, sem).start(priority=e % 2)
        rows = rows + n
    return rows


def _wait_rows(rows_hbm, n, sem, align):
    @pl.when(n > 0)
    def _():
        m = pl.multiple_of(n, align)
        half = rows_hbm.shape[0] // (2 * align) * align
        pltpu.make_async_copy(rows_hbm.at[pl.ds(0, m)], rows_hbm.at[pl.ds(half, m)], sem).wait()


def _dispatch_kernel(n_blocks,
                     size_ref, goff_ref, gdst_ref, used_ref,
                     h2_ref, pos_ref, rows_ref,
                     stage, zbuf, inflight, zeroed, sems, zsem):
    t = pl.program_id(0)
    n_tiles = pl.num_programs(0)
    slot = t % 2

    @pl.when(t == 0)
    def _():
        zbuf[...] = jnp.zeros_like(zbuf)
        def zfill(e, n):
            first = gdst_ref[e * LANES]
            last = gdst_ref[e * LANES + LANES - 1] + size_ref[e * LANES + LANES - 1]
            pad = pl.multiple_of((last - first + ROW_BLOCK - 1) // ROW_BLOCK * ROW_BLOCK - (last - first), SUBLANES)
            @pl.when(pad > 0)
            def _():
                pltpu.make_async_copy(zbuf.at[pl.ds(0, pad)],
                                      rows_ref.at[pl.ds(pl.multiple_of(last, SUBLANES), pad)], zsem).start()
            return n + pad
        n_zero = lax.fori_loop(0, N_EXPERTS, zfill, jnp.int32(0))
        def ztail(b, _):
            pltpu.make_async_copy(
                zbuf, rows_ref.at[pl.ds(pl.multiple_of(b * ROW_BLOCK, ROW_BLOCK), ROW_BLOCK)], zsem).start()
            return 0
        lax.fori_loop(used_ref[0], n_blocks, ztail, 0)
        zeroed[0] = n_zero + (n_blocks - used_ref[0]) * ROW_BLOCK

    @pl.when(t >= 2)
    def _():
        _wait_rows(rows_ref, inflight[slot], sems.at[slot], SUBLANES)

    pos = pos_ref[...]
    r = lax.broadcasted_iota(jnp.int32, (STAGE_ROWS, pos.shape[1]), 0)
    hit = jnp.logical_or(jnp.logical_or(r == pos[0:1], r == pos[1:2]),
                         jnp.logical_or(r == pos[2:3], r == pos[3:4]))
    perm = jnp.where(hit, 1.0, 0.0).astype(jnp.bfloat16)
    stage[slot] = jnp.dot(perm, h2_ref[...], preferred_element_type=jnp.float32)

    def make_copy(off, dst, n, sem):
        return pltpu.make_async_copy(stage.at[slot, pl.ds(off, n)], rows_ref.at[pl.ds(dst, n)], sem)
    inflight[slot] = _start_group_copies(t, size_ref, goff_ref, gdst_ref, sems.at[slot], make_copy, SUBLANES)

    @pl.when(t == n_tiles - 1)
    def _():
        _wait_rows(rows_ref, inflight[1 - slot], sems.at[1 - slot], SUBLANES)
        _wait_rows(rows_ref, inflight[slot], sems.at[slot], SUBLANES)
        _wait_rows(rows_ref, zeroed[0], zsem, SUBLANES)


def _dispatch_call(size, goff, gdst, used, h2, pos, n_rows):
    n_tok = h2.shape[0]
    ts = ROUTE_TILE
    n_blocks = n_rows // ROW_BLOCK
    grid_spec = pltpu.PrefetchScalarGridSpec(
        num_scalar_prefetch=4,
        grid=(n_tok // ts,),
        in_specs=[
            pl.BlockSpec((ts, D_MODEL), lambda t, *_: (t, 0)),
            pl.BlockSpec((TOP_K, ts), lambda t, *_: (0, t)),
        ],
        out_specs=pl.BlockSpec(memory_space=pl.ANY),
        scratch_shapes=[
            pltpu.VMEM((2, STAGE_ROWS, D_MODEL), jnp.float32),
            pltpu.VMEM((ROW_BLOCK, D_MODEL), jnp.float32),
            pltpu.SMEM((2,), jnp.int32),
            pltpu.SMEM((1,), jnp.int32),
            pltpu.SemaphoreType.DMA((2,)),
            pltpu.SemaphoreType.DMA(()),
        ])
    return pl.pallas_call(
        functools.partial(_dispatch_kernel, n_blocks),
        grid_spec=grid_spec,
        out_shape=jax.ShapeDtypeStruct((n_rows, D_MODEL), jnp.float32),
        name="dispatch_rows",
        compiler_params=pltpu.CompilerParams(dimension_semantics=("arbitrary",),
                                             vmem_limit_bytes=VMEM_LIMIT_BYTES),
    )(size, goff, gdst, used, h2, pos)


def _expert_kernel(blk_ref, used_ref, valid_ref, x_ref, w1_hbm, b1_ref, w2_hbm, b2_ref, y_ref,
                   w1_f32, w2_f32, w1_bf, w2_bf, state, sems):
    i = pl.program_id(0)
    bf16 = jnp.bfloat16
    e = blk_ref[i]
    used = used_ref[0]

    def weight_copies(expert, slot):
        return (pltpu.make_async_copy(w1_hbm.at[expert], w1_f32.at[slot], sems.at[0, slot]),
                pltpu.make_async_copy(w2_hbm.at[expert], w2_f32.at[slot], sems.at[1, slot]))

    @pl.when(i == 0)
    def _():
        state[0] = jnp.int32(-1)
        state[1] = jnp.int32(0)
        @pl.when(used > 0)
        def _():
            for cp in weight_copies(e, 0):
                cp.start()

    @pl.when(jnp.logical_and(i < used, e != state[0]))
    def _():
        slot = state[1]
        for cp in weight_copies(e, slot):
            cp.wait()
        w1_bf[...] = w1_f32[slot].astype(bf16)
        w2_bf[...] = w2_f32[slot].astype(bf16)
        nxt = lax.while_loop(lambda j: jnp.logical_and(j < used, blk_ref[jnp.minimum(j, used - 1)] == e),
                             lambda j: j + 1, i + 1)
        @pl.when(nxt < used)
        def _():
            for cp in weight_copies(blk_ref[nxt], 1 - slot):
                cp.start()
        state[0] = e
        state[1] = 1 - slot

    @pl.when(i >= used)
    def _():
        y_ref[...] = jnp.zeros_like(y_ref)

    def ffn(rows):
        hu = jnp.dot(x_ref[rows, :].astype(bf16), w1_bf[...], preferred_element_type=jnp.float32) + b1_ref[0]
        glu = jnp.minimum(hu[:, :D_FF], SWIGLU_LIMIT)
        lin = jnp.clip(hu[:, D_FF:], -SWIGLU_LIMIT, SWIGLU_LIMIT)
        act = glu * (1.0 / (1.0 + jnp.exp(-SWIGLU_ALPHA * glu))) * (lin + 1.0)
        y = jnp.dot(act.astype(bf16), w2_bf[...], preferred_element_type=jnp.float32) + b2_ref[0]
        y_ref[rows, :] = y.astype(bf16)

    half = ROW_BLOCK // 2
    valid = valid_ref[i]

    @pl.when(jnp.logical_and(i < used, valid > half))
    def _():
        ffn(slice(0, ROW_BLOCK))

    @pl.when(jnp.logical_and(i < used, valid <= half))
    def _():
        ffn(slice(0, half))
        y_ref[half:, :] = jnp.zeros((ROW_BLOCK - half, D_MODEL), bf16)


def _expert_call(blk, used, valid, rows, w1, b1, w2, b2):
    n_rows = rows.shape[0]
    n_blocks = n_rows // ROW_BLOCK
    last = lambda i, u: jnp.maximum(jnp.minimum(i, u[0] - 1), 0)
    grid_spec = pltpu.PrefetchScalarGridSpec(
        num_scalar_prefetch=3,
        grid=(n_blocks,),
        in_specs=[
            pl.BlockSpec((ROW_BLOCK, D_MODEL), lambda i, b, u, v: (last(i, u), 0)),
            pl.BlockSpec(memory_space=pl.ANY),
            pl.BlockSpec((1, 1, 2 * D_FF), lambda i, b, u, v: (b[i], 0, 0)),
            pl.BlockSpec(memory_space=pl.ANY),
            pl.BlockSpec((1, 1, D_MODEL), lambda i, b, u, v: (b[i], 0, 0)),
        ],
        out_specs=pl.BlockSpec((ROW_BLOCK, D_MODEL), lambda i, b, u, v: (i, 0)),
        scratch_shapes=[
            pltpu.VMEM((2, D_MODEL, 2 * D_FF), jnp.float32),
            pltpu.VMEM((2, D_FF, D_MODEL), jnp.float32),
            pltpu.VMEM((D_MODEL, 2 * D_FF), jnp.bfloat16),
            pltpu.VMEM((D_FF, D_MODEL), jnp.bfloat16),
            pltpu.SMEM((2,), jnp.int32),
            pltpu.SemaphoreType.DMA((2, 2)),
        ])
    return pl.pallas_call(
        _expert_kernel,
        grid_spec=grid_spec,
        out_shape=jax.ShapeDtypeStruct((n_rows, D_MODEL), jnp.bfloat16),
        name="expert_ffn",
        compiler_params=pltpu.CompilerParams(dimension_semantics=("arbitrary",),
                                             vmem_limit_bytes=VMEM_LIMIT_BYTES),
    )(blk, used, valid, rows, w1, b1.reshape(N_EXPERTS, 1, 2 * D_FF), w2, b2.reshape(N_EXPERTS, 1, D_MODEL))


def _combine_kernel(span_ref, soff_ref, src_ref, goff_ref, delta_ref,
                    x1_ref, pos_ref, gate_ref, y_ref, gfin_ref, out_ref,
                    stage, inflight, sems):
    t = pl.program_id(0)
    n_tiles = pl.num_programs(0)
    slot = t % 2

    def fetch(tile, slot_):
        def make_copy(off, src, n, sem):
            return pltpu.make_async_copy(y_ref.at[pl.ds(src, n)], stage.at[slot_, pl.ds(off, n)], sem)
        inflight[slot_] = _start_group_copies(tile, span_ref, soff_ref, src_ref, sems.at[slot_], make_copy,
                                              PACK_ROWS)

    @pl.when(t == 0)
    def _():
        stage[...] = jnp.zeros_like(stage)
        fetch(0, 0)

    @pl.when(t + 1 < n_tiles)
    def _():
        fetch(t + 1, 1 - slot)

    _wait_rows(y_ref, inflight[slot], sems.at[slot], PACK_ROWS)

    pos = pos_ref[...]
    shift = jnp.zeros_like(pos)
    prev = jnp.int32(0)
    for e in range(N_EXPERTS):
        d = delta_ref[e * LANES + t]
        shift = shift + jnp.where(pos >= goff_ref[e * LANES + t], d - prev, 0)
        prev = d
    pos = pos + shift
    gate = gate_ref[...]
    r = lax.broadcasted_iota(jnp.int32, (COMBINE_ROWS, pos.shape[1]), 0)
    w = jnp.zeros(r.shape, jnp.float32)
    for k in range(TOP_K):
        w = jnp.where(r == pos[k:k + 1], gate[k:k + 1], w)
    moe = lax.dot_general(w.astype(jnp.bfloat16), stage[slot],
                          (((0,), (0,)), ((), ())), preferred_element_type=jnp.float32)
    out_ref[...] = _rms(x1_ref[...] + moe, gfin_ref[...])


def _combine_call(span, soff, src, goff, delta, x1, pos, gate, y_rows, g_final):
    n_tok = x1.shape[0]
    ts = ROUTE_TILE
    grid_spec = pltpu.PrefetchScalarGridSpec(
        num_scalar_prefetch=5,
        grid=(n_tok // ts,),
        in_specs=[
            pl.BlockSpec((ts, D_MODEL), lambda t, *_: (t, 0)),
            pl.BlockSpec((TOP_K, ts), lambda t, *_: (0, t)),
            pl.BlockSpec((TOP_K, ts), lambda t, *_: (0, t)),
            pl.BlockSpec(memory_space=pl.ANY),
            pl.BlockSpec((1, D_MODEL), lambda t, *_: (0, 0)),
        ],
        out_specs=pl.BlockSpec((ts, D_MODEL), lambda t, *_: (t, 0)),
        scratch_shapes=[
            pltpu.VMEM((2, COMBINE_ROWS, D_MODEL), jnp.bfloat16),
            pltpu.SMEM((2,), jnp.int32),
            pltpu.SemaphoreType.DMA((2,)),
        ])
    return pl.pallas_call(
        _combine_kernel,
        grid_spec=grid_spec,
        out_shape=jax.ShapeDtypeStruct((n_tok, D_MODEL), jnp.float32),
        name="combine_norm",
        compiler_params=pltpu.CompilerParams(dimension_semantics=("arbitrary",),
                                             vmem_limit_bytes=VMEM_LIMIT_BYTES),
    )(span, soff, src, goff, delta, x1, pos, gate, y_rows, g_final)


def _attention_bias():
    slopes = jnp.exp2(-8.0 * jnp.arange(1, N_Q_HEADS + 1, dtype=jnp.float32) / N_Q_HEADS)
    qi = jnp.arange(BLOCK)[:, None]
    kj = jnp.arange(BAND)[None, :]
    dist = qi + BLOCK - kj
    valid = (dist >= 0) & (dist < WINDOW)
    b = -slopes[:, None, None] * dist.astype(jnp.float32)[None]
    b = jnp.where(valid[None], b, -jnp.inf)
    b = b.reshape(N_KV_HEADS, GROUP, BLOCK, BAND).transpose(0, 3, 1, 2)
    return b.reshape(N_KV_HEADS, BAND, GROUP * BLOCK)


def kernel(x, g_mix, w_in, conv_w, sinks, g_attn_out, g_conv_out, w_out, g_ffn, w_router, b_router,
           w1, b1, w2, b2, g_final):
    bsz, s_len, d = x.shape
    n_tok = bsz * s_len
    n_tiles = n_tok // ROUTE_TILE
    worst = n_tok * TOP_K + n_tiles * N_EXPERTS * (SUBLANES - 1) + N_EXPERTS * (ROW_BLOCK - SUBLANES)
    n_blocks = -(-worst // ROW_BLOCK)
    n_rows = n_blocks * ROW_BLOCK
    x2d = x.reshape(n_tok, d)
    bias = _attention_bias()
    l = 0
    wrt = w_router[l].T
    wr_hi = wrt.astype(jnp.bfloat16)
    wr_lo = (wrt - wr_hi.astype(jnp.float32)).astype(jnp.bfloat16)
    wr_split = jnp.concatenate([wr_hi, wr_lo], axis=0)
    x1, h2, pos, gate, cnt = _mixer_call(
        x2d, sinks[l], g_mix[l].reshape(1, d), w_in[l].astype(jnp.bfloat16), conv_w[l], bias,
        g_attn_out[l].reshape(1, ATTN_WIDTH), g_conv_out[l].reshape(1, CONV_WIDTH),
        w_out[l].astype(jnp.bfloat16), g_ffn[l].reshape(1, d), wr_split,
        b_router[l].reshape(N_EXPERTS, 1), s_len)
    size, goff, gdst, blk, used, span, soff, src, delta = _tables_call(cnt, n_blocks)
    size, goff, gdst = size.reshape(-1), goff.reshape(-1), gdst.reshape(-1)
    span, soff, src, delta = span.reshape(-1), soff.reshape(-1), src.reshape(-1), delta.reshape(-1)
    blk, valid, used = blk[0], blk[1], used.reshape(-1)[:1]
    rows = _dispatch_call(size, goff, gdst, used, h2, pos, n_rows)
    y_rows = _expert_call(blk, used, valid, rows, w1[l], b1[l], w2[l], b2[l])
    out = _combine_call(span, soff, src, goff, delta, x1, pos, gate, y_rows, g_final.reshape(1, d))
    return out.reshape(bsz, s_len, d)
```

```python
import functools
import math

import jax
import jax.numpy as jnp
from jax import lax
from jax.experimental import pallas as pl
from jax.experimental.pallas import tpu as pltpu

D_MODEL = 1024
N_Q_HEADS = 8
N_KV_HEADS = 2
HEAD_DIM = 64
GROUP = N_Q_HEADS // N_KV_HEADS
ATTN_WIDTH = N_Q_HEADS * HEAD_DIM
KV_WIDTH = N_KV_HEADS * HEAD_DIM
CONV_WIDTH = D_MODEL - ATTN_WIDTH
QKV_COLS = ATTN_WIDTH + 2 * KV_WIDTH
IN_COLS = QKV_COLS + 3 * CONV_WIDTH
WINDOW = 128
BLOCK = 128
BAND = 2 * BLOCK
N_EXPERTS = 32
TOP_K = 4
D_FF = D_MODEL
SWIGLU_ALPHA = 1.702
SWIGLU_LIMIT = 7.0
EPS = 1e-5

LANES = 128
SUBLANES = 8
VMEM_LIMIT_BYTES = 56 * 1024 * 1024

TOKEN_TILE = 1024
SUB_TILE = 512
ATTN_AHEAD = 1
ROUTE_TILE = 256
ROW_BLOCK = 512
STAGE_ROWS = -(-(TOP_K * ROUTE_TILE + N_EXPERTS * (SUBLANES - 1)) // LANES) * LANES
PACK_ROWS = 16
COMBINE_ROWS = -(-(TOP_K * ROUTE_TILE + N_EXPERTS * (SUBLANES - 1 + PACK_ROWS)) // LANES) * LANES


def _rms(x, g):
    return x * lax.rsqrt(jnp.mean(x * x, axis=-1, keepdims=True) + EPS) * g


def _excl_cumsum(a, axis):
    idx = lax.broadcasted_iota(jnp.int32, a.shape, axis)
    inc = a
    step = 1
    while step < a.shape[axis]:
        inc = inc + jnp.where(idx >= step, pltpu.roll(inc, step, axis), 0)
        step *= 2
    return inc - a


def _mixer_kernel(tiles_per_seq,
                  sink_ref,
                  x_ref, gmix_ref, win_ref, convw_ref, bias_ref, gattn_ref, gconv_ref, wout_ref,
                  gffn_ref, wr_ref, br_ref,
                  x1_ref, h2_ref, pos_ref, gate_ref, cnt_ref,
                  p_scr, h_scr, mix_scr, k_scr, vt_scr, u_scr, attn_t_scr, tri_scr):
    tm = x_ref.shape[0]
    sub_tm = SUB_TILE
    ts = ROUTE_TILE
    i = pl.program_id(0)
    first = (i % tiles_per_seq) == 0
    f32, bf16 = jnp.float32, jnp.bfloat16

    @pl.when(i == 0)
    def _():
        cnt_ref[...] = jnp.zeros_like(cnt_ref)
        r = lax.broadcasted_iota(jnp.int32, (ts, ts), 0)
        c = lax.broadcasted_iota(jnp.int32, (ts, ts), 1)
        tri_scr[...] = jnp.where(r < c, 1.0, 0.0).astype(bf16)

    @pl.when(first)
    def _():
        k_scr[0:BLOCK, :] = jnp.zeros((BLOCK, k_scr.shape[1]), bf16)
        vt_scr[:, 0:BLOCK] = jnp.zeros((KV_WIDTH, BLOCK), bf16)
        u_scr[0:SUBLANES, :] = jnp.zeros((SUBLANES, CONV_WIDTH), f32)

    n_sub = tm // sub_tm
    in_chunks = [(c, c + 2 * LANES) for c in range(0, IN_COLS, 2 * LANES)]
    out_chunks = [(c, c + 2 * LANES) for c in range(0, D_MODEL, 2 * LANES)]
    lane = lax.broadcasted_iota(jnp.int32, (BAND, LANES), 1)
    lo = lane < HEAD_DIM
    qgrp = lax.broadcasted_iota(jnp.int32, (BLOCK, 2 * LANES), 1) // HEAD_DIM
    key_row = lax.broadcasted_iota(jnp.int32, (BAND, GROUP * BLOCK), 0)
    scale = 1.0 / math.sqrt(HEAD_DIM)
    zero_q = jnp.zeros((BLOCK, 2 * LANES), bf16)
    nt = (((1,), (1,)), ((), ()))

    def rows_of(sub):
        return slice(sub * sub_tm, (sub + 1) * sub_tm)

    def prep(sub):
        r0 = rows_of(sub)
        h_scr[r0, :] = _rms(x_ref[r0, :], gmix_ref[...]).astype(bf16)

    def in_proj(sub, c0, c1):
        r0 = rows_of(sub)
        p_scr[r0, c0:c1] = jnp.dot(h_scr[r0, :], win_ref[:, c0:c1], preferred_element_type=f32)

    def kv_stage(sub):
        r0 = rows_of(sub)
        base = sub * sub_tm
        kk = p_scr[r0, ATTN_WIDTH:ATTN_WIDTH + KV_WIDTH]
        kr = pltpu.roll(kk, HEAD_DIM, axis=1)
        k_scr[BLOCK + base:BLOCK + base + sub_tm, :] = jnp.concatenate([kk, kr], axis=1).astype(bf16)
        vt_scr[:, BLOCK + base:BLOCK + base + sub_tm] = jnp.transpose(
            p_scr[r0, ATTN_WIDTH + KV_WIDTH:QKV_COLS]).astype(bf16)

    def attn_scores(j, hh):
        rows = slice(j * BLOCK, (j + 1) * BLOCK)
        kband = k_scr[j * BLOCK:j * BLOCK + BAND, :]
        bk, bkr = kband[:, 0:LANES], kband[:, LANES:2 * LANES]
        k_both = jnp.where(lo, bk, bkr) if hh == 0 else jnp.where(lo, bkr, bk)
        kmat = jnp.concatenate([k_both, k_both], axis=1)
        qh = (p_scr[rows, hh * 2 * LANES:(hh + 1) * 2 * LANES] * scale).astype(bf16)
        qm = jnp.concatenate([jnp.where(qgrp == g, qh, zero_q) for g in range(GROUP)], axis=0)
        s = lax.dot_general(kmat, qm, nt, preferred_element_type=f32) + bias_ref[hh]
        if j == 0:
            s = jnp.where(jnp.logical_and(first, key_row < BLOCK), -jnp.inf, s)
        return s

    def attn_finish(j, hh, s):
        rows = slice(j * BLOCK, (j + 1) * BLOCK)
        vt_band = vt_scr[:, j * BLOCK:j * BLOCK + BAND]
        sink = jnp.concatenate(
            [jnp.full((1, BLOCK), sink_ref[hh * GROUP + g], f32) for g in range(GROUP)], axis=1)
        m = jnp.maximum(jnp.max(s, axis=0, keepdims=True), sink)
        e = jnp.exp(s - m)
        inv = 1.0 / (jnp.sum(e, axis=0, keepdims=True) + jnp.exp(sink - m))
        o_t = jnp.dot(vt_band, e.astype(bf16), preferred_element_type=f32)
        o_t = o_t[hh * HEAD_DIM:(hh + 1) * HEAD_DIM, :] * inv
        for g in range(GROUP):
            head = hh * GROUP + g
            attn_t_scr[head * HEAD_DIM:(head + 1) * HEAD_DIM, rows] = o_t[:, g * BLOCK:(g + 1) * BLOCK]

    def conv_mix(sub):
        r0 = rows_of(sub)
        base = sub * sub_tm
        bg = p_scr[r0, QKV_COLS:QKV_COLS + CONV_WIDTH]
        cg = p_scr[r0, QKV_COLS + CONV_WIDTH:QKV_COLS + 2 * CONV_WIDTH]
        xin = p_scr[r0, QKV_COLS + 2 * CONV_WIDTH:IN_COLS]
        u = cg * xin
        u_scr[SUBLANES + base:SUBLANES + base + sub_tm, :] = u
        u1 = u_scr[SUBLANES - 1 + base:SUBLANES - 1 + base + sub_tm, :]
        u2 = u_scr[SUBLANES - 2 + base:SUBLANES - 2 + base + sub_tm, :]
        y = convw_ref[0:1, :] * u2 + convw_ref[1:2, :] * u1 + convw_ref[2:3, :] * u
        conv = bg * y
        mix_scr[r0, :] = jnp.concatenate(
            [_rms(jnp.transpose(attn_t_scr[:, r0]), gattn_ref[...]), _rms(conv, gconv_ref[...])],
            axis=1).astype(bf16)

    def out_proj(sub, c0, c1):
        r0 = rows_of(sub)
        x1_ref[r0, c0:c1] = x_ref[r0, c0:c1] + jnp.dot(mix_scr[r0, :], wout_ref[:, c0:c1],
                                                      preferred_element_type=f32)

    def route(sub):
        r0 = rows_of(sub)
        h2 = _rms(x1_ref[r0, :], gffn_ref[...])
        h2_hi = h2.astype(bf16)
        h2_ref[r0, :] = h2_hi
        h2_lo = (h2 - h2_hi.astype(f32)).astype(bf16)
        by_hi = lax.dot_general(wr_ref[...], h2_hi, nt, preferred_element_type=f32)
        by_lo = lax.dot_general(wr_ref[0:N_EXPERTS, :], h2_lo, nt, preferred_element_type=f32)
        logits = by_hi[0:N_EXPERTS] + by_hi[N_EXPERTS:] + by_lo + br_ref[...]
        row = lax.broadcasted_iota(jnp.int32, (N_EXPERTS, sub_tm), 0)
        vals, idxs = [], []
        l = logits
        for _ in range(TOP_K):
            m = jnp.max(l, axis=0, keepdims=True)
            idx = jnp.min(jnp.where(l == m, row, N_EXPERTS), axis=0, keepdims=True)
            vals.append(m)
            idxs.append(idx)
            l = jnp.where(row == idx, -jnp.inf, l)
        es = [jnp.exp(v - vals[0]) for v in vals]
        inv = 1.0 / (es[0] + es[1] + es[2] + es[3])
        gate_ref[:, r0] = jnp.concatenate([e * inv for e in es], axis=0)

        onehots = [row == idx for idx in idxs]
        sel = jnp.logical_or(jnp.logical_or(onehots[0], onehots[1]), jnp.logical_or(onehots[2], onehots[3]))
        oh = jnp.where(sel, 1.0, 0.0)
        cnt_lane = lax.broadcasted_iota(jnp.int32, cnt_ref.shape, 1)
        poss = []
        for rt in range(sub_tm // ts):
            sl = slice(rt * ts, (rt + 1) * ts)
            oh_s = oh[:, sl]
            cum = jnp.dot(oh_s.astype(bf16), tri_scr[...], preferred_element_type=f32)
            cnt = jnp.sum(oh_s, axis=1, keepdims=True)
            padded = SUBLANES * jnp.floor((cnt + (SUBLANES - 1.0)) * (1.0 / SUBLANES))
            goff = _excl_cumsum(jnp.broadcast_to(padded, (N_EXPERTS, LANES)), 0)[:, 0:1]
            pos = goff + cum
            poss.append(jnp.concatenate(
                [jnp.sum(jnp.where(o[:, sl], pos, 0.0), axis=0, keepdims=True) for o in onehots], axis=0))
            tile_id = i * (tm // ts) + sub * (sub_tm // ts) + rt
            cnt_ref[...] = jnp.where(cnt_lane == tile_id, cnt.astype(jnp.int32), cnt_ref[...])
        pos_ref[:, r0] = jnp.concatenate(poss, axis=1).astype(jnp.int32)

    def interleave(main, fillers):
        after = [[] for _ in main]
        for n, filler in enumerate(fillers):
            after[n * len(main) // len(fillers)].append(filler)
        for stage, extra in zip(main, after):
            stage()
            for filler in extra:
                filler()

    def attention_stages(sub):
        blocks = range(sub * (sub_tm // BLOCK), (sub + 1) * (sub_tm // BLOCK))
        its = [(j, hh) for j in blocks for hh in range(N_KV_HEADS)]
        scores = {}

        def score(n):
            scores[n] = attn_scores(*its[n])

        def finish(n):
            attn_finish(*its[n], scores.pop(n))

        stages = [functools.partial(score, n) for n in range(min(ATTN_AHEAD, len(its)))]
        for n in range(len(its)):
            if n + ATTN_AHEAD < len(its):
                stages.append(functools.partial(score, n + ATTN_AHEAD))
            stages.append(functools.partial(finish, n))
        return stages

    prep(0)
    for c0, c1 in in_chunks:
        in_proj(0, c0, c1)
    kv_stage(0)
    for sub in range(n_sub):
        fillers = []
        if sub > 0:
            fillers.append(functools.partial(conv_mix, sub - 1))
            fillers += [functools.partial(out_proj, sub - 1, c0, c1) for c0, c1 in out_chunks]
        if sub + 1 < n_sub:
            prep(sub + 1)
            fillers += [functools.partial(in_proj, sub + 1, c0, c1) for c0, c1 in in_chunks]
        interleave(attention_stages(sub), fillers)
        if sub + 1 < n_sub:
            kv_stage(sub + 1)
        if 0 < sub < n_sub - 1:
            route(sub - 1)
    last = n_sub - 1
    conv_mix(last)
    out_proj(last, *out_chunks[0])
    if last > 0:
        route(last - 1)
    for c0, c1 in out_chunks[1:]:
        out_proj(last, c0, c1)
    route(last)

    k_scr[0:BLOCK, :] = k_scr[tm:tm + BLOCK, :]
    vt_scr[:, 0:BLOCK] = vt_scr[:, tm:tm + BLOCK]
    u_scr[0:SUBLANES, :] = u_scr[tm:tm + SUBLANES, :]


def _mixer_call(x2d, sinks, g_mix, w_in, conv_w, bias, g_attn, g_conv, w_out, g_ffn, wr_split, br, seq_len):
    n_tok = x2d.shape[0]
    tm = TOKEN_TILE
    n_tiles = n_tok // tm
    assert n_tok // ROUTE_TILE == LANES, "one lane of the count table per routing tile"
    const = lambda shape: pl.BlockSpec(shape, lambda i, *_: (0,) * len(shape))
    grid_spec = pltpu.PrefetchScalarGridSpec(
        num_scalar_prefetch=0,
        grid=(n_tiles,),
        in_specs=[
            pl.BlockSpec(memory_space=pltpu.SMEM),
            pl.BlockSpec((tm, D_MODEL), lambda i: (i, 0)),
            const((1, D_MODEL)), const((D_MODEL, IN_COLS)), const((3, CONV_WIDTH)),
            const((N_KV_HEADS, BAND, GROUP * BLOCK)),
            const((1, ATTN_WIDTH)), const((1, CONV_WIDTH)), const((D_MODEL, D_MODEL)),
            const((1, D_MODEL)), const((2 * N_EXPERTS, D_MODEL)), const((N_EXPERTS, 1)),
        ],
        out_specs=[
            pl.BlockSpec((tm, D_MODEL), lambda i: (i, 0)),
            pl.BlockSpec((tm, D_MODEL), lambda i: (i, 0)),
            pl.BlockSpec((TOP_K, tm), lambda i: (0, i)),
            pl.BlockSpec((TOP_K, tm), lambda i: (0, i)),
            pl.BlockSpec((N_EXPERTS, LANES), lambda i: (0, 0)),
        ],
        scratch_shapes=[
            pltpu.VMEM((tm, IN_COLS), jnp.float32),
            pltpu.VMEM((tm, D_MODEL), jnp.bfloat16),
            pltpu.VMEM((tm, D_MODEL), jnp.bfloat16),
            pltpu.VMEM((BLOCK + tm, 2 * LANES), jnp.bfloat16),
            pltpu.VMEM((KV_WIDTH, BLOCK + tm), jnp.bfloat16),
            pltpu.VMEM((SUBLANES + tm, CONV_WIDTH), jnp.float32),
            pltpu.VMEM((ATTN_WIDTH, tm), jnp.float32),
            pltpu.VMEM((ROUTE_TILE, ROUTE_TILE), jnp.bfloat16),
        ])
    out_shape = [
        jax.ShapeDtypeStruct((n_tok, D_MODEL), jnp.float32),
        jax.ShapeDtypeStruct((n_tok, D_MODEL), jnp.bfloat16),
        jax.ShapeDtypeStruct((TOP_K, n_tok), jnp.int32),
        jax.ShapeDtypeStruct((TOP_K, n_tok), jnp.float32),
        jax.ShapeDtypeStruct((N_EXPERTS, LANES), jnp.int32),
    ]
    return pl.pallas_call(
        functools.partial(_mixer_kernel, seq_len // tm),
        grid_spec=grid_spec, out_shape=out_shape, name="mixer_router",
        compiler_params=pltpu.CompilerParams(dimension_semantics=("arbitrary",),
                                             vmem_limit_bytes=VMEM_LIMIT_BYTES),
    )(sinks, x2d, g_mix, w_in, conv_w, bias, g_attn, g_conv, w_out, g_ffn, wr_split, br)


def _tables_kernel(cnt_ref, size_ref, goff_ref, gdst_ref, blk_ref, used_ref,
                   span_ref, soff_ref, src_ref, delta_ref):
    size = ((cnt_ref[...] + (SUBLANES - 1)) // SUBLANES) * SUBLANES
    size_ref[...] = size
    goff = _excl_cumsum(size, 0)
    goff_ref[...] = goff
    total = jnp.broadcast_to(jnp.sum(size, axis=1, keepdims=True), size.shape)
    region = ((total + (ROW_BLOCK - 1)) // ROW_BLOCK) * ROW_BLOCK
    pstart = _excl_cumsum(region, 0)
    gdst = pstart + _excl_cumsum(size, 1)
    gdst_ref[...] = gdst
    src = (gdst // PACK_ROWS) * PACK_ROWS
    span = jnp.where(size > 0, ((gdst + size + (PACK_ROWS - 1)) // PACK_ROWS) * PACK_ROWS - src, 0)
    soff = _excl_cumsum(span, 0)
    span_ref[...] = span
    soff_ref[...] = soff
    src_ref[...] = src
    delta_ref[...] = soff + (gdst - src) - goff
    pend = (pstart + region)[:, 0:1]
    start = lax.broadcasted_iota(jnp.int32, (N_EXPERTS, blk_ref.shape[1]), 1) * ROW_BLOCK
    owner = jnp.sum(jnp.where(pend <= start, 1, 0), axis=0, keepdims=True)
    blk_ref[0:1, :] = jnp.minimum(owner, N_EXPERTS - 1)
    first_row, last_row = pstart[:, 0:1], (pstart + total)[:, 0:1]
    inside = jnp.logical_and(first_row <= start, start < pend)
    blk_ref[1:2, :] = jnp.sum(jnp.where(inside, jnp.minimum(last_row - start, ROW_BLOCK), 0),
                              axis=0, keepdims=True)
    used_ref[...] = jnp.broadcast_to(jnp.max(pend, axis=0, keepdims=True) // ROW_BLOCK, used_ref.shape)


def _tables_call(cnt, n_blocks):
    nb_pad = -(-n_blocks // LANES) * LANES
    tbl = jax.ShapeDtypeStruct((N_EXPERTS, LANES), jnp.int32)
    return pl.pallas_call(
        _tables_kernel,
        out_shape=[tbl, tbl, tbl, jax.ShapeDtypeStruct((2, nb_pad), jnp.int32),
                   jax.ShapeDtypeStruct((1, LANES), jnp.int32), tbl, tbl, tbl, tbl],
        name="group_tables",
    )(cnt)


def _start_group_copies(tile, size_ref, goff_ref, gdst_ref, sem, make_copy, align):
    rows = jnp.int32(0)
    for e in range(N_EXPERTS):
        n = pl.multiple_of(size_ref[e * LANES + tile], align)
        @pl.when(n > 0)
        def _():
            off = pl.multiple_of(goff_ref[e * LANES + tile], align)
            dst = pl.multiple_of(gdst_ref[e * LANES + tile], align)
            make_copy(off, dst, n, sem).start(priority=e % 2)
        rows = rows + n
    return rows


def _wait_rows(rows_hbm, n, sem, align):
    @pl.when(n > 0)
    def _():
        m = pl.multiple_of(n, align)
        half = rows_hbm.shape[0] // (2 * align) * align
        pltpu.make_async_copy(rows_hbm.at[pl.ds(0, m)], rows_hbm.at[pl.ds(half, m)], sem).wait()


def _dispatch_kernel(n_blocks,
                     size_ref, goff_ref, gdst_ref, used_ref,
                     h2_ref, pos_ref, rows_ref,
                     stage, zbuf, inflight, zeroed, sems, zsem):
    t = pl.program_id(0)
    n_tiles = pl.num_programs(0)
    slot = t % 2

    @pl.when(t == 0)
    def _():
        zbuf[...] = jnp.zeros_like(zbuf)
        def zfill(e, n):
            first = gdst_ref[e * LANES]
            last = gdst_ref[e * LANES + LANES - 1] + size_ref[e * LANES + LANES - 1]
            pad = pl.multiple_of((last - first + ROW_BLOCK - 1) // ROW_BLOCK * ROW_BLOCK - (last - first), SUBLANES)
            @pl.when(pad > 0)
            def _():
                pltpu.make_async_copy(zbuf.at[pl.ds(0, pad)],
                                      rows_ref.at[pl.ds(pl.multiple_of(last, SUBLANES), pad)], zsem).start()
            return n + pad
        n_zero = lax.fori_loop(0, N_EXPERTS, zfill, jnp.int32(0))
        def ztail(b, _):
            pltpu.make_async_copy(
                zbuf, rows_ref.at[pl.ds(pl.multiple_of(b * ROW_BLOCK, ROW_BLOCK), ROW_BLOCK)], zsem).start()
            return 0
        lax.fori_loop(used_ref[0], n_blocks, ztail, 0)
        zeroed[0] = n_zero + (n_blocks - used_ref[0]) * ROW_BLOCK

    @pl.when(t >= 2)
    def _():
        _wait_rows(rows_ref, inflight[slot], sems.at[slot], SUBLANES)

    pos = pos_ref[...]
    r = lax.broadcasted_iota(jnp.int32, (STAGE_ROWS, pos.shape[1]), 0)
    hit = jnp.logical_or(jnp.logical_or(r == pos[0:1], r == pos[1:2]),
                         jnp.logical_or(r == pos[2:3], r == pos[3:4]))
    perm = jnp.where(hit, 1.0, 0.0).astype(jnp.bfloat16)
    stage[slot] = jnp.dot(perm, h2_ref[...], preferred_element_type=jnp.float32)

    def make_copy(off, dst, n, sem):
        return pltpu.make_async_copy(stage.at[slot, pl.ds(off, n)], rows_ref.at[pl.ds(dst, n)], sem)
    inflight[slot] = _start_group_copies(t, size_ref, goff_ref, gdst_ref, sems.at[slot], make_copy, SUBLANES)

    @pl.when(t == n_tiles - 1)
    def _():
        _wait_rows(rows_ref, inflight[1 - slot], sems.at[1 - slot], SUBLANES)
        _wait_rows(rows_ref, inflight[slot], sems.at[slot], SUBLANES)
        _wait_rows(rows_ref, zeroed[0], zsem, SUBLANES)


def _dispatch_call(size, goff, gdst, used, h2, pos, n_rows):
    n_tok = h2.shape[0]
    ts = ROUTE_TILE
    n_blocks = n_rows // ROW_BLOCK
    grid_spec = pltpu.PrefetchScalarGridSpec(
        num_scalar_prefetch=4,
        grid=(n_tok // ts,),
        in_specs=[
            pl.BlockSpec((ts, D_MODEL), lambda t, *_: (t, 0)),
            pl.BlockSpec((TOP_K, ts), lambda t, *_: (0, t)),
        ],
        out_specs=pl.BlockSpec(memory_space=pl.ANY),
        scratch_shapes=[
            pltpu.VMEM((2, STAGE_ROWS, D_MODEL), jnp.float32),
            pltpu.VMEM((ROW_BLOCK, D_MODEL), jnp.float32),
            pltpu.SMEM((2,), jnp.int32),
            pltpu.SMEM((1,), jnp.int32),
            pltpu.SemaphoreType.DMA((2,)),
            pltpu.SemaphoreType.DMA(()),
        ])
    return pl.pallas_call(
        functools.partial(_dispatch_kernel, n_blocks),
        grid_spec=grid_spec,
        out_shape=jax.ShapeDtypeStruct((n_rows, D_MODEL), jnp.float32),
        name="dispatch_rows",
        compiler_params=pltpu.CompilerParams(dimension_semantics=("arbitrary",),
                                             vmem_limit_bytes=VMEM_LIMIT_BYTES),
    )(size, goff, gdst, used, h2, pos)


def _expert_kernel(blk_ref, used_ref, valid_ref, x_ref, w1_hbm, b1_ref, w2_hbm, b2_ref, y_ref,
                   w1_f32, w2_f32, w1_bf, w2_bf, state, sems):
    i = pl.program_id(0)
    bf16 = jnp.bfloat16
    e = blk_ref[i]
    used = used_ref[0]

    def weight_copies(expert, slot):
        return (pltpu.make_async_copy(w1_hbm.at[expert], w1_f32.at[slot], sems.at[0, slot]),
                pltpu.make_async_copy(w2_hbm.at[expert], w2_f32.at[slot], sems.at[1, slot]))

    @pl.when(i == 0)
    def _():
        state[0] = jnp.int32(-1)
        state[1] = jnp.int32(0)
        @pl.when(used > 0)
        def _():
            for cp in weight_copies(e, 0):
                cp.start(priority=1)

    @pl.when(jnp.logical_and(i < used, e != state[0]))
    def _():
        slot = state[1]
        for cp in weight_copies(e, slot):
            cp.wait()
        w1_bf[...] = w1_f32[slot].astype(bf16)
        w2_bf[...] = w2_f32[slot].astype(bf16)
        nxt = lax.while_loop(lambda j: jnp.logical_and(j < used, blk_ref[jnp.minimum(j, used - 1)] == e),
                             lambda j: j + 1, i + 1)
        @pl.when(nxt < used)
        def _():
            for cp in weight_copies(blk_ref[nxt], 1 - slot):
                cp.start(priority=1)
        state[0] = e
        state[1] = 1 - slot

    @pl.when(i >= used)
    def _():
        y_ref[...] = jnp.zeros_like(y_ref)

    def ffn(rows):
        hu = jnp.dot(x_ref[rows, :].astype(bf16), w1_bf[...], preferred_element_type=jnp.float32) + b1_ref[0]
        glu = jnp.minimum(hu[:, :D_FF], SWIGLU_LIMIT)
        lin = jnp.clip(hu[:, D_FF:], -SWIGLU_LIMIT, SWIGLU_LIMIT)
        act = glu * (1.0 / (1.0 + jnp.exp(-SWIGLU_ALPHA * glu))) * (lin + 1.0)
        y = jnp.dot(act.astype(bf16), w2_bf[...], preferred_element_type=jnp.float32) + b2_ref[0]
        y_ref[rows, :] = y.astype(bf16)

    half = ROW_BLOCK // 2
    valid = valid_ref[i]

    @pl.when(jnp.logical_and(i < used, valid > half))
    def _():
        ffn(slice(0, ROW_BLOCK))

    @pl.when(jnp.logical_and(i < used, valid <= half))
    def _():
        ffn(slice(0, half))
        y_ref[half:, :] = jnp.zeros((ROW_BLOCK - half, D_MODEL), bf16)


def _expert_call(blk, used, valid, rows, w1, b1, w2, b2):
    n_rows = rows.shape[0]
    n_blocks = n_rows // ROW_BLOCK
    last = lambda i, u: jnp.maximum(jnp.minimum(i, u[0] - 1), 0)
    grid_spec = pltpu.PrefetchScalarGridSpec(
        num_scalar_prefetch=3,
        grid=(n_blocks,),
        in_specs=[
            pl.BlockSpec((ROW_BLOCK, D_MODEL), lambda i, b, u, v: (last(i, u), 0)),
            pl.BlockSpec(memory_space=pl.ANY),
            pl.BlockSpec((1, 1, 2 * D_FF), lambda i, b, u, v: (b[i], 0, 0)),
            pl.BlockSpec(memory_space=pl.ANY),
            pl.BlockSpec((1, 1, D_MODEL), lambda i, b, u, v: (b[i], 0, 0)),
        ],
        out_specs=pl.BlockSpec((ROW_BLOCK, D_MODEL), lambda i, b, u, v: (i, 0)),
        scratch_shapes=[
            pltpu.VMEM((2, D_MODEL, 2 * D_FF), jnp.float32),
            pltpu.VMEM((2, D_FF, D_MODEL), jnp.float32),
            pltpu.VMEM((D_MODEL, 2 * D_FF), jnp.bfloat16),
            pltpu.VMEM((D_FF, D_MODEL), jnp.bfloat16),
            pltpu.SMEM((2,), jnp.int32),
            pltpu.SemaphoreType.DMA((2, 2)),
        ])
    return pl.pallas_call(
        _expert_kernel,
        grid_spec=grid_spec,
        out_shape=jax.ShapeDtypeStruct((n_rows, D_MODEL), jnp.bfloat16),
        name="expert_ffn",
        compiler_params=pltpu.CompilerParams(dimension_semantics=("arbitrary",),
                                             vmem_limit_bytes=VMEM_LIMIT_BYTES),
    )(blk, used, valid, rows, w1, b1.reshape(N_EXPERTS, 1, 2 * D_FF), w2, b2.reshape(N_EXPERTS, 1, D_MODEL))


def _combine_kernel(span_ref, soff_ref, src_ref, goff_ref, delta_ref,
                    x1_ref, pos_ref, gate_ref, y_ref, gfin_ref, out_ref,
                    stage, inflight, sems):
    t = pl.program_id(0)
    n_tiles = pl.num_programs(0)
    slot = t % 2

    def fetch(tile, slot_):
        def make_copy(off, src, n, sem):
            return pltpu.make_async_copy(y_ref.at[pl.ds(src, n)], stage.at[slot_, pl.ds(off, n)], sem)
        inflight[slot_] = _start_group_copies(tile, span_ref, soff_ref, src_ref, sems.at[slot_], make_copy,
                                              PACK_ROWS)

    @pl.when(t == 0)
    def _():
        stage[...] = jnp.zeros_like(stage)
        fetch(0, 0)

    @pl.when(t + 1 < n_tiles)
    def _():
        fetch(t + 1, 1 - slot)

    _wait_rows(y_ref, inflight[slot], sems.at[slot], PACK_ROWS)

    pos = pos_ref[...]
    shift = jnp.zeros_like(pos)
    prev = jnp.int32(0)
    for e in range(N_EXPERTS):
        d = delta_ref[e * LANES + t]
        shift = shift + jnp.where(pos >= goff_ref[e * LANES + t], d - prev, 0)
        prev = d
    pos = pos + shift
    gate = gate_ref[...]
    r = lax.broadcasted_iota(jnp.int32, (COMBINE_ROWS, pos.shape[1]), 0)
    w = jnp.zeros(r.shape, jnp.float32)
    for k in range(TOP_K):
        w = jnp.where(r == pos[k:k + 1], gate[k:k + 1], w)
    moe = lax.dot_general(w.astype(jnp.bfloat16), stage[slot],
                          (((0,), (0,)), ((), ())), preferred_element_type=jnp.float32)
    out_ref[...] = _rms(x1_ref[...] + moe, gfin_ref[...])


def _combine_call(span, soff, src, goff, delta, x1, pos, gate, y_rows, g_final):
    n_tok = x1.shape[0]
    ts = ROUTE_TILE
    grid_spec = pltpu.PrefetchScalarGridSpec(
        num_scalar_prefetch=5,
        grid=(n_tok // ts,),
        in_specs=[
            pl.BlockSpec((ts, D_MODEL), lambda t, *_: (t, 0)),
            pl.BlockSpec((TOP_K, ts), lambda t, *_: (0, t)),
            pl.BlockSpec((TOP_K, ts), lambda t, *_: (0, t)),
            pl.BlockSpec(memory_space=pl.ANY),
            pl.BlockSpec((1, D_MODEL), lambda t, *_: (0, 0)),
        ],
        out_specs=pl.BlockSpec((ts, D_MODEL), lambda t, *_: (t, 0)),
        scratch_shapes=[
            pltpu.VMEM((2, COMBINE_ROWS, D_MODEL), jnp.bfloat16),
            pltpu.SMEM((2,), jnp.int32),
            pltpu.SemaphoreType.DMA((2,)),
        ])
    return pl.pallas_call(
        _combine_kernel,
        grid_spec=grid_spec,
        out_shape=jax.ShapeDtypeStruct((n_tok, D_MODEL), jnp.float32),
        name="combine_norm",
        compiler_params=pltpu.CompilerParams(dimension_semantics=("arbitrary",),
                                             vmem_limit_bytes=VMEM_LIMIT_BYTES),
    )(span, soff, src, goff, delta, x1, pos, gate, y_rows, g_final)


def _attention_bias():
    slopes = jnp.exp2(-8.0 * jnp.arange(1, N_Q_HEADS + 1, dtype=jnp.float32) / N_Q_HEADS)
    qi = jnp.arange(BLOCK)[:, None]
    kj = jnp.arange(BAND)[None, :]
    dist = qi + BLOCK - kj
    valid = (dist >= 0) & (dist < WINDOW)
    b = -slopes[:, None, None] * dist.astype(jnp.float32)[None]
    b = jnp.where(valid[None], b, -jnp.inf)
    b = b.reshape(N_KV_HEADS, GROUP, BLOCK, BAND).transpose(0, 3, 1, 2)
    return b.reshape(N_KV_HEADS, BAND, GROUP * BLOCK)


def kernel(x, g_mix, w_in, conv_w, sinks, g_attn_out, g_conv_out, w_out, g_ffn, w_router, b_router,
           w1, b1, w2, b2, g_final):
    bsz, s_len, d = x.shape
    n_tok = bsz * s_len
    n_tiles = n_tok // ROUTE_TILE
    worst = n_tok * TOP_K + n_tiles * N_EXPERTS * (SUBLANES - 1) + N_EXPERTS * (ROW_BLOCK - SUBLANES)
    n_blocks = -(-worst // ROW_BLOCK)
    n_rows = n_blocks * ROW_BLOCK
    x2d = x.reshape(n_tok, d)
    bias = _attention_bias()
    l = 0
    wrt = w_router[l].T
    wr_hi = wrt.astype(jnp.bfloat16)
    wr_lo = (wrt - wr_hi.astype(jnp.float32)).astype(jnp.bfloat16)
    wr_split = jnp.concatenate([wr_hi, wr_lo], axis=0)
    x1, h2, pos, gate, cnt = _mixer_call(
        x2d, sinks[l], g_mix[l].reshape(1, d), w_in[l].astype(jnp.bfloat16), conv_w[l], bias,
        g_attn_out[l].reshape(1, ATTN_WIDTH), g_conv_out[l].reshape(1, CONV_WIDTH),
        w_out[l].astype(jnp.bfloat16), g_ffn[l].reshape(1, d), wr_split,
        b_router[l].reshape(N_EXPERTS, 1), s_len)
    size, goff, gdst, blk, used, span, soff, src, delta = _tables_call(cnt, n_blocks)
    size, goff, gdst = size.reshape(-1), goff.reshape(-1), gdst.reshape(-1)
    span, soff, src, delta = span.reshape(-1), soff.reshape(-1), src.reshape(-1), delta.reshape(-1)
    blk, valid, used = blk[0], blk[1], used.reshape(-1)[:1]
    rows = _dispatch_call(size, goff, gdst, used, h2, pos, n_rows)
    y_rows = _expert_call(blk, used, valid, rows, w1[l], b1[l], w2[l], b2[l])
    out = _combine_call(span, soff, src, goff, delta, x1, pos, gate, y_rows, g_final.reshape(1, d))
    return out.reshape(bsz, s_len, d)
```
